```python
import jax, jax.numpy as jnp
from jax import lax
import numpy as np

D_MODEL = 1024
BATCH = 4
SEQ = 4096
DEPTH = 2
DEC_BATCH = 128
DEC_SEQ = 8
PAST_LEN = 2048
PAGE_SIZE = 128

HEAD_DIM = 64
RW_DIM = D_MODEL // 4
RW_HEADS = RW_DIM // HEAD_DIM
W_LORA = 64
A_LORA = 64
G_LORA = 128
RW_PROJ = 3 * RW_DIM + W_LORA + A_LORA + G_LORA
CONV_DIM = D_MODEL // 4
CONV_GROUPS = CONV_DIM // HEAD_DIM
CONV_K = 3
SB_DIM = D_MODEL // 2
SB_HEADS = SB_DIM // HEAD_DIM
SB_BIAS_INIT = -8.0
MIX_DIM = RW_DIM + CONV_DIM + SB_DIM
IN_COLS = RW_PROJ + 3 * CONV_DIM + 3 * SB_DIM
D_FF = 4 * D_MODEL
PLE_DIM = 256
Q_BLOCK = 128
RMS_EPS = 1e-6
GN_EPS = 64e-5

kernel_name = 'hybrid_rwkv7_shortconv_stickbreak_step'


def rmsnorm(x, g):
    xf = x.astype(jnp.float32)
    y = xf * lax.rsqrt(jnp.mean(xf * xf, axis=-1, keepdims=True) + RMS_EPS)
    return (y * g).astype(x.dtype)


def wkv7_scan(r, decay, k, v, a_vec, b_vec, s0):
    def step(s, inp):
        r_t, w_t, k_t, v_t, a_t, b_t = inp
        sa = jnp.einsum('bhvk,bhk->bhv', s, a_t)
        s = (s * w_t[:, :, None, :] + sa[..., None] * b_t[:, :, None, :]
             + v_t[..., None] * k_t[:, :, None, :])
        y = jnp.einsum('bhvk,bhk->bhv', s, r_t)
        return s, y
    xs = tuple(jnp.moveaxis(t, 1, 0) for t in (r, decay, k, v, a_vec, b_vec))
    s_final, ys = lax.scan(step, s0, xs)
    return jnp.moveaxis(ys, 0, 1), s_final


def rwkv7_mixer(proj, prev_row, s0, lw):
    B, T, _ = proj.shape
    f32 = jnp.float32
    proj = proj.astype(f32)
    prev = jnp.concatenate([prev_row.astype(f32)[:, None], proj[:, :-1]], axis=1)
    xs = proj + (prev - proj) * lw['mu_shift']
    cuts = [RW_DIM, 2 * RW_DIM, 3 * RW_DIM, 3 * RW_DIM + W_LORA, 3 * RW_DIM + W_LORA + A_LORA]
    r, k, v, dw, da, dg = jnp.split(xs, cuts, axis=-1)
    w_log = -jax.nn.softplus(-(lw['w0'] + jnp.tanh(dw) @ lw['w2'])) - 0.5
    decay = jnp.exp(-jnp.exp(w_log))
    a = jax.nn.sigmoid(lw['a0'] + da @ lw['a2'])
    g = jax.nn.sigmoid(dg) @ lw['g2']
    hs = lambda t: t.reshape(B, T, RW_HEADS, HEAD_DIM)
    kk = hs(k * lw['k_k'])
    kk = kk * lax.rsqrt(jnp.sum(kk * kk, axis=-1, keepdims=True) + 1e-12)
    k = k * (1.0 + (a - 1.0) * lw['k_a'])
    rh, kh, vh, ah = hs(r), hs(k), hs(v), hs(a)
    y, s_fin = wkv7_scan(rh, hs(decay), kh, vh, -kk, kk * ah, s0.astype(f32))
    mu = jnp.mean(y, axis=-1, keepdims=True)
    var = jnp.mean(jnp.square(y - mu), axis=-1, keepdims=True)
    yn = ((y - mu) * lax.rsqrt(var + GN_EPS)).reshape(B, T, RW_DIM) * lw['gn_w'] + lw['gn_b']
    bonus = jnp.sum(rh * kh * lw['r_k'], axis=-1, keepdims=True) * vh
    out = (yn + bonus.reshape(B, T, RW_DIM)) * g
    return out, s_fin, proj[:, -1]


def short_conv_mixer(pb, pc, px, conv_prev, conv_w):
    T = px.shape[1]
    u = pc * px
    ext = jnp.concatenate([conv_prev.astype(u.dtype), u], axis=1)
    y = conv_w[0] * ext[:, 0:T]
    for j in range(1, CONV_K):
        y = y + conv_w[j] * ext[:, j:j + T]
    return pb * y, ext[:, -(CONV_K - 1):]


def stick_breaking(q, k, v, past_len, bias):
    B, T, H, d = q.shape
    S = k.shape[1]
    qb = Q_BLOCK if T % Q_BLOCK == 0 else T
    nb = T // qb
    scale = d ** -0.5
    kf = k.astype(jnp.float32)
    vf = v.astype(jnp.float32)
    bias_f = bias.astype(jnp.float32)[None, :, None, None]
    k_pos = jnp.arange(S)
    q_blocks = jnp.moveaxis(q.reshape(B, nb, qb, H, d), 1, 0)
    q_pos_blocks = (past_len + jnp.arange(T)).reshape(nb, qb)

    def block(args):
        q_blk, q_pos = args
        z = jnp.einsum('bqhd,bkhd->bhqk', q_blk.astype(jnp.float32), kf) * scale + bias_f
        mask = k_pos[None, :] < q_pos[:, None]
        log_beta = jnp.where(mask, jax.nn.log_sigmoid(z), -jnp.inf)
        log_keep = jnp.where(mask, jax.nn.log_sigmoid(-z), 0.0)
        later = lax.cumsum(log_keep, axis=3, reverse=True) - log_keep
        wts = jnp.exp(log_beta + later)
        return jnp.einsum('bhqk,bkhd->bqhd', wts, vf)

    out = lax.map(block, (q_blocks, q_pos_blocks))
    return jnp.moveaxis(out, 0, 1).reshape(B, T, H, d).astype(q.dtype)


def decoder_layer(x, p, past_len, shift_prev, wkv_prev, conv_prev, k_past, v_past, lw):
    B, T, _ = x.shape
    h = rmsnorm(x, lw['g_mix'])
    proj = h @ lw['w_in']
    p_rw, p_cv, p_sb = jnp.split(proj, [RW_PROJ, RW_PROJ + 3 * CONV_DIM], axis=-1)
    o_rw, wkv_new, shift_new = rwkv7_mixer(p_rw, shift_prev, wkv_prev, lw)
    cb, cc, cx = jnp.split(p_cv, 3, axis=-1)
    o_cv, conv_new = short_conv_mixer(cb, cc, cx, conv_prev, lw['conv_w'])
    q, k, v = [t.reshape(B, T, SB_HEADS, HEAD_DIM) for t in jnp.split(p_sb, 3, axis=-1)]
    q = rmsnorm(q, lw['q_gain'])
    k = rmsnorm(k, lw['k_gain'])
    if k_past is None:
        k_all, v_all = k, v
    else:
        k_all = jnp.concatenate([k_past.astype(k.dtype), k], axis=1)
        v_all = jnp.concatenate([v_past.astype(v.dtype), v], axis=1)
    o_sb = stick_breaking(q, k_all, v_all, past_len, lw['sb_bias']).reshape(B, T, SB_DIM)
    mixed = jnp.concatenate([o_rw.astype(x.dtype), o_cv.astype(x.dtype), o_sb.astype(x.dtype)], axis=-1)
    x = x + mixed @ lw['w_out']
    h2 = rmsnorm(x, lw['g_mlp'])
    x = x + jnp.square(jax.nn.relu(h2 @ lw['w_up'])) @ lw['w_down']
    gate = jax.nn.sigmoid(rmsnorm(x, lw['g_ple']) @ lw['w_ple_gate'])
    x = x + gate * (p @ lw['w_ple_proj'])
    return (x, k, v, wkv_new.astype(wkv_prev.dtype), shift_new.astype(shift_prev.dtype),
            conv_new.astype(conv_prev.dtype))


def gather_past(cache_l, page_table):
    pages = cache_l[page_table]
    b, n, ps, hh, d = pages.shape
    return pages.reshape(b, n * ps, hh, d)


def run_trunk(x, p, past_len, shift, wkv, conv, cache_k, cache_v, page_table, params):
    ks, vs, wkvs, shifts, convs = [], [], [], [], []
    for l in range(DEPTH):
        lw = {name: arr[l] for name, arr in params.items()}
        if cache_k is None:
            k_past, v_past = None, None
        else:
            k_past = gather_past(cache_k[l], page_table)
            v_past = gather_past(cache_v[l], page_table)
        x, k_new, v_new, wkv_new, shift_new, conv_new = decoder_layer(
            x, p[l], past_len, shift[l], wkv[l], conv[l], k_past, v_past, lw)
        ks.append(k_new)
        vs.append(v_new)
        wkvs.append(wkv_new)
        shifts.append(shift_new)
        convs.append(conv_new)
    return (x, jnp.stack(ks), jnp.stack(vs), jnp.stack(wkvs), jnp.stack(shifts), jnp.stack(convs))


def setup_inputs(seed: int = 0) -> dict:
    key = jax.random.key(seed)
    keys = iter(jax.random.split(key, 48))
    f32 = jnp.float32

    def nrm(shape, scale=1.0):
        return jax.random.normal(next(keys), shape, f32) * scale

    n_pages = PAST_LEN // PAGE_SIZE
    n_phys = (DEC_BATCH * n_pages * 5) // 4
    page_table = jax.random.permutation(next(keys), n_phys)[:DEC_BATCH * n_pages]
    page_table = page_table.reshape(DEC_BATCH, n_pages).astype(jnp.int32)
    return {
        'x_prompt': nrm((BATCH, SEQ, D_MODEL)),
        'x_sample': nrm((DEC_BATCH, DEC_SEQ, D_MODEL)),
        'cache_k': nrm((DEPTH, n_phys, PAGE_SIZE, SB_HEADS, HEAD_DIM)),
        'cache_v': nrm((DEPTH, n_phys, PAGE_SIZE, SB_HEADS, HEAD_DIM)),
        'state_wkv': nrm((DEPTH, DEC_BATCH, RW_HEADS, HEAD_DIM, HEAD_DIM), 0.5),
        'state_shift': nrm((DEPTH, DEC_BATCH, RW_PROJ)),
        'state_conv': nrm((DEPTH, DEC_BATCH, CONV_K - 1, CONV_DIM)),
        'page_table': page_table,
        'p_prompt': nrm((DEPTH, BATCH, SEQ, PLE_DIM)),
        'p_sample': nrm((DEPTH, DEC_BATCH, DEC_SEQ, PLE_DIM)),
        'g_mix': 1.0 + nrm((DEPTH, D_MODEL), 0.1),
        'w_in': nrm((DEPTH, D_MODEL, IN_COLS), D_MODEL ** -0.5),
        'mu_shift': jax.random.uniform(next(keys), (DEPTH, RW_PROJ), f32, 0.0, 1.0),
        'w0': nrm((DEPTH, RW_DIM), 0.5) - 1.0,
        'w2': nrm((DEPTH, W_LORA, RW_DIM), 0.5 * W_LORA ** -0.5),
        'a0': nrm((DEPTH, RW_DIM), 0.5),
        'a2': nrm((DEPTH, A_LORA, RW_DIM), A_LORA ** -0.5),
        'g2': nrm((DEPTH, G_LORA, RW_DIM), G_LORA ** -0.5),
        'k_k': 1.0 + nrm((DEPTH, RW_DIM), 0.1),
        'k_a': 1.0 + nrm((DEPTH, RW_DIM), 0.1),
        'r_k': nrm((DEPTH, RW_HEADS, HEAD_DIM), 0.1),
        'gn_w': 1.0 + nrm((DEPTH, RW_DIM), 0.1),
        'gn_b': nrm((DEPTH, RW_DIM), 0.01),
        'conv_w': nrm((DEPTH, CONV_K, CONV_DIM), CONV_K ** -0.5),
        'q_gain': 1.0 + nrm((DEPTH, HEAD_DIM), 0.1),
        'k_gain': 1.0 + nrm((DEPTH, HEAD_DIM), 0.1),
        'sb_bias': SB_BIAS_INIT + nrm((DEPTH, SB_HEADS), 0.1),
        'w_out': nrm((DEPTH, MIX_DIM, D_MODEL), MIX_DIM ** -0.5),
        'g_mlp': 1.0 + nrm((DEPTH, D_MODEL), 0.1),
        'w_up': nrm((DEPTH, D_MODEL, D_FF), D_MODEL ** -0.5),
        'w_down': nrm((DEPTH, D_FF, D_MODEL), D_FF ** -0.5),
        'g_ple': 1.0 + nrm((DEPTH, D_MODEL), 0.1),
        'w_ple_gate': nrm((DEPTH, D_MODEL, D_MODEL), D_MODEL ** -0.5),
        'w_ple_proj': nrm((DEPTH, PLE_DIM, D_MODEL), PLE_DIM ** -0.5),
    }


def reference(x_prompt, x_sample, cache_k, cache_v, state_wkv, state_shift, state_conv, page_table,
              p_prompt, p_sample, g_mix, w_in, mu_shift, w0, w2, a0, a2, g2, k_k, k_a, r_k,
              gn_w, gn_b, conv_w, q_gain, k_gain, sb_bias, w_out, g_mlp, w_up, w_down, g_ple,
              w_ple_gate, w_ple_proj):
    params = dict(g_mix=g_mix, w_in=w_in, mu_shift=mu_shift, w0=w0, w2=w2, a0=a0, a2=a2, g2=g2,
                  k_k=k_k, k_a=k_a, r_k=r_k, gn_w=gn_w, gn_b=gn_b, conv_w=conv_w,
                  q_gain=q_gain, k_gain=k_gain, sb_bias=sb_bias, w_out=w_out, g_mlp=g_mlp,
                  w_up=w_up, w_down=w_down, g_ple=g_ple, w_ple_gate=w_ple_gate,
                  w_ple_proj=w_ple_proj)
    dt = x_prompt.dtype
    shift0 = jnp.zeros((DEPTH, BATCH, RW_PROJ), dt)
    wkv0 = jnp.zeros((DEPTH, BATCH, RW_HEADS, HEAD_DIM, HEAD_DIM), dt)
    conv0 = jnp.zeros((DEPTH, BATCH, CONV_K - 1, CONV_DIM), dt)
    (y_prompt, k_prompt, v_prompt, wkv_prompt, shift_prompt, conv_prompt) = run_trunk(
        x_prompt, p_prompt, 0, shift0, wkv0, conv0, None, None, None, params)
    (y_sample, k_sample, v_sample, wkv_sample, shift_sample, conv_sample) = run_trunk(
        x_sample, p_sample, PAST_LEN, state_shift, state_wkv, state_conv,
        cache_k, cache_v, page_table, params)
    return (y_prompt, y_sample, k_prompt, v_prompt, wkv_prompt, shift_prompt, conv_prompt,
            k_sample, v_sample, wkv_sample, shift_sample, conv_sample)
```

```python
import functools

import jax
import jax.numpy as jnp
from jax import lax
from jax.experimental import pallas as pl
from jax.experimental.pallas import tpu as pltpu

F32 = jnp.float32
BF16 = jnp.bfloat16

HEAD_DIM = 64
RW_HEADS = 4
RW_DIM = RW_HEADS * HEAD_DIM
LORA_WA = 128
CONV_DIM = 256
CONV_K = 3
SB_HEADS = 8
SB_DIM = SB_HEADS * HEAD_DIM
PAGE_SIZE = 128
RMS_EPS = 1e-6
GN_EPS = 64e-5
KK_EPS = 1e-12

WKV_ROWS = 64
ATT_BLOCK = 256
VMEM_LIMIT_BYTES = 56 * 1024 * 1024


def _cparams(*sem):
    return pltpu.CompilerParams(dimension_semantics=sem, vmem_limit_bytes=VMEM_LIMIT_BYTES)


def _dot(a, b):
    return jnp.dot(a.astype(BF16), b.astype(BF16), preferred_element_type=F32)


def _dg(a, b, ca, cb):
    return lax.dot_general(a.astype(BF16), b.astype(BF16), (((ca,), (cb,)), ((), ())),
                           preferred_element_type=F32)


def _split2(a):
    hi = a.astype(BF16)
    lo = (a - hi.astype(F32)).astype(BF16)
    return hi, lo


def _dg3(a, b, ca=1, cb=0):
    ah, al = _split2(a)
    bh, bl = _split2(b)
    dn = (((ca,), (cb,)), ((), ()))
    out = lax.dot_general(ah, bh, dn, preferred_element_type=F32)
    out = out + lax.dot_general(ah, bl, dn, preferred_element_type=F32)
    out = out + lax.dot_general(al, bh, dn, preferred_element_type=F32)
    return out


def _dot_exact_rhs(a, b_bf16):
    h1 = a.astype(BF16)
    r1 = a - h1.astype(F32)
    h2 = r1.astype(BF16)
    h3 = (r1 - h2.astype(F32)).astype(BF16)
    out = jnp.dot(h1, b_bf16, preferred_element_type=F32)
    out = out + jnp.dot(h2, b_bf16, preferred_element_type=F32)
    out = out + jnp.dot(h3, b_bf16, preferred_element_type=F32)
    return out


def _dot_exact_lhs(a_bf16, b):
    h1 = b.astype(BF16)
    r1 = b - h1.astype(F32)
    h2 = r1.astype(BF16)
    h3 = (r1 - h2.astype(F32)).astype(BF16)
    out = jnp.dot(a_bf16, h1, preferred_element_type=F32)
    out = out + jnp.dot(a_bf16, h2, preferred_element_type=F32)
    out = out + jnp.dot(a_bf16, h3, preferred_element_type=F32)
    return out


def _softplus(x):
    return jnp.maximum(x, 0.0) + jnp.log1p(jnp.exp(-jnp.abs(x)))


def _sigmoid(x):
    return 1.0 / (1.0 + jnp.exp(-x))


def _rmsnorm_rows(x, g):
    ms = jnp.mean(x * x, axis=-1, keepdims=True)
    return x * lax.rsqrt(ms + RMS_EPS) * g


def _head_block_ones(width, scale):
    i = jnp.arange(width) // HEAD_DIM
    return jnp.where(i[:, None] == i[None, :], scale, 0.0).astype(BF16)


def _in_proj_kernel(x_ref, g_ref, w_ref, ebd_ref, qg_ref, kg_ref,
                    rw_ref, cv_ref, q_ref, k_ref, v_ref):
    h = _rmsnorm_rows(x_ref[...], g_ref[...]).astype(BF16)
    c0 = rw_ref.shape[1]
    c1 = c0 + cv_ref.shape[1]
    rw_ref[...] = jnp.dot(h, w_ref[:, 0:c0], preferred_element_type=F32)
    cv_ref[...] = jnp.dot(h, w_ref[:, c0:c1], preferred_element_type=F32)

    def head_norm(t, gain):
        ms = jnp.dot((t * t).astype(BF16), ebd_ref[...], preferred_element_type=F32)
        return t * lax.rsqrt(ms + RMS_EPS) * gain

    q = jnp.dot(h, w_ref[:, c1:c1 + SB_DIM], preferred_element_type=F32)
    q_ref[...] = (head_norm(q, qg_ref[...]) * (HEAD_DIM ** -0.5)).astype(q_ref.dtype)
    k = jnp.dot(h, w_ref[:, c1 + SB_DIM:c1 + 2 * SB_DIM], preferred_element_type=F32)
    k_ref[...] = head_norm(k, kg_ref[...])
    v_ref[...] = jnp.dot(h, w_ref[:, c1 + 2 * SB_DIM:c1 + 3 * SB_DIM], preferred_element_type=F32)


def _in_proj(x2d, g_mix, w_in_bf16, q_gain, k_gain, tm):
    n, d = x2d.shape
    cols = w_in_bf16.shape[1]
    rw_cols = cols - 3 * CONV_DIM - 3 * SB_DIM
    ebd = _head_block_ones(SB_DIM, 1.0 / HEAD_DIM)
    qg = jnp.tile(q_gain, SB_HEADS)[None, :]
    kg = jnp.tile(k_gain, SB_HEADS)[None, :]
    row = lambda w: pl.BlockSpec((tm, w), lambda i: (i, 0))
    full = lambda a: pl.BlockSpec(a.shape, lambda i: (0,) * a.ndim)
    g2d = g_mix[None, :]
    return pl.pallas_call(
        _in_proj_kernel,
        grid=(n // tm,),
        in_specs=[row(d), full(g2d), full(w_in_bf16), full(ebd), full(qg), full(kg)],
        out_specs=[row(rw_cols), row(3 * CONV_DIM), row(SB_DIM), row(SB_DIM), row(SB_DIM)],
        out_shape=[jax.ShapeDtypeStruct((n, rw_cols), F32),
                   jax.ShapeDtypeStruct((n, 3 * CONV_DIM), F32),
                   jax.ShapeDtypeStruct((n, SB_DIM), F32),
                   jax.ShapeDtypeStruct((n, SB_DIM), F32),
                   jax.ShapeDtypeStruct((n, SB_DIM), F32)],
        compiler_params=_cparams("parallel"),
        name="in_proj",
    )(x2d, g2d, w_in_bf16, ebd, qg, kg)


def _rwkv_pre_kernel(p_ref, prev_ref, mu_ref, w0_ref, a0_ref, kk_ref, ka_ref, rk_ref,
                     wwa_ref, g2_ref, ones_ref,
                     r_ref, lw_ref, k_ref, v_ref, av_ref, bv_ref, g_ref, bonus_ref):
    p = p_ref[...]
    xs = p + (prev_ref[...] - p) * mu_ref[...]
    r = xs[:, 0:RW_DIM]
    k = xs[:, RW_DIM:2 * RW_DIM]
    v = xs[:, 2 * RW_DIM:3 * RW_DIM]
    d = xs[:, 3 * RW_DIM:3 * RW_DIM + LORA_WA]
    dg = xs[:, 3 * RW_DIM + LORA_WA:]
    lane = lax.broadcasted_iota(jnp.int32, d.shape, 1)
    dwa = jnp.where(lane < LORA_WA // 2, jnp.tanh(d), d)
    wa = _dot(dwa, wwa_ref[...])
    w_log = -_softplus(-(w0_ref[...] + wa[:, 0:RW_DIM])) - 0.5
    lw_ref[...] = -jnp.exp(w_log)
    a = _sigmoid(a0_ref[...] + wa[:, RW_DIM:])
    g_ref[...] = _dot(_sigmoid(dg), g2_ref[...])
    kk = k * kk_ref[...]
    ss = _dot_exact_rhs(kk * kk, ones_ref[...])
    kk = kk * lax.rsqrt(ss + KK_EPS)
    k2 = k * (1.0 + (a - 1.0) * ka_ref[...])
    r_ref[...] = r
    k_ref[...] = k2
    v_ref[...] = v
    av_ref[...] = -kk
    bv_ref[...] = kk * a
    bonus_ref[...] = _dot_exact_rhs(r * k2 * rk_ref[...], ones_ref[...]) * v


def _rwkv_pre(p_rw, prev, lw, tm):
    n, c = p_rw.shape
    ones = _head_block_ones(RW_DIM, 1.0)
    w_lora = lw["w2"].shape[0]
    a_lora = lw["a2"].shape[0]
    wwa = jnp.zeros((w_lora + a_lora, 2 * RW_DIM), F32)
    wwa = wwa.at[:w_lora, :RW_DIM].set(lw["w2"]).at[w_lora:, RW_DIM:].set(lw["a2"]).astype(BF16)
    g2 = lw["g2"].astype(BF16)
    vec = lambda a: a.reshape(1, -1)
    smalls = [vec(lw["mu_shift"]), vec(lw["w0"]), vec(lw["a0"]), vec(lw["k_k"]), vec(lw["k_a"]),
              vec(lw["r_k"]), wwa, g2, ones]
    row = lambda w: pl.BlockSpec((tm, w), lambda i: (i, 0))
    full = lambda a: pl.BlockSpec(a.shape, lambda i: (0,) * a.ndim)
    return pl.pallas_call(
        _rwkv_pre_kernel,
        grid=(n // tm,),
        in_specs=[row(c), row(c)] + [full(a) for a in smalls],
        out_specs=[row(RW_DIM)] * 8,
        out_shape=[jax.ShapeDtypeStruct((n, RW_DIM), F32)] * 8,
        compiler_params=_cparams("parallel"),
        name="rwkv_pre",
    )(p_rw, prev, *smalls)


def _wkv_kernel(r_ref, lw_ref, k_ref, v_ref, a_ref, b_ref, s0_ref, y_ref, s_ref, *, nseq, tlen):
    P = WKV_ROWS
    HP = RW_HEADS * P
    c = pl.program_id(1)

    @pl.when(c == 0)
    def _():
        s_ref[...] = s0_ref[...]

    prow = lax.broadcasted_iota(jnp.int32, (P, P), 0)
    pcol = lax.broadcasted_iota(jnp.int32, (P, P), 1)
    same_seq = (prow // tlen) == (pcol // tlen)
    tri_incl = jnp.where(same_seq & (pcol <= prow), 1.0, 0.0).astype(BF16)
    srow = lax.broadcasted_iota(jnp.int32, (HP, HP), 0)
    scol = lax.broadcasted_iota(jnp.int32, (HP, HP), 1)
    same_blk = ((srow // tlen) == (scol // tlen))
    m_strict = same_blk & (scol < srow)
    m_incl = same_blk & (scol <= srow)
    hrow = lax.broadcasted_iota(jnp.int32, (HP, RW_DIM), 0) // P
    hlane = lax.broadcasted_iota(jnp.int32, (HP, RW_DIM), 1) // HEAD_DIM
    head_mask = hrow == hlane

    def stack(x):
        return jnp.where(head_mask, jnp.concatenate([x] * RW_HEADS, axis=0), 0.0)

    lw = lw_ref[0]
    cw = _dot_exact_lhs(tri_incl, lw)
    last_sel = jnp.where(same_seq & (pcol == (prow // tlen) * tlen + (tlen - 1)), 1.0, 0.0).astype(BF16)
    cw_end = _dot_exact_lhs(last_sel, cw)
    w_incl = jnp.exp(cw)
    w_excl = jnp.exp(cw - lw)
    w_inv = jnp.exp(-cw)
    w_toend = jnp.exp(cw_end - cw)
    kk = k_ref[0]
    bb = b_ref[0]
    a_s = stack(a_ref[0] * w_excl)
    r_s = stack(r_ref[0] * w_incl)
    b_s = stack(bb * w_inv)
    k_s = stack(kk * w_inv)
    v_s = stack(v_ref[0])
    bend_s = stack(bb * w_toend)
    kend_s = stack(kk * w_toend)

    n_ab = jnp.where(m_strict, _dg3(a_s, b_s, 1, 1), 0.0)
    a_ak = jnp.where(m_strict, _dg3(a_s, k_s, 1, 1), 0.0)
    a_rb = jnp.where(m_incl, _dg3(r_s, b_s, 1, 1), 0.0)
    a_rk = jnp.where(m_incl, _dg3(r_s, k_s, 1, 1), 0.0)

    eye = jnp.where(srow == scol, 1.0, 0.0)
    t_inv = eye + n_ab
    pw = n_ab
    span = 2
    while span < tlen:
        pw = _dg3(pw, pw)
        t_inv = t_inv + _dg3(t_inv, pw)
        span *= 2

    ap_s = _dg3(t_inv, a_s)
    u1_s = _dg3(t_inv, _dg3(a_ak, v_s))
    y1_s = _dg3(a_rb, u1_s) + _dg3(a_rk, v_s)
    rp_s = r_s + _dg3(a_rb, ap_s)

    wc_row = jnp.exp(cw_end)
    y_s = y1_s
    for i in range(nseq):
        s_i = s_ref[i]
        if nseq == 1:
            ap_i, rp_i, b_i, k_i, v_i, u1_i = ap_s, rp_s, bend_s, kend_s, v_s, u1_s
        else:
            in_seq = (lax.broadcasted_iota(jnp.int32, (HP, RW_DIM), 0) % P) // tlen == i
            sel = lambda x: jnp.where(in_seq, x, 0.0)
            ap_i, rp_i, b_i, k_i, v_i, u1_i = sel(ap_s), sel(rp_s), sel(bend_s), sel(kend_s), sel(v_s), sel(u1_s)
        u_i = _dg3(ap_i, s_i, 1, 1) + u1_i
        y_s = y_s + _dg3(rp_i, s_i, 1, 1)
        wc_i = wc_row[i * tlen:i * tlen + 1, :]
        s_ref[i] = s_i * wc_i + _dg3(u_i, b_i, 0, 0) + _dg3(v_i, k_i, 0, 0)
    y = y_s[0:P]
    for h in range(1, RW_HEADS):
        y = y + y_s[h * P:(h + 1) * P]
    y_ref[0] = y


def _wkv(r, lw, k, v, av, bv, s0_bd, nseq, tlen):
    g, rows, _ = r.shape
    n_chunks = rows // WKV_ROWS
    blk = pl.BlockSpec((1, WKV_ROWS, RW_DIM), lambda i, c: (i, c, 0))
    sblk = pl.BlockSpec((nseq, RW_DIM, RW_DIM), lambda i, c: (i, 0, 0))
    return pl.pallas_call(
        functools.partial(_wkv_kernel, nseq=nseq, tlen=tlen),
        grid=(g, n_chunks),
        in_specs=[blk] * 6 + [sblk],
        out_specs=[blk, sblk],
        out_shape=[jax.ShapeDtypeStruct(r.shape, F32), jax.ShapeDtypeStruct(s0_bd.shape, F32)],
        compiler_params=_cparams("parallel", "arbitrary"),
        name="wkv",
    )(r, lw, k, v, av, bv, s0_bd)


def _state_to_bd(s):
    b, h, n, _ = s.shape
    eye = jnp.eye(h, dtype=s.dtype)
    return jnp.einsum("bhvk,hg->bhvgk", s, eye).reshape(b, h * n, h * n)


def _state_from_bd(s_bd, h):
    b, hn, _ = s_bd.shape
    n = hn // h
    s5 = s_bd.reshape(b, h, n, h, n)
    return jnp.stack([s5[:, i, :, i, :] for i in range(h)], axis=1)


def _attn_block(k_blk, q_t, v_t, bias, carry, acc, tri, diag_mask):
    z = jnp.dot(k_blk, q_t, preferred_element_type=F32) + bias
    sp = _softplus(z)
    lk = -sp
    if diag_mask is not None:
        lk = jnp.where(diag_mask, lk, 0.0)
    lk_hi, lk_lo = _split2(lk)
    later = (jnp.dot(tri, lk_hi, preferred_element_type=F32)
             + jnp.dot(tri, lk_lo, preferred_element_type=F32))
    w = jnp.exp((z - sp) + later + carry)
    if diag_mask is not None:
        w = jnp.where(diag_mask, w, 0.0)
    acc = acc + jnp.dot(v_t, w.astype(BF16), preferred_element_type=F32)
    carry = carry + jnp.sum(lk, axis=0, keepdims=True)
    return carry, acc


def _attn_prompt_kernel(bias_ref, qt_ref, k_ref, vt_ref, o_ref):
    h = pl.program_id(1)
    qi = pl.program_id(2)
    tq = qt_ref.shape[3]
    tk = vt_ref.shape[4]
    bias = bias_ref[h]
    q_t = qt_ref[0, 0]
    krow = lax.broadcasted_iota(jnp.int32, (tk, tk), 0)
    kcol = lax.broadcasted_iota(jnp.int32, (tk, tk), 1)
    tri = jnp.where(kcol > krow, 1.0, 0.0).astype(BF16)
    diag_mask = (lax.broadcasted_iota(jnp.int32, (tk, tq), 0)
                 < lax.broadcasted_iota(jnp.int32, (tk, tq), 1))

    def load(j):
        start = pl.multiple_of(j * tk, tk)
        return k_ref[0, 0, pl.ds(start, tk), :], vt_ref[0, 0, j]

    k_blk, v_t = load(qi)
    carry = jnp.zeros((1, tq), F32)
    acc = jnp.zeros((HEAD_DIM, tq), F32)
    carry, acc = _attn_block(k_blk, q_t, v_t, bias, carry, acc, tri, diag_mask)

    def body(step, state):
        k_b, v_b = load(qi - 1 - step)
        return _attn_block(k_b, q_t, v_b, bias, state[0], state[1], tri, None)

    carry, acc = lax.fori_loop(0, qi, body, (carry, acc))
    o_ref[0, 0] = acc


def _attn_prompt(q, k, v, sb_bias, b, t):
    blk = min(ATT_BLOCK, t)
    nk = t // blk
    q_t = q.reshape(b, t, SB_HEADS, HEAD_DIM).transpose(0, 2, 3, 1).astype(BF16)
    k_h = k.reshape(b, t, SB_HEADS, HEAD_DIM).transpose(0, 2, 1, 3).astype(BF16)
    v_t = v.reshape(b, nk, blk, SB_HEADS, HEAD_DIM).transpose(0, 3, 1, 4, 2).astype(BF16)
    grid_spec = pltpu.PrefetchScalarGridSpec(
        num_scalar_prefetch=1,
        grid=(b, SB_HEADS, nk),
        in_specs=[pl.BlockSpec((1, 1, HEAD_DIM, blk), lambda i, h, j, bias: (i, h, 0, j)),
                  pl.BlockSpec((1, 1, t, HEAD_DIM), lambda i, h, j, bias: (i, h, 0, 0)),
                  pl.BlockSpec((1, 1, nk, HEAD_DIM, blk), lambda i, h, j, bias: (i, h, 0, 0, 0))],
        out_specs=pl.BlockSpec((1, 1, HEAD_DIM, blk), lambda i, h, j, bias: (i, h, 0, j)),
    )
    o_t = pl.pallas_call(
        _attn_prompt_kernel,
        grid_spec=grid_spec,
        out_shape=jax.ShapeDtypeStruct((b, SB_HEADS, HEAD_DIM, t), F32),
        compiler_params=_cparams("parallel", "parallel", "arbitrary"),
        name="attn_prompt",
    )(sb_bias, q_t, k_h, v_t)
    return o_t.transpose(0, 3, 1, 2).reshape(b * t, SB_DIM)


def _attn_sample_kernel(pt_ref, q_ref, bias_ref, kn_ref, vn_ref, kc_ref, vc_ref, o_ref, carry_ref, acc_ref,
                        *, tlen):
    j = pl.program_id(1)
    nj = pl.num_programs(1)
    rows = SB_HEADS * tlen
    rhead = lax.broadcasted_iota(jnp.int32, (rows, SB_DIM), 0) // tlen
    lhead = lax.broadcasted_iota(jnp.int32, (rows, SB_DIM), 1) // HEAD_DIM
    head_mask = rhead == lhead
    q_s = jnp.where(head_mask, jnp.concatenate([q_ref[0]] * SB_HEADS, axis=0), 0.0).astype(BF16)
    jrow = lax.broadcasted_iota(jnp.int32, (PAGE_SIZE, 2 * PAGE_SIZE), 0)
    jcol = lax.broadcasted_iota(jnp.int32, (PAGE_SIZE, 2 * PAGE_SIZE), 1)
    tri_ones = jnp.where((jrow > jcol) | (jcol >= PAGE_SIZE), 1.0, 0.0).astype(BF16)

    def sweep(k_page, v_page, mask):
        z = _dg(q_s, k_page, 1, 1) + bias_ref[...]
        sp = _softplus(z)
        lk = -sp
        if mask is not None:
            lk = jnp.where(mask, lk, 0.0)
        lk_hi, lk_lo = _split2(lk)
        sums = (jnp.dot(lk_hi, tri_ones, preferred_element_type=F32)
                + jnp.dot(lk_lo, tri_ones, preferred_element_type=F32))
        w = jnp.exp((z - sp) + sums[:, 0:PAGE_SIZE] + carry_ref[...])
        if mask is not None:
            w = jnp.where(mask, w, 0.0)
        acc_ref[...] += _dot(w, v_page)
        carry_ref[...] += sums[:, PAGE_SIZE:]

    @pl.when(j == 0)
    def _():
        carry_ref[...] = jnp.zeros_like(carry_ref)
        acc_ref[...] = jnp.zeros_like(acc_ref)
        pad = jnp.zeros((PAGE_SIZE - tlen, SB_DIM), F32)
        k_new = jnp.concatenate([kn_ref[0], pad], axis=0)
        v_new = jnp.concatenate([vn_ref[0], pad], axis=0)
        key_pos = lax.broadcasted_iota(jnp.int32, (rows, PAGE_SIZE), 1)
        q_pos = lax.broadcasted_iota(jnp.int32, (rows, PAGE_SIZE), 0) % tlen
        sweep(k_new, v_new, key_pos < q_pos)

    @pl.when(j > 0)
    def _():
        sweep(kc_ref[0], vc_ref[0], None)

    @pl.when(j == nj - 1)
    def _():
        acc = jnp.where(head_mask, acc_ref[...], 0.0)
        out = acc[0:tlen]
        for h in range(1, SB_HEADS):
            out = out + acc[h * tlen:(h + 1) * tlen]
        o_ref[0] = out


def _attn_sample(q, k_new, v_new, cache_k, cache_v, page_table, sb_bias, b, tlen):
    n_pages = page_table.shape[1]
    n_phys = cache_k.shape[0]
    kc = cache_k.reshape(n_phys, PAGE_SIZE, SB_DIM)
    vc = cache_v.reshape(n_phys, PAGE_SIZE, SB_DIM)
    rows = SB_HEADS * tlen
    bias_rows = jnp.broadcast_to(jnp.repeat(sb_bias, tlen)[:, None], (rows, PAGE_SIZE)).astype(F32)
    seq = lambda a: a.reshape(b, tlen, SB_DIM)

    def page_idx(i, j, pt):
        return (pt[i, n_pages - jnp.maximum(j, 1)], 0, 0)

    tok = pl.BlockSpec((1, tlen, SB_DIM), lambda i, j, pt: (i, 0, 0))
    grid_spec = pltpu.PrefetchScalarGridSpec(
        num_scalar_prefetch=1,
        grid=(b, n_pages + 1),
        in_specs=[tok,
                  pl.BlockSpec((rows, PAGE_SIZE), lambda i, j, pt: (0, 0)),
                  tok, tok,
                  pl.BlockSpec((1, PAGE_SIZE, SB_DIM), page_idx),
                  pl.BlockSpec((1, PAGE_SIZE, SB_DIM), page_idx)],
        out_specs=tok,
        scratch_shapes=[pltpu.VMEM((rows, PAGE_SIZE), F32), pltpu.VMEM((rows, SB_DIM), F32)],
    )
    out = pl.pallas_call(
        functools.partial(_attn_sample_kernel, tlen=tlen),
        grid_spec=grid_spec,
        out_shape=jax.ShapeDtypeStruct((b, tlen, SB_DIM), F32),
        compiler_params=_cparams("parallel", "arbitrary"),
        name="attn_sample",
    )(page_table, seq(q), bias_rows, seq(k_new), seq(v_new), kc, vc)
    return out.reshape(b * tlen, SB_DIM)


def _tail_kernel(x_ref, y_ref, bonus_ref, g_ref, cb_ref, cc_ref, cx_ref, cc1_ref, cx1_ref, cc2_ref, cx2_ref,
                 osb_ref, p_ref, gnw_ref, gnb_ref, convw_ref, ones_ref, wout_ref, gmlp_ref, wup_ref, wdown_ref,
                 gple_ref, wgate_ref, wproj_ref, o_ref, x1_ref, h2_ref, acc_ref):
    f = pl.program_id(1)
    nf = pl.num_programs(1)

    @pl.when(f == 0)
    def _():
        y = y_ref[...]
        mu = _dot_exact_rhs(y, ones_ref[...])
        yc = y - mu
        var = _dot_exact_rhs(yc * yc, ones_ref[...])
        yn = yc * lax.rsqrt(var + GN_EPS) * gnw_ref[...] + gnb_ref[...]
        o_rw = (yn + bonus_ref[...]) * g_ref[...]
        cw = convw_ref[...]
        conv = (cw[0:1] * (cc2_ref[...] * cx2_ref[...]) + cw[1:2] * (cc1_ref[...] * cx1_ref[...])
                + cw[2:3] * (cc_ref[...] * cx_ref[...]))
        o_cv = cb_ref[...] * conv
        mix = _dot(o_rw, wout_ref[0:RW_DIM, :])
        mix = mix + _dot(o_cv, wout_ref[RW_DIM:RW_DIM + CONV_DIM, :])
        mix = mix + _dot(osb_ref[...], wout_ref[RW_DIM + CONV_DIM:, :])
        x1 = x_ref[...] + mix
        x1_ref[...] = x1
        h2_ref[...] = _rmsnorm_rows(x1, gmlp_ref[...]).astype(BF16)
        acc_ref[...] = jnp.zeros_like(acc_ref)

    up = jnp.dot(h2_ref[...], wup_ref[...], preferred_element_type=F32)
    act = jnp.square(jnp.maximum(up, 0.0))
    acc_ref[...] += _dot(act, wdown_ref[...])

    @pl.when(f == nf - 1)
    def _():
        x2 = x1_ref[...] + acc_ref[...]
        gate = _sigmoid(_dot(_rmsnorm_rows(x2, gple_ref[...]), wgate_ref[...]))
        o_ref[...] = x2 + gate * _dot(p_ref[...], wproj_ref[...])


def _tail(x2d, y, bonus, g, p_cv, p_cv1, p_cv2, o_sb, p2d, lw, wts, tm, tf):
    n, d = x2d.shape
    d_ff = wts["w_up"].shape[1]
    ones = _head_block_ones(RW_DIM, 1.0 / HEAD_DIM)
    vec = lambda a: a.reshape(1, -1)
    row = lambda w, col=0: pl.BlockSpec((tm, w), lambda i, f, col=col: (i, col))
    full = lambda a: pl.BlockSpec(a.shape, lambda i, f: (0,) * a.ndim)
    gnw, gnb, gmlp, gple = vec(lw["gn_w"]), vec(lw["gn_b"]), vec(lw["g_mlp"]), vec(lw["g_ple"])
    in_specs = [row(d), row(RW_DIM), row(RW_DIM), row(RW_DIM),
                row(CONV_DIM, 0), row(CONV_DIM, 1), row(CONV_DIM, 2),
                row(CONV_DIM, 1), row(CONV_DIM, 2), row(CONV_DIM, 1), row(CONV_DIM, 2),
                row(SB_DIM), row(p2d.shape[1]),
                full(gnw), full(gnb), full(lw["conv_w"]), full(ones), full(wts["w_out"]), full(gmlp),
                pl.BlockSpec((d, tf), lambda i, f: (0, f)), pl.BlockSpec((tf, d), lambda i, f: (f, 0)),
                full(gple), full(wts["w_ple_gate"]), full(wts["w_ple_proj"])]
    return pl.pallas_call(
        _tail_kernel,
        grid=(n // tm, d_ff // tf),
        in_specs=in_specs,
        out_specs=pl.BlockSpec((tm, d), lambda i, f: (i, 0)),
        out_shape=jax.ShapeDtypeStruct((n, d), F32),
        scratch_shapes=[pltpu.VMEM((tm, d), F32), pltpu.VMEM((tm, d), BF16), pltpu.VMEM((tm, d), F32)],
        compiler_params=_cparams("parallel", "arbitrary"),
        name="tail",
    )(x2d, y, bonus, g, p_cv, p_cv, p_cv, p_cv1, p_cv1, p_cv2, p_cv2, o_sb, p2d,
      gnw, gnb, lw["conv_w"], ones, wts["w_out"], gmlp, wts["w_up"], wts["w_down"],
      gple, wts["w_ple_gate"], wts["w_ple_proj"])


def _conv_state_kernel(cv_ref, o_ref):
    o_ref[...] = cv_ref[:, :, CONV_DIM:2 * CONV_DIM] * cv_ref[:, :, 2 * CONV_DIM:3 * CONV_DIM]


def _conv_state(p_cv3):
    b, t, c = p_cv3.shape
    return pl.pallas_call(
        _conv_state_kernel,
        grid=(1,),
        in_specs=[pl.BlockSpec((b, 8, c), lambda i: (0, t // 8 - 1, 0))],
        out_specs=pl.BlockSpec((b, 8, CONV_DIM), lambda i: (0, 0, 0)),
        out_shape=jax.ShapeDtypeStruct((b, 8, CONV_DIM), F32),
        name="conv_state",
    )(p_cv3)


def _shift_rows(x3, first_rows, n):
    return jnp.concatenate([first_rows, x3[:, :-n]], axis=1)


def _layer(x2d, p2d, b, t, shift_prev, wkv_prev_bd, conv_prev, attn_fn, lw, wts, tm, tf, wkv_nseq, wkv_tlen):
    n, d = x2d.shape
    p_rw, p_cv, q, k_new, v_new = _in_proj(x2d, lw["g_mix"], wts["w_in"], lw["q_gain"], lw["k_gain"], tm)
    rw_cols = p_rw.shape[1]
    p_rw3 = p_rw.reshape(b, t, rw_cols)
    prev = _shift_rows(p_rw3, shift_prev[:, None, :], 1).reshape(n, rw_cols)
    r, lwd, k2, v, av, bv, g, bonus = _rwkv_pre(p_rw, prev, lw, tm)
    grp = lambda a: a.reshape(-1, (n // (b // wkv_nseq)), RW_DIM) if wkv_nseq > 1 else a.reshape(b, t, RW_DIM)
    y, s_bd = _wkv(grp(r), grp(lwd), grp(k2), grp(v), grp(av), grp(bv), wkv_prev_bd, wkv_nseq, wkv_tlen)
    y = y.reshape(n, RW_DIM)

    o_sb = attn_fn(q, k_new, v_new)

    p_cv3 = p_cv.reshape(b, t, 3 * CONV_DIM)
    ones = jnp.ones((b, 1, CONV_DIM), F32)
    state_rows = lambda s: jnp.concatenate([jnp.zeros_like(s), s, jnp.broadcast_to(ones, s.shape)], axis=-1)
    p_cv1 = _shift_rows(p_cv3, state_rows(conv_prev[:, 1:2]), 1).reshape(n, 3 * CONV_DIM)
    p_cv2 = _shift_rows(p_cv3, state_rows(conv_prev), 2).reshape(n, 3 * CONV_DIM)
    x_out = _tail(x2d, y, bonus, g, p_cv, p_cv1, p_cv2, o_sb, p2d, lw, wts, tm, tf)
    conv_new = _conv_state(p_cv3)[:, 8 - (CONV_K - 1):]
    return x_out, k_new, v_new, s_bd, p_rw3[:, -1], conv_new


def _run_trunk(x, p, shift, wkv, conv, attn_builder, params, bf16_w, tm, tf, wkv_nseq, wkv_tlen):
    b, t, d = x.shape
    depth = p.shape[0]
    x2d = x.reshape(b * t, d)
    ks, vs, wkvs, shifts, convs = [], [], [], [], []
    for l in range(depth):
        lw = {name: arr[l] for name, arr in params.items()}
        wts = {name: arr[l] for name, arr in bf16_w.items()}
        x2d, k_new, v_new, s_bd, shift_new, conv_new = _layer(
            x2d, p[l].reshape(b * t, -1), b, t, shift[l], _state_to_bd(wkv[l]), conv[l],
            attn_builder(l, lw), lw, wts, tm, tf, wkv_nseq, wkv_tlen)
        ks.append(k_new.reshape(b, t, SB_HEADS, HEAD_DIM))
        vs.append(v_new.reshape(b, t, SB_HEADS, HEAD_DIM))
        wkvs.append(_state_from_bd(s_bd, RW_HEADS))
        shifts.append(shift_new)
        convs.append(conv_new)
    return (x2d.reshape(b, t, d), jnp.stack(ks), jnp.stack(vs), jnp.stack(wkvs), jnp.stack(shifts),
            jnp.stack(convs))


def kernel(x_prompt, x_sample, cache_k, cache_v, state_wkv, state_shift, state_conv, page_table,
           p_prompt, p_sample, g_mix, w_in, mu_shift, w0, w2, a0, a2, g2, k_k, k_a, r_k,
           gn_w, gn_b, conv_w, q_gain, k_gain, sb_bias, w_out, g_mlp, w_up, w_down, g_ple,
           w_ple_gate, w_ple_proj):
    params = dict(g_mix=g_mix, mu_shift=mu_shift, w0=w0, w2=w2, a0=a0, a2=a2, g2=g2,
                  k_k=k_k, k_a=k_a, r_k=r_k, gn_w=gn_w, gn_b=gn_b, conv_w=conv_w,
                  q_gain=q_gain, k_gain=k_gain, sb_bias=sb_bias, g_mlp=g_mlp, g_ple=g_ple)
    bf16_w = dict(w_in=w_in.astype(BF16), w_out=w_out.astype(BF16), w_up=w_up.astype(BF16),
                  w_down=w_down.astype(BF16), w_ple_gate=w_ple_gate.astype(BF16),
                  w_ple_proj=w_ple_proj.astype(BF16))
    depth = w_in.shape[0]
    b, t, d = x_prompt.shape
    db, dt, _ = x_sample.shape
    rw_proj = state_shift.shape[-1]
    d_ff = w_up.shape[-1]
    tf = min(1024, d_ff)

    def prompt_attn(l, lw):
        return lambda q, k, v: _attn_prompt(q, k, v, lw["sb_bias"], b, t)

    def sample_attn(l, lw):
        return lambda q, k, v: _attn_sample(q, k, v, cache_k[l], cache_v[l], page_table, lw["sb_bias"], db, dt)

    dtp = x_prompt.dtype
    shift0 = jnp.zeros((depth, b, rw_proj), dtp)
    wkv0 = jnp.zeros((depth, b, RW_HEADS, HEAD_DIM, HEAD_DIM), dtp)
    conv0 = jnp.zeros((depth, b, CONV_K - 1, CONV_DIM), dtp)
    (y_prompt, k_prompt, v_prompt, wkv_prompt, shift_prompt, conv_prompt) = _run_trunk(
        x_prompt, p_prompt, shift0, wkv0, conv0, prompt_attn, params, bf16_w,
        min(512, b * t), tf, 1, WKV_ROWS)
    (y_sample, k_sample, v_sample, wkv_sample, shift_sample, conv_sample) = _run_trunk(
        x_sample, p_sample, state_shift, state_wkv, state_conv, sample_attn, params, bf16_w,
        min(512, db * dt), tf, WKV_ROWS // dt, dt)
    return (y_prompt, y_sample, k_prompt, v_prompt, wkv_prompt, shift_prompt, conv_prompt,
            k_sample, v_sample, wkv_sample, shift_sample, conv_sample)
```

```python
import functools

import jax
import jax.numpy as jnp
from jax import lax
from jax.experimental import pallas as pl
from jax.experimental.pallas import tpu as pltpu

F32 = jnp.float32
BF16 = jnp.bfloat16

HEAD_DIM = 64
RW_HEADS = 4
RW_DIM = RW_HEADS * HEAD_DIM
LORA_WA = 128
CONV_DIM = 256
CONV_K = 3
SB_HEADS = 8
SB_DIM = SB_HEADS * HEAD_DIM
PAGE_SIZE = 128
RMS_EPS = 1e-6
GN_EPS = 64e-5
KK_EPS = 1e-12
LOG2E = 1.4426950408889634

WKV_ROWS = 64
ATT_BLOCK = 256
ATT_HEADS = 4
VMEM_LIMIT_BYTES = 56 * 1024 * 1024


def _cparams(*sem):
    return pltpu.CompilerParams(dimension_semantics=sem, vmem_limit_bytes=VMEM_LIMIT_BYTES)


def _dot(a, b):
    return jnp.dot(a.astype(BF16), b.astype(BF16), preferred_element_type=F32)


def _dg(a, b, ca, cb):
    return lax.dot_general(a.astype(BF16), b.astype(BF16), (((ca,), (cb,)), ((), ())),
                           preferred_element_type=F32)


def _split2(a):
    hi = a.astype(BF16)
    lo = (a - hi.astype(F32)).astype(BF16)
    return hi, lo


def _dot_exact_rhs(a, b_bf16):
    h1 = a.astype(BF16)
    r1 = a - h1.astype(F32)
    h2 = r1.astype(BF16)
    h3 = (r1 - h2.astype(F32)).astype(BF16)
    out = jnp.dot(h1, b_bf16, preferred_element_type=F32)
    out = out + jnp.dot(h2, b_bf16, preferred_element_type=F32)
    out = out + jnp.dot(h3, b_bf16, preferred_element_type=F32)
    return out


def _dot_exact_lhs(a_bf16, b):
    h1 = b.astype(BF16)
    r1 = b - h1.astype(F32)
    h2 = r1.astype(BF16)
    h3 = (r1 - h2.astype(F32)).astype(BF16)
    out = jnp.dot(a_bf16, h1, preferred_element_type=F32)
    out = out + jnp.dot(a_bf16, h2, preferred_element_type=F32)
    out = out + jnp.dot(a_bf16, h3, preferred_element_type=F32)
    return out


def _softplus(x):
    return jnp.maximum(x, 0.0) + jnp.log1p(jnp.exp(-jnp.abs(x)))


def _sigmoid(x):
    return 1.0 / (1.0 + jnp.exp(-x))


def _rmsnorm_rows(x, g):
    ms = jnp.mean(x * x, axis=-1, keepdims=True)
    return x * lax.rsqrt(ms + RMS_EPS) * g


def _head_block_ones(width, scale):
    i = jnp.arange(width) // HEAD_DIM
    return jnp.where(i[:, None] == i[None, :], scale, 0.0).astype(BF16)


def _in_proj_kernel(x_ref, g_ref, w_ref, ebd_ref, qg_ref, kg_ref,
                    rw_ref, cv_ref, k_ref, v_ref, *attn_refs, att_blk):
    h = _rmsnorm_rows(x_ref[...], g_ref[...]).astype(BF16)
    c0 = rw_ref.shape[1]
    c1 = c0 + cv_ref.shape[1]
    rw_ref[...] = jnp.dot(h, w_ref[:, 0:c0], preferred_element_type=F32)
    cv_ref[...] = jnp.dot(h, w_ref[:, c0:c1], preferred_element_type=F32)

    def head_norm(t, gain):
        ms = jnp.dot((t * t).astype(BF16), ebd_ref[...], preferred_element_type=F32)
        return t * lax.rsqrt(ms + RMS_EPS) * gain

    q = jnp.dot(h, w_ref[:, c1:c1 + SB_DIM], preferred_element_type=F32)
    q = head_norm(q, qg_ref[...]) * (HEAD_DIM ** -0.5)
    k = jnp.dot(h, w_ref[:, c1 + SB_DIM:c1 + 2 * SB_DIM], preferred_element_type=F32)
    k = head_norm(k, kg_ref[...])
    k_ref[...] = k
    v = jnp.dot(h, w_ref[:, c1 + 2 * SB_DIM:c1 + 3 * SB_DIM], preferred_element_type=F32)
    v_ref[...] = v
    if att_blk is None:
        (q_ref,) = attn_refs
        q_ref[...] = q
    else:
        qt_ref, kb_ref, vt_ref = attn_refs
        kb_ref[...] = k.astype(BF16)
        q_t = (q * LOG2E).T
        v_t = v.T
        for s in range(qt_ref.shape[1]):
            qt_ref[0, s] = q_t[:, s * att_blk:(s + 1) * att_blk].astype(BF16)
            vt_ref[0, s] = v_t[:, s * att_blk:(s + 1) * att_blk].astype(BF16)


def _in_proj(x2d, g_mix, w_in_bf16, q_gain, k_gain, tm, seq_len, att_blk):
    n, d = x2d.shape
    cols = w_in_bf16.shape[1]
    rw_cols = cols - 3 * CONV_DIM - 3 * SB_DIM
    ebd = _head_block_ones(SB_DIM, 1.0 / HEAD_DIM)
    qg = jnp.tile(q_gain, SB_HEADS)[None, :]
    kg = jnp.tile(k_gain, SB_HEADS)[None, :]
    row = lambda w: pl.BlockSpec((tm, w), lambda i: (i, 0))
    full = lambda a: pl.BlockSpec(a.shape, lambda i: (0,) * a.ndim)
    g2d = g_mix[None, :]
    out_specs = [row(rw_cols), row(3 * CONV_DIM), row(SB_DIM), row(SB_DIM)]
    out_shape = [jax.ShapeDtypeStruct((n, rw_cols), F32), jax.ShapeDtypeStruct((n, 3 * CONV_DIM), F32),
                 jax.ShapeDtypeStruct((n, SB_DIM), F32), jax.ShapeDtypeStruct((n, SB_DIM), F32)]
    if att_blk is None:
        out_specs.append(row(SB_DIM))
        out_shape.append(jax.ShapeDtypeStruct((n, SB_DIM), F32))
    else:
        tiles_per_seq = seq_len // tm
        blks = tm // att_blk
        t_spec = pl.BlockSpec((1, blks, SB_DIM, att_blk), lambda i: (i // tiles_per_seq, i % tiles_per_seq, 0, 0))
        t_shape = jax.ShapeDtypeStruct((n // seq_len, seq_len // att_blk, SB_DIM, att_blk), BF16)
        out_specs += [t_spec, row(SB_DIM), t_spec]
        out_shape += [t_shape, jax.ShapeDtypeStruct((n, SB_DIM), BF16), t_shape]
    return pl.pallas_call(
        functools.partial(_in_proj_kernel, att_blk=att_blk),
        grid=(n // tm,),
        in_specs=[row(d), full(g2d), full(w_in_bf16), full(ebd), full(qg), full(kg)],
        out_specs=out_specs,
        out_shape=out_shape,
        compiler_params=_cparams("parallel"),
        name="in_proj",
    )(x2d, g2d, w_in_bf16, ebd, qg, kg)


def _rwkv_pre_kernel(p_ref, prev_ref, mu_ref, w0_ref, a0_ref, kk_ref, ka_ref, rk_ref,
                     wwa_ref, g2_ref, ones_ref,
                     r_ref, lw_ref, k_ref, v_ref, av_ref, bv_ref, g_ref, bonus_ref):
    p = p_ref[...]
    xs = p + (prev_ref[...] - p) * mu_ref[...]
    r = xs[:, 0:RW_DIM]
    k = xs[:, RW_DIM:2 * RW_DIM]
    v = xs[:, 2 * RW_DIM:3 * RW_DIM]
    d = xs[:, 3 * RW_DIM:3 * RW_DIM + LORA_WA]
    dg = xs[:, 3 * RW_DIM + LORA_WA:]
    lane = lax.broadcasted_iota(jnp.int32, d.shape, 1)
    dwa = jnp.where(lane < LORA_WA // 2, jnp.tanh(d), d)
    wa = _dot(dwa, wwa_ref[...])
    w_log = -_softplus(-(w0_ref[...] + wa[:, 0:RW_DIM])) - 0.5
    lw_ref[...] = -jnp.exp(w_log)
    a = _sigmoid(a0_ref[...] + wa[:, RW_DIM:])
    g_ref[...] = _dot(_sigmoid(dg), g2_ref[...])
    kk = k * kk_ref[...]
    ss = _dot_exact_rhs(kk * kk, ones_ref[...])
    kk = kk * lax.rsqrt(ss + KK_EPS)
    k2 = k * (1.0 + (a - 1.0) * ka_ref[...])
    r_ref[...] = r
    k_ref[...] = k2
    v_ref[...] = v
    av_ref[...] = -kk
    bv_ref[...] = kk * a
    bonus_ref[...] = _dot_exact_rhs(r * k2 * rk_ref[...], ones_ref[...]) * v


def _rwkv_pre(p_rw, prev, lw, tm):
    n, c = p_rw.shape
    ones = _head_block_ones(RW_DIM, 1.0)
    w_lora = lw["w2"].shape[0]
    a_lora = lw["a2"].shape[0]
    wwa = jnp.zeros((w_lora + a_lora, 2 * RW_DIM), F32)
    wwa = wwa.at[:w_lora, :RW_DIM].set(lw["w2"]).at[w_lora:, RW_DIM:].set(lw["a2"]).astype(BF16)
    g2 = lw["g2"].astype(BF16)
    vec = lambda a: a.reshape(1, -1)
    smalls = [vec(lw["mu_shift"]), vec(lw["w0"]), vec(lw["a0"]), vec(lw["k_k"]), vec(lw["k_a"]),
              vec(lw["r_k"]), wwa, g2, ones]
    row = lambda w: pl.BlockSpec((tm, w), lambda i: (i, 0))
    full = lambda a: pl.BlockSpec(a.shape, lambda i: (0,) * a.ndim)
    return pl.pallas_call(
        _rwkv_pre_kernel,
        grid=(n // tm,),
        in_specs=[row(c), row(c)] + [full(a) for a in smalls],
        out_specs=[row(RW_DIM)] * 8,
        out_shape=[jax.ShapeDtypeStruct((n, RW_DIM), F32)] * 8,
        compiler_params=_cparams("parallel"),
        name="rwkv_pre",
    )(p_rw, prev, *smalls)


def _wkv_kernel(r_ref, lw_ref, k_ref, v_ref, a_ref, b_ref, s0_ref, y_ref, s_ref, *, nseq, tlen):
    P = WKV_ROWS
    HP = RW_HEADS * P
    c = pl.program_id(1)

    @pl.when(c == 0)
    def _():
        s_ref[...] = s0_ref[...]

    prow = lax.broadcasted_iota(jnp.int32, (P, P), 0)
    pcol = lax.broadcasted_iota(jnp.int32, (P, P), 1)
    same_seq = (prow // tlen) == (pcol // tlen)
    tri_incl = jnp.where(same_seq & (pcol <= prow), 1.0, 0.0).astype(BF16)
    last_sel = jnp.where(same_seq & (pcol == (prow // tlen) * tlen + (tlen - 1)), 1.0, 0.0).astype(BF16)
    srow = lax.broadcasted_iota(jnp.int32, (HP, HP), 0)
    scol = lax.broadcasted_iota(jnp.int32, (HP, HP), 1)
    same_blk = ((srow // tlen) == (scol // tlen))
    m_strict = same_blk & (scol < srow)
    m_incl = same_blk & (scol <= srow)
    eye = jnp.where(srow == scol, 1.0, 0.0)
    hrow = lax.broadcasted_iota(jnp.int32, (HP, RW_DIM), 0)
    head_mask = (hrow // P) == (lax.broadcasted_iota(jnp.int32, (HP, RW_DIM), 1) // HEAD_DIM)
    seq_of_row = (hrow % P) // tlen

    def stack(x):
        return jnp.where(head_mask, jnp.concatenate([x] * RW_HEADS, axis=0), 0.0)

    groups = range(r_ref.shape[0])
    lw = [lw_ref[g] for g in groups]
    cw = [_dot_exact_lhs(tri_incl, x) for x in lw]
    cw_end = [_dot_exact_lhs(last_sel, x) for x in cw]
    w_inv = [jnp.exp(-x) for x in cw]
    w_toend = [jnp.exp(e - x) for e, x in zip(cw_end, cw)]
    a_s = [stack(a_ref[g] * jnp.exp(cw[g] - lw[g])) for g in groups]
    r_s = [stack(r_ref[g] * jnp.exp(cw[g])) for g in groups]
    b_s = [stack(b_ref[g] * w_inv[g]) for g in groups]
    k_s = [stack(k_ref[g] * w_inv[g]) for g in groups]
    v_s = [stack(v_ref[g]) for g in groups]
    bend_s = [stack(b_ref[g] * w_toend[g]) for g in groups]
    kend_s = [stack(k_ref[g] * w_toend[g]) for g in groups]

    n_ab = [jnp.where(m_strict, _dg(a_s[g], b_s[g], 1, 1), 0.0) for g in groups]
    a_ak = [jnp.where(m_strict, _dg(a_s[g], k_s[g], 1, 1), 0.0) for g in groups]
    a_rb = [jnp.where(m_incl, _dg(r_s[g], b_s[g], 1, 1), 0.0) for g in groups]
    a_rk = [jnp.where(m_incl, _dg(r_s[g], k_s[g], 1, 1), 0.0) for g in groups]

    t_inv = [eye + x for x in n_ab]
    pw = n_ab
    span = 2
    while span < tlen:
        pw = [_dot(x, x) for x in pw]
        t_inv = [t + _dot(t, x) for t, x in zip(t_inv, pw)]
        span *= 2

    ap_s = [_dot(t_inv[g], a_s[g]) for g in groups]
    akv = [_dot(a_ak[g], v_s[g]) for g in groups]
    u1_s = [_dot(t_inv[g], akv[g]) for g in groups]
    y_s = [_dot(a_rb[g], u1_s[g]) + _dot(a_rk[g], v_s[g]) for g in groups]
    rp_s = [r_s[g] + _dot(a_rb[g], ap_s[g]) for g in groups]
    wc_row = [jnp.exp(x) for x in cw_end]

    for i in range(nseq):
        if nseq == 1:
            sel = lambda x: x
        else:
            sel = lambda x: jnp.where(seq_of_row == i, x, 0.0)
        s_i = [s_ref[g * nseq + i] for g in groups]
        u_i = [_dg(sel(ap_s[g]), s_i[g], 1, 1) + sel(u1_s[g]) for g in groups]
        y_s = [y_s[g] + _dg(sel(rp_s[g]), s_i[g], 1, 1) for g in groups]
        for g in groups:
            wc_i = wc_row[g][i * tlen:i * tlen + 1, :]
            s_ref[g * nseq + i] = (s_i[g] * wc_i + _dg(u_i[g], sel(bend_s[g]), 0, 0)
                                   + _dg(sel(v_s[g]), sel(kend_s[g]), 0, 0))
    for g in groups:
        y = y_s[g][0:P]
        for h in range(1, RW_HEADS):
            y = y + y_s[g][h * P:(h + 1) * P]
        y_ref[g] = y


def _wkv(r, lw, k, v, av, bv, s0_bd, nseq, tlen, groups):
    g, rows, _ = r.shape
    n_chunks = rows // WKV_ROWS
    blk = pl.BlockSpec((groups, WKV_ROWS, RW_DIM), lambda i, c: (i, c, 0))
    sblk = pl.BlockSpec((groups * nseq, RW_DIM, RW_DIM), lambda i, c: (i, 0, 0))
    return pl.pallas_call(
        functools.partial(_wkv_kernel, nseq=nseq, tlen=tlen),
        grid=(g // groups, n_chunks),
        in_specs=[blk] * 6 + [sblk],
        out_specs=[blk, sblk],
        out_shape=[jax.ShapeDtypeStruct(r.shape, F32), jax.ShapeDtypeStruct(s0_bd.shape, F32)],
        compiler_params=_cparams("parallel", "arbitrary"),
        name="wkv",
    )(r, lw, k, v, av, bv, s0_bd)


def _state_to_bd(s):
    b, h, n, _ = s.shape
    eye = jnp.eye(h, dtype=s.dtype)
    return jnp.einsum("bhvk,hg->bhvgk", s, eye).reshape(b, h * n, h * n)


def _state_from_bd(s_bd, h):
    b, hn, _ = s_bd.shape
    n = hn // h
    s5 = s_bd.reshape(b, h, n, h, n)
    return jnp.stack([s5[:, i, :, i, :] for i in range(h)], axis=1)


ATT_NEG = -1e30


def _attn_prompt_kernel(bias_ref, tq_ref, tj_ref, qt_ref, k_ref, vt_ref, o_ref, z_ref, lw_ref, acc_ref, carry_ref):
    grp = pl.program_id(1)
    nk, gw, blk = qt_ref.shape[1:]
    n_tiles = nk * (nk + 1) // 2
    row_head = lax.broadcasted_iota(jnp.int32, (gw, blk), 0) // HEAD_DIM
    offs = lax.broadcasted_iota(jnp.int32, (blk, blk), 0) - lax.broadcasted_iota(jnp.int32, (blk, blk), 1)
    krow = lax.broadcasted_iota(jnp.int32, (blk, blk), 0)
    kcol = lax.broadcasted_iota(jnp.int32, (blk, blk), 1)
    neg_tri = jnp.where(kcol > krow, -1.0, 0.0).astype(BF16)
    z_ref[...] = jnp.zeros_like(z_ref)
    lw_ref[...] = jnp.full(lw_ref.shape, ATT_NEG, F32)
    acc_ref[...] = jnp.zeros_like(acc_ref)
    carry_ref[...] = jnp.zeros_like(carry_ref)

    def body(i, _):
        slot_new = i % 2
        slot_old = 1 - slot_new

        f3 = jnp.maximum(i - 2, 0)
        q3 = tq_ref[f3]
        j3 = tj_ref[f3]
        keep = jnp.where(j3 == q3, 0.0, 1.0)
        for h in range(ATT_HEADS):
            rows = slice(h * HEAD_DIM, (h + 1) * HEAD_DIM)
            w = jnp.exp2(lw_ref[slot_old, h]).astype(BF16)
            pv = jnp.dot(vt_ref[0, j3, rows, :], w, preferred_element_type=F32)
            acc_ref[rows, :] = acc_ref[rows, :] * keep + pv
        o_ref[0, q3] = acc_ref[...]

        f2 = jnp.clip(i - 1, 0, n_tiles - 1)
        valid2 = (i >= 1) & (i <= n_tiles)
        q2 = tq_ref[f2]
        j2 = tj_ref[f2]
        visible = offs < (q2 - j2) * blk
        first2 = j2 == q2
        null2 = jnp.where(valid2, 0.0, ATT_NEG)
        count2 = jnp.where(valid2, 1.0, 0.0)
        for h in range(ATT_HEADS):
            z = jnp.where(visible, z_ref[slot_old, h] + bias_ref[grp * ATT_HEADS + h] * LOG2E, ATT_NEG)
            sp = jnp.maximum(z, 0.0) + jnp.log(1.0 + jnp.exp2(-jnp.abs(z))) * LOG2E
            later = jnp.dot(neg_tri, sp.astype(BF16), preferred_element_type=F32)
            carry = jnp.where(first2, 0.0, carry_ref[h])
            lw_ref[slot_new, h] = (z - sp) + later + (carry + null2)
            carry_ref[h] = carry - jnp.sum(sp, axis=0, keepdims=True) * count2

        f1 = jnp.minimum(i, n_tiles - 1)
        q1 = tq_ref[f1]
        j1 = tj_ref[f1]
        k_blk = k_ref[0, pl.ds(pl.multiple_of(j1 * blk, blk), blk), :]
        q_grp = qt_ref[0, q1]
        for h in range(ATT_HEADS):
            q_h = jnp.where(row_head == h, q_grp, jnp.zeros_like(q_grp))
            z_ref[slot_new, h] = jnp.dot(k_blk, q_h, preferred_element_type=F32)
        return 0

    lax.fori_loop(0, n_tiles + 2, body, 0)


def _attn_prompt(q_t, k_bf16, v_t, sb_bias, b, t):
    _, nk, _, blk = q_t.shape
    gw = ATT_HEADS * HEAD_DIM
    k3 = k_bf16.reshape(b, t, SB_DIM)
    tiles = [(qi, j) for qi in range(nk) for j in range(qi, -1, -1)]
    tile_q = jnp.array([qi for qi, _ in tiles], jnp.int32)
    tile_j = jnp.array([j for _, j in tiles], jnp.int32)
    blocks = pl.BlockSpec((1, nk, gw, blk), lambda i, g, *_: (i, 0, g, 0))
    grid_spec = pltpu.PrefetchScalarGridSpec(
        num_scalar_prefetch=3,
        grid=(b, SB_HEADS // ATT_HEADS),
        in_specs=[blocks, pl.BlockSpec((1, t, gw), lambda i, g, *_: (i, 0, g)), blocks],
        out_specs=blocks,
        scratch_shapes=[pltpu.VMEM((2, ATT_HEADS, blk, blk), F32), pltpu.VMEM((2, ATT_HEADS, blk, blk), F32),
                        pltpu.VMEM((gw, blk), F32), pltpu.VMEM((ATT_HEADS, 1, blk), F32)],
    )
    return pl.pallas_call(
        _attn_prompt_kernel,
        grid_spec=grid_spec,
        out_shape=jax.ShapeDtypeStruct((b, nk, SB_DIM, blk), F32),
        compiler_params=_cparams("parallel", "parallel"),
        name="attn_prompt",
    )(sb_bias, tile_q, tile_j, q_t, k3, v_t)


def _attn_sample_kernel(pt_ref, q_ref, bias_ref, kn_ref, vn_ref, *refs, tlen, n_pages):
    k_refs = refs[:n_pages]
    v_refs = refs[n_pages:2 * n_pages]
    o_ref = refs[2 * n_pages]
    rows = SB_HEADS * tlen
    rhead = lax.broadcasted_iota(jnp.int32, (rows, SB_DIM), 0) // tlen
    lhead = lax.broadcasted_iota(jnp.int32, (rows, SB_DIM), 1) // HEAD_DIM
    head_mask = rhead == lhead
    q_s = jnp.where(head_mask, jnp.concatenate([q_ref[0]] * SB_HEADS, axis=0), 0.0).astype(BF16)
    jrow = lax.broadcasted_iota(jnp.int32, (PAGE_SIZE, 2 * PAGE_SIZE), 0)
    jcol = lax.broadcasted_iota(jnp.int32, (PAGE_SIZE, 2 * PAGE_SIZE), 1)
    tri_ones = jnp.where((jrow > jcol) | (jcol >= PAGE_SIZE), 1.0, 0.0).astype(BF16)
    bias = bias_ref[...]

    def scores(z, mask):
        z = z + bias
        sp = jnp.maximum(z, 0.0) + jnp.log(1.0 + jnp.exp(-jnp.abs(z)))
        lk = -sp
        if mask is not None:
            lk = jnp.where(mask, lk, 0.0)
        lk_hi, lk_lo = _split2(lk)
        sums = (jnp.dot(lk_hi, tri_ones, preferred_element_type=F32)
                + jnp.dot(lk_lo, tri_ones, preferred_element_type=F32))
        return (z - sp) + sums[:, 0:PAGE_SIZE], sums[:, PAGE_SIZE:]

    pad = jnp.zeros((PAGE_SIZE - tlen, SB_DIM), F32)
    k_new = jnp.concatenate([kn_ref[0], pad], axis=0)
    v_new = jnp.concatenate([vn_ref[0], pad], axis=0)
    new_mask = (lax.broadcasted_iota(jnp.int32, (rows, PAGE_SIZE), 1)
                < lax.broadcasted_iota(jnp.int32, (rows, PAGE_SIZE), 0) % tlen)
    order = list(reversed(range(n_pages)))
    zs = [_dg(q_s, k_new, 1, 1)] + [_dot(q_s, k_refs[j][0, 0]) for j in order]
    parts = [scores(zs[0], new_mask)] + [scores(z, None) for z in zs[1:]]
    carry = jnp.zeros((rows, PAGE_SIZE), F32)
    ws = []
    for n, (logw, total) in enumerate(parts):
        w = jnp.exp(logw + carry)
        ws.append(jnp.where(new_mask, w, 0.0) if n == 0 else w)
        carry = carry + total
    acc = _dot(ws[0], v_new)
    for w, j in zip(ws[1:], order):
        acc = acc + _dg(w, v_refs[j][0, 0], 1, 1)
    acc = jnp.where(head_mask, acc, 0.0)
    out = acc[0:tlen]
    for h in range(1, SB_HEADS):
        out = out + acc[h * tlen:(h + 1) * tlen]
    o_ref[0] = out


def _attn_sample(q, k_new, v_new, cache_kt, cache_vt, layer, page_table, sb_bias, b, tlen):
    n_pages = page_table.shape[1]
    rows = SB_HEADS * tlen
    bias_rows = jnp.broadcast_to(jnp.repeat(sb_bias, tlen)[:, None], (rows, PAGE_SIZE)).astype(F32)
    seq = lambda a: a.reshape(b, tlen, SB_DIM)
    tok = pl.BlockSpec((1, tlen, SB_DIM), lambda i, pt: (i, 0, 0))
    page = lambda j: pl.BlockSpec((1, 1, SB_DIM, PAGE_SIZE), lambda i, pt, j=j: (layer, pt[i, j], 0, 0))
    pages = [page(j) for j in range(n_pages)]
    grid_spec = pltpu.PrefetchScalarGridSpec(
        num_scalar_prefetch=1,
        grid=(b,),
        in_specs=[tok, pl.BlockSpec((rows, PAGE_SIZE), lambda i, pt: (0, 0)), tok, tok] + pages + pages,
        out_specs=tok,
    )
    out = pl.pallas_call(
        functools.partial(_attn_sample_kernel, tlen=tlen, n_pages=n_pages),
        grid_spec=grid_spec,
        out_shape=jax.ShapeDtypeStruct((b, tlen, SB_DIM), F32),
        compiler_params=_cparams("parallel"),
        name="attn_sample",
    )(page_table, seq(q), bias_rows, seq(k_new), seq(v_new), *([cache_kt] * n_pages), *([cache_vt] * n_pages))
    return out.reshape(b * tlen, SB_DIM)


def _tail_kernel(x_ref, y_ref, bonus_ref, g_ref, cb_ref, cc_ref, cx_ref, cc1_ref, cx1_ref, cc2_ref, cx2_ref,
                 osb_ref, p_ref, gnw_ref, gnb_ref, convw_ref, ones_ref, wout_ref, gmlp_ref, wup_ref, wdown_ref,
                 gple_ref, wgate_ref, wproj_ref, o_ref, x1_ref, h2_ref, acc_ref):
    f = pl.program_id(1)
    nf = pl.num_programs(1)

    @pl.when(f == 0)
    def _():
        y = y_ref[...]
        mu = _dot_exact_rhs(y, ones_ref[...])
        yc = y - mu
        var = _dot_exact_rhs(yc * yc, ones_ref[...])
        yn = yc * lax.rsqrt(var + GN_EPS) * gnw_ref[...] + gnb_ref[...]
        o_rw = (yn + bonus_ref[...]) * g_ref[...]
        cw = convw_ref[...]
        conv = (cw[0:1] * (cc2_ref[...] * cx2_ref[...]) + cw[1:2] * (cc1_ref[...] * cx1_ref[...])
                + cw[2:3] * (cc_ref[...] * cx_ref[...]))
        o_cv = cb_ref[...] * conv
        mix = _dot(o_rw, wout_ref[0:RW_DIM, :])
        mix = mix + _dot(o_cv, wout_ref[RW_DIM:RW_DIM + CONV_DIM, :])
        w_sb = wout_ref[RW_DIM + CONV_DIM:, :]
        if len(osb_ref.shape) == 2:
            mix = mix + _dot(osb_ref[...], w_sb)
        else:
            mix = mix + jnp.concatenate(
                [_dot(osb_ref[0, s].T, w_sb) for s in range(osb_ref.shape[1])], axis=0)
        x1 = x_ref[...] + mix
        x1_ref[...] = x1
        h2_ref[...] = _rmsnorm_rows(x1, gmlp_ref[...]).astype(BF16)
        acc_ref[...] = jnp.zeros_like(acc_ref)

    up = jnp.dot(h2_ref[...], wup_ref[...], preferred_element_type=F32)
    act = jnp.square(jnp.maximum(up, 0.0))
    acc_ref[...] += _dot(act, wdown_ref[...])

    @pl.when(f == nf - 1)
    def _():
        x2 = x1_ref[...] + acc_ref[...]
        gate = _sigmoid(_dot(_rmsnorm_rows(x2, gple_ref[...]), wgate_ref[...]))
        o_ref[...] = x2 + gate * _dot(p_ref[...], wproj_ref[...])


def _tail(x2d, y, bonus, g, p_cv, p_cv1, p_cv2, o_sb, p2d, lw, wts, tm, tf):
    n, d = x2d.shape
    d_ff = wts["w_up"].shape[1]
    ones = _head_block_ones(RW_DIM, 1.0 / HEAD_DIM)
    vec = lambda a: a.reshape(1, -1)
    row = lambda w, col=0: pl.BlockSpec((tm, w), lambda i, f, col=col: (i, col))
    full = lambda a: pl.BlockSpec(a.shape, lambda i, f: (0,) * a.ndim)
    gnw, gnb, gmlp, gple = vec(lw["gn_w"]), vec(lw["gn_b"]), vec(lw["g_mlp"]), vec(lw["g_ple"])
    if o_sb.ndim == 2:
        osb_spec = row(SB_DIM)
    else:
        _, nblk, _, blk = o_sb.shape
        tiles_per_seq = nblk * blk // tm
        osb_spec = pl.BlockSpec((1, tm // blk, SB_DIM, blk),
                                lambda i, f: (i // tiles_per_seq, i % tiles_per_seq, 0, 0))
    in_specs = [row(d), row(RW_DIM), row(RW_DIM), row(RW_DIM),
                row(CONV_DIM, 0), row(CONV_DIM, 1), row(CONV_DIM, 2),
                row(CONV_DIM, 1), row(CONV_DIM, 2), row(CONV_DIM, 1), row(CONV_DIM, 2),
                osb_spec, row(p2d.shape[1]),
                full(gnw), full(gnb), full(lw["conv_w"]), full(ones), full(wts["w_out"]), full(gmlp),
                pl.BlockSpec((d, tf), lambda i, f: (0, f)), pl.BlockSpec((tf, d), lambda i, f: (f, 0)),
                full(gple), full(wts["w_ple_gate"]), full(wts["w_ple_proj"])]
    return pl.pallas_call(
        _tail_kernel,
        grid=(n // tm, d_ff // tf),
        in_specs=in_specs,
        out_specs=pl.BlockSpec((tm, d), lambda i, f: (i, 0)),
        out_shape=jax.ShapeDtypeStruct((n, d), F32),
        scratch_shapes=[pltpu.VMEM((tm, d), F32), pltpu.VMEM((tm, d), BF16), pltpu.VMEM((tm, d), F32)],
        compiler_params=_cparams("parallel", "arbitrary"),
        name="tail",
    )(x2d, y, bonus, g, p_cv, p_cv, p_cv, p_cv1, p_cv1, p_cv2, p_cv2, o_sb, p2d,
      gnw, gnb, lw["conv_w"], ones, wts["w_out"], gmlp, wts["w_up"], wts["w_down"],
      gple, wts["w_ple_gate"], wts["w_ple_proj"])


def _conv_state_kernel(cv_ref, o_ref):
    o_ref[...] = cv_ref[:, :, CONV_DIM:2 * CONV_DIM] * cv_ref[:, :, 2 * CONV_DIM:3 * CONV_DIM]


def _conv_state(p_cv3):
    b, t, c = p_cv3.shape
    return pl.pallas_call(
        _conv_state_kernel,
        grid=(1,),
        in_specs=[pl.BlockSpec((b, 8, c), lambda i: (0, t // 8 - 1, 0))],
        out_specs=pl.BlockSpec((b, 8, CONV_DIM), lambda i: (0, 0, 0)),
        out_shape=jax.ShapeDtypeStruct((b, 8, CONV_DIM), F32),
        name="conv_state",
    )(p_cv3)


def _shift_rows(x3, first_rows, n):
    return jnp.concatenate([first_rows, x3[:, :-n]], axis=1)


def _layer(x2d, p2d, b, t, shift_prev, wkv_prev_bd, conv_prev, attn_fn, att_blk, lw, wts, tm, tf,
           wkv_nseq, wkv_tlen, wkv_groups):
    n, d = x2d.shape
    p_rw, p_cv, k_new, v_new, *attn_ops = _in_proj(x2d, lw["g_mix"], wts["w_in"], lw["q_gain"], lw["k_gain"],
                                                   tm, t, att_blk)
    rw_cols = p_rw.shape[1]
    p_rw3 = p_rw.reshape(b, t, rw_cols)
    prev = _shift_rows(p_rw3, shift_prev[:, None, :], 1).reshape(n, rw_cols)
    r, lwd, k2, v, av, bv, g, bonus = _rwkv_pre(p_rw, prev, lw, tm)
    grp = lambda a: a.reshape(-1, (n // (b // wkv_nseq)), RW_DIM) if wkv_nseq > 1 else a.reshape(b, t, RW_DIM)
    y, s_bd = _wkv(grp(r), grp(lwd), grp(k2), grp(v), grp(av), grp(bv), wkv_prev_bd, wkv_nseq, wkv_tlen,
                   wkv_groups)
    y = y.reshape(n, RW_DIM)

    o_sb = attn_fn(k_new, v_new, *attn_ops)

    p_cv3 = p_cv.reshape(b, t, 3 * CONV_DIM)
    ones = jnp.ones((b, 1, CONV_DIM), F32)
    state_rows = lambda s: jnp.concatenate([jnp.zeros_like(s), s, jnp.broadcast_to(ones, s.shape)], axis=-1)
    p_cv1 = _shift_rows(p_cv3, state_rows(conv_prev[:, 1:2]), 1).reshape(n, 3 * CONV_DIM)
    p_cv2 = _shift_rows(p_cv3, state_rows(conv_prev), 2).reshape(n, 3 * CONV_DIM)
    x_out = _tail(x2d, y, bonus, g, p_cv, p_cv1, p_cv2, o_sb, p2d, lw, wts, tm, tf)
    conv_new = _conv_state(p_cv3)[:, 8 - (CONV_K - 1):]
    return x_out, k_new, v_new, s_bd, p_rw3[:, -1], conv_new


def _run_trunk(x, p, shift, wkv, conv, attn_builder, att_blk, params, bf16_w, tm, tf,
               wkv_nseq, wkv_tlen, wkv_groups):
    b, t, d = x.shape
    depth = p.shape[0]
    x2d = x.reshape(b * t, d)
    ks, vs, wkvs, shifts, convs = [], [], [], [], []
    for l in range(depth):
        lw = {name: arr[l] for name, arr in params.items()}
        wts = {name: arr[l] for name, arr in bf16_w.items()}
        x2d, k_new, v_new, s_bd, shift_new, conv_new = _layer(
            x2d, p[l].reshape(b * t, -1), b, t, shift[l], _state_to_bd(wkv[l]), conv[l],
            attn_builder(l, lw), att_blk, lw, wts, tm, tf, wkv_nseq, wkv_tlen, wkv_groups)
        ks.append(k_new.reshape(b, t, SB_HEADS, HEAD_DIM))
        vs.append(v_new.reshape(b, t, SB_HEADS, HEAD_DIM))
        wkvs.append(_state_from_bd(s_bd, RW_HEADS))
        shifts.append(shift_new)
        convs.append(conv_new)
    return (x2d.reshape(b, t, d), jnp.stack(ks), jnp.stack(vs), jnp.stack(wkvs), jnp.stack(shifts),
            jnp.stack(convs))


def kernel(x_prompt, x_sample, cache_k, cache_v, state_wkv, state_shift, state_conv, page_table,
           p_prompt, p_sample, g_mix, w_in, mu_shift, w0, w2, a0, a2, g2, k_k, k_a, r_k,
           gn_w, gn_b, conv_w, q_gain, k_gain, sb_bias, w_out, g_mlp, w_up, w_down, g_ple,
           w_ple_gate, w_ple_proj):
    params = dict(g_mix=g_mix, mu_shift=mu_shift, w0=w0, w2=w2, a0=a0, a2=a2, g2=g2,
                  k_k=k_k, k_a=k_a, r_k=r_k, gn_w=gn_w, gn_b=gn_b, conv_w=conv_w,
                  q_gain=q_gain, k_gain=k_gain, sb_bias=sb_bias, g_mlp=g_mlp, g_ple=g_ple)
    bf16_w = dict(w_in=w_in.astype(BF16), w_out=w_out.astype(BF16), w_up=w_up.astype(BF16),
                  w_down=w_down.astype(BF16), w_ple_gate=w_ple_gate.astype(BF16),
                  w_ple_proj=w_ple_proj.astype(BF16))
    depth = w_in.shape[0]
    b, t, d = x_prompt.shape
    db, dt, _ = x_sample.shape
    rw_proj = state_shift.shape[-1]
    d_ff = w_up.shape[-1]
    tf = min(1024, d_ff)

    n_phys = cache_k.shape[1]
    page_view = lambda c: c.transpose(0, 1, 3, 4, 2).reshape(depth, n_phys, SB_DIM, PAGE_SIZE)
    cache_kt, cache_vt = page_view(cache_k), page_view(cache_v)

    def prompt_attn(l, lw):
        return lambda k, v, q_t, k_bf16, v_t: _attn_prompt(q_t, k_bf16, v_t, lw["sb_bias"], b, t)

    def sample_attn(l, lw):
        return lambda k, v, q: _attn_sample(q, k, v, cache_kt, cache_vt, l, page_table, lw["sb_bias"], db, dt)

    dtp = x_prompt.dtype
    shift0 = jnp.zeros((depth, b, rw_proj), dtp)
    wkv0 = jnp.zeros((depth, b, RW_HEADS, HEAD_DIM, HEAD_DIM), dtp)
    conv0 = jnp.zeros((depth, b, CONV_K - 1, CONV_DIM), dtp)
    (y_prompt, k_prompt, v_prompt, wkv_prompt, shift_prompt, conv_prompt) = _run_trunk(
        x_prompt, p_prompt, shift0, wkv0, conv0, prompt_attn, min(ATT_BLOCK, t), params, bf16_w,
        min(512, t), tf, 1, WKV_ROWS, b)
    seqs_per_group = WKV_ROWS // dt
    (y_sample, k_sample, v_sample, wkv_sample, shift_sample, conv_sample) = _run_trunk(
        x_sample, p_sample, state_shift, state_wkv, state_conv, sample_attn, None, params, bf16_w,
        min(512, db * dt), tf, seqs_per_group, dt, min(2, db // seqs_per_group))
    return (y_prompt, y_sample, k_prompt, v_prompt, wkv_prompt, shift_prompt, conv_prompt,
            k_sample, v_sample, wkv_sample, shift_sample, conv_sample)
```

```python
import functools

import jax
import jax.numpy as jnp
from jax import lax
from jax.experimental import pallas as pl
from jax.experimental.pallas import tpu as pltpu

F32 = jnp.float32
BF16 = jnp.bfloat16

HEAD_DIM = 64
RW_HEADS = 4
RW_DIM = RW_HEADS * HEAD_DIM
LORA_WA = 128
CONV_DIM = 256
CONV_K = 3
SB_HEADS = 8
SB_DIM = SB_HEADS * HEAD_DIM
PAGE_SIZE = 128
RMS_EPS = 1e-6
GN_EPS = 64e-5
KK_EPS = 1e-12
LOG2E = 1.4426950408889634

WKV_ROWS = 64
ATT_BLOCK = 256
ATT_HEADS = 4
VMEM_LIMIT_BYTES = 56 * 1024 * 1024


def _cparams(*sem):
    return pltpu.CompilerParams(dimension_semantics=sem, vmem_limit_bytes=VMEM_LIMIT_BYTES)


def _dot(a, b):
    return jnp.dot(a.astype(BF16), b.astype(BF16), preferred_element_type=F32)


def _dg(a, b, ca, cb):
    return lax.dot_general(a.astype(BF16), b.astype(BF16), (((ca,), (cb,)), ((), ())),
                           preferred_element_type=F32)


def _split2(a):
    hi = a.astype(BF16)
    lo = (a - hi.astype(F32)).astype(BF16)
    return hi, lo


def _dot_exact_rhs(a, b_bf16):
    h1 = a.astype(BF16)
    r1 = a - h1.astype(F32)
    h2 = r1.astype(BF16)
    h3 = (r1 - h2.astype(F32)).astype(BF16)
    out = jnp.dot(h1, b_bf16, preferred_element_type=F32)
    out = out + jnp.dot(h2, b_bf16, preferred_element_type=F32)
    out = out + jnp.dot(h3, b_bf16, preferred_element_type=F32)
    return out


def _dot_exact_lhs(a_bf16, b):
    h1 = b.astype(BF16)
    r1 = b - h1.astype(F32)
    h2 = r1.astype(BF16)
    h3 = (r1 - h2.astype(F32)).astype(BF16)
    out = jnp.dot(a_bf16, h1, preferred_element_type=F32)
    out = out + jnp.dot(a_bf16, h2, preferred_element_type=F32)
    out = out + jnp.dot(a_bf16, h3, preferred_element_type=F32)
    return out


def _softplus(x):
    return jnp.maximum(x, 0.0) + jnp.log1p(jnp.exp(-jnp.abs(x)))


def _sigmoid(x):
    return 1.0 / (1.0 + jnp.exp(-x))


def _rmsnorm_rows(x, g):
    ms = jnp.mean(x * x, axis=-1, keepdims=True)
    return x * lax.rsqrt(ms + RMS_EPS) * g


def _head_block_ones(width, scale):
    i = jnp.arange(width) // HEAD_DIM
    return jnp.where(i[:, None] == i[None, :], scale, 0.0).astype(BF16)


def _in_proj_kernel(x_ref, g_ref, w_ref, ebd_ref, qg_ref, kg_ref,
                    rw_ref, cv_ref, k_ref, v_ref, *attn_refs, att_blk):
    h = _rmsnorm_rows(x_ref[...], g_ref[...]).astype(BF16)
    c0 = rw_ref.shape[1]
    c1 = c0 + cv_ref.shape[1]
    rw_ref[...] = jnp.dot(h, w_ref[:, 0:c0], preferred_element_type=F32)
    cv_ref[...] = jnp.dot(h, w_ref[:, c0:c1], preferred_element_type=F32)

    def head_norm(t, gain):
        ms = jnp.dot((t * t).astype(BF16), ebd_ref[...], preferred_element_type=F32)
        return t * lax.rsqrt(ms + RMS_EPS) * gain

    q = jnp.dot(h, w_ref[:, c1:c1 + SB_DIM], preferred_element_type=F32)
    q = head_norm(q, qg_ref[...]) * (HEAD_DIM ** -0.5)
    k = jnp.dot(h, w_ref[:, c1 + SB_DIM:c1 + 2 * SB_DIM], preferred_element_type=F32)
    k = head_norm(k, kg_ref[...])
    k_ref[...] = k
    v = jnp.dot(h, w_ref[:, c1 + 2 * SB_DIM:c1 + 3 * SB_DIM], preferred_element_type=F32)
    v_ref[...] = v
    if att_blk is None:
        (q_ref,) = attn_refs
        q_ref[...] = q
    else:
        qt_ref, kb_ref, vt_ref = attn_refs
        kb_ref[...] = k.astype(BF16)
        q_t = (q * LOG2E).T
        v_t = v.T
        for s in range(qt_ref.shape[1]):
            qt_ref[0, s] = q_t[:, s * att_blk:(s + 1) * att_blk].astype(BF16)
            vt_ref[0, s] = v_t[:, s * att_blk:(s + 1) * att_blk].astype(BF16)


def _in_proj(x2d, g_mix, w_in_bf16, q_gain, k_gain, tm, seq_len, att_blk):
    n, d = x2d.shape
    cols = w_in_bf16.shape[1]
    rw_cols = cols - 3 * CONV_DIM - 3 * SB_DIM
    ebd = _head_block_ones(SB_DIM, 1.0 / HEAD_DIM)
    qg = jnp.tile(q_gain, SB_HEADS)[None, :]
    kg = jnp.tile(k_gain, SB_HEADS)[None, :]
    row = lambda w: pl.BlockSpec((tm, w), lambda i: (i, 0))
    full = lambda a: pl.BlockSpec(a.shape, lambda i: (0,) * a.ndim)
    g2d = g_mix[None, :]
    out_specs = [row(rw_cols), row(3 * CONV_DIM), row(SB_DIM), row(SB_DIM)]
    out_shape = [jax.ShapeDtypeStruct((n, rw_cols), F32), jax.ShapeDtypeStruct((n, 3 * CONV_DIM), F32),
                 jax.ShapeDtypeStruct((n, SB_DIM), F32), jax.ShapeDtypeStruct((n, SB_DIM), F32)]
    if att_blk is None:
        out_specs.append(row(SB_DIM))
        out_shape.append(jax.ShapeDtypeStruct((n, SB_DIM), F32))
    else:
        tiles_per_seq = seq_len // tm
        blks = tm // att_blk
        t_spec = pl.BlockSpec((1, blks, SB_DIM, att_blk), lambda i: (i // tiles_per_seq, i % tiles_per_seq, 0, 0))
        t_shape = jax.ShapeDtypeStruct((n // seq_len, seq_len // att_blk, SB_DIM, att_blk), BF16)
        out_specs += [t_spec, row(SB_DIM), t_spec]
        out_shape += [t_shape, jax.ShapeDtypeStruct((n, SB_DIM), BF16), t_shape]
    return pl.pallas_call(
        functools.partial(_in_proj_kernel, att_blk=att_blk),
        grid=(n // tm,),
        in_specs=[row(d), full(g2d), full(w_in_bf16), full(ebd), full(qg), full(kg)],
        out_specs=out_specs,
        out_shape=out_shape,
        compiler_params=_cparams("parallel"),
        name="in_proj",
    )(x2d, g2d, w_in_bf16, ebd, qg, kg)


def _rows_before(cur, halo, first_rows, is_first, n):
    row = lax.broadcasted_iota(jnp.int32, cur.shape, 0)
    out = pltpu.roll(cur, n, 0)
    n_halo, n_first = halo.shape[0], first_rows.shape[0]
    for k in range(n):
        before = jnp.where(is_first, first_rows[n_first - n + k:n_first - n + k + 1],
                           halo[n_halo - n + k:n_halo - n + k + 1])
        out = jnp.where(row == k, before, out)
    return out


def _rwkv_pre_kernel(p_ref, *refs, tiles_per_seq):
    p = p_ref[...]
    if tiles_per_seq is None:
        prev = refs[0][...]
        refs = refs[1:]
    else:
        prev = _rows_before(p, refs[0][...], refs[1][0], pl.program_id(0) % tiles_per_seq == 0, 1)
        refs = refs[2:]
    (mu_ref, w0_ref, a0_ref, kk_ref, ka_ref, rk_ref, wwa_ref, g2_ref, ones_ref,
     r_ref, lw_ref, k_ref, v_ref, av_ref, bv_ref, g_ref, bonus_ref) = refs
    xs = p + (prev - p) * mu_ref[...]
    r = xs[:, 0:RW_DIM]
    k = xs[:, RW_DIM:2 * RW_DIM]
    v = xs[:, 2 * RW_DIM:3 * RW_DIM]
    d = xs[:, 3 * RW_DIM:3 * RW_DIM + LORA_WA]
    dg = xs[:, 3 * RW_DIM + LORA_WA:]
    lane = lax.broadcasted_iota(jnp.int32, d.shape, 1)
    dwa = jnp.where(lane < LORA_WA // 2, jnp.tanh(d), d)
    wa = _dot(dwa, wwa_ref[...])
    w_log = -_softplus(-(w0_ref[...] + wa[:, 0:RW_DIM])) - 0.5
    lw_ref[...] = -jnp.exp(w_log)
    a = _sigmoid(a0_ref[...] + wa[:, RW_DIM:])
    g_ref[...] = _dot(_sigmoid(dg), g2_ref[...])
    kk = k * kk_ref[...]
    ss = _dot_exact_rhs(kk * kk, ones_ref[...])
    kk = kk * lax.rsqrt(ss + KK_EPS)
    k2 = k * (1.0 + (a - 1.0) * ka_ref[...])
    r_ref[...] = r
    k_ref[...] = k2
    v_ref[...] = v
    av_ref[...] = -kk
    bv_ref[...] = kk * a
    bonus_ref[...] = _dot_exact_rhs(r * k2 * rk_ref[...], ones_ref[...]) * v


def _rwkv_pre(p_rw, shift_prev, seq_len, lw, tm):
    n, c = p_rw.shape
    b = n // seq_len
    if seq_len % tm == 0:
        tiles_per_seq = seq_len // tm
        before = [p_rw, shift_prev[:, None, :]]
        before_specs = [pl.BlockSpec((8, c), lambda i: (jnp.maximum(i * (tm // 8) - 1, 0), 0)),
                        pl.BlockSpec((1, 1, c), lambda i: (i // tiles_per_seq, 0, 0))]
    else:
        tiles_per_seq = None
        prev = _shift_rows(p_rw.reshape(b, seq_len, c), shift_prev[:, None, :], 1).reshape(n, c)
        before = [prev]
        before_specs = [pl.BlockSpec((tm, c), lambda i: (i, 0))]
    ones = _head_block_ones(RW_DIM, 1.0)
    w_lora = lw["w2"].shape[0]
    a_lora = lw["a2"].shape[0]
    wwa = jnp.zeros((w_lora + a_lora, 2 * RW_DIM), F32)
    wwa = wwa.at[:w_lora, :RW_DIM].set(lw["w2"]).at[w_lora:, RW_DIM:].set(lw["a2"]).astype(BF16)
    g2 = lw["g2"].astype(BF16)
    vec = lambda a: a.reshape(1, -1)
    smalls = [vec(lw["mu_shift"]), vec(lw["w0"]), vec(lw["a0"]), vec(lw["k_k"]), vec(lw["k_a"]),
              vec(lw["r_k"]), wwa, g2, ones]
    row = lambda w: pl.BlockSpec((tm, w), lambda i: (i, 0))
    full = lambda a: pl.BlockSpec(a.shape, lambda i: (0,) * a.ndim)
    return pl.pallas_call(
        functools.partial(_rwkv_pre_kernel, tiles_per_seq=tiles_per_seq),
        grid=(n // tm,),
        in_specs=[row(c)] + before_specs + [full(a) for a in smalls],
        out_specs=[row(RW_DIM)] * 8,
        out_shape=[jax.ShapeDtypeStruct((n, RW_DIM), F32)] * 8,
        compiler_params=_cparams("parallel"),
        name="rwkv_pre",
    )(p_rw, *before, *smalls)


def _wkv_kernel(r_ref, lw_ref, k_ref, v_ref, a_ref, b_ref, s0_ref, y_ref, s_ref, *, nseq, tlen):
    P = WKV_ROWS
    HP = RW_HEADS * P
    c = pl.program_id(1)

    @pl.when(c == 0)
    def _():
        s_ref[...] = s0_ref[...]

    prow = lax.broadcasted_iota(jnp.int32, (P, P), 0)
    pcol = lax.broadcasted_iota(jnp.int32, (P, P), 1)
    same_seq = (prow // tlen) == (pcol // tlen)
    tri_incl = jnp.where(same_seq & (pcol <= prow), 1.0, 0.0).astype(BF16)
    last_sel = jnp.where(same_seq & (pcol == (prow // tlen) * tlen + (tlen - 1)), 1.0, 0.0).astype(BF16)
    srow = lax.broadcasted_iota(jnp.int32, (HP, HP), 0)
    scol = lax.broadcasted_iota(jnp.int32, (HP, HP), 1)
    same_blk = ((srow // tlen) == (scol // tlen))
    m_strict = same_blk & (scol < srow)
    m_incl = same_blk & (scol <= srow)
    eye = jnp.where(srow == scol, 1.0, 0.0)
    hrow = lax.broadcasted_iota(jnp.int32, (HP, RW_DIM), 0)
    head_mask = (hrow // P) == (lax.broadcasted_iota(jnp.int32, (HP, RW_DIM), 1) // HEAD_DIM)
    seq_of_row = (hrow % P) // tlen

    def stack(x):
        return jnp.where(head_mask, jnp.concatenate([x] * RW_HEADS, axis=0), 0.0)

    groups = range(r_ref.shape[0])
    lw = [lw_ref[g] for g in groups]
    cw = [_dot_exact_lhs(tri_incl, x) for x in lw]
    cw_end = [_dot_exact_lhs(last_sel, x) for x in cw]
    w_inv = [jnp.exp(-x) for x in cw]
    w_toend = [jnp.exp(e - x) for e, x in zip(cw_end, cw)]
    a_s = [stack(a_ref[g] * jnp.exp(cw[g] - lw[g])) for g in groups]
    r_s = [stack(r_ref[g] * jnp.exp(cw[g])) for g in groups]
    b_s = [stack(b_ref[g] * w_inv[g]) for g in groups]
    k_s = [stack(k_ref[g] * w_inv[g]) for g in groups]
    v_s = [stack(v_ref[g]) for g in groups]
    bend_s = [stack(b_ref[g] * w_toend[g]) for g in groups]
    kend_s = [stack(k_ref[g] * w_toend[g]) for g in groups]

    n_ab = [jnp.where(m_strict, _dg(a_s[g], b_s[g], 1, 1), 0.0) for g in groups]
    a_ak = [jnp.where(m_strict, _dg(a_s[g], k_s[g], 1, 1), 0.0) for g in groups]
    a_rb = [jnp.where(m_incl, _dg(r_s[g], b_s[g], 1, 1), 0.0) for g in groups]
    a_rk = [jnp.where(m_incl, _dg(r_s[g], k_s[g], 1, 1), 0.0) for g in groups]

    t_inv = [eye + x for x in n_ab]
    pw = n_ab
    span = 2
    while span < tlen:
        pw = [_dot(x, x) for x in pw]
        t_inv = [t + _dot(t, x) for t, x in zip(t_inv, pw)]
        span *= 2

    ap_s = [_dot(t_inv[g], a_s[g]) for g in groups]
    akv = [_dot(a_ak[g], v_s[g]) for g in groups]
    u1_s = [_dot(t_inv[g], akv[g]) for g in groups]
    y_s = [_dot(a_rb[g], u1_s[g]) + _dot(a_rk[g], v_s[g]) for g in groups]
    rp_s = [r_s[g] + _dot(a_rb[g], ap_s[g]) for g in groups]
    wc_row = [jnp.exp(x) for x in cw_end]

    for i in range(nseq):
        if nseq == 1:
            sel = lambda x: x
        else:
            sel = lambda x: jnp.where(seq_of_row == i, x, 0.0)
        s_i = [s_ref[g * nseq + i] for g in groups]
        u_i = [_dg(sel(ap_s[g]), s_i[g], 1, 1) + sel(u1_s[g]) for g in groups]
        y_s = [y_s[g] + _dg(sel(rp_s[g]), s_i[g], 1, 1) for g in groups]
        for g in groups:
            wc_i = wc_row[g][i * tlen:i * tlen + 1, :]
            s_ref[g * nseq + i] = (s_i[g] * wc_i + _dg(u_i[g], sel(bend_s[g]), 0, 0)
                                   + _dg(sel(v_s[g]), sel(kend_s[g]), 0, 0))
    for g in groups:
        y = y_s[g][0:P]
        for h in range(1, RW_HEADS):
            y = y + y_s[g][h * P:(h + 1) * P]
        y_ref[g] = y


def _wkv(r, lw, k, v, av, bv, s0_bd, nseq, tlen, groups):
    g, rows, _ = r.shape
    n_chunks = rows // WKV_ROWS
    blk = pl.BlockSpec((groups, WKV_ROWS, RW_DIM), lambda i, c: (i, c, 0))
    sblk = pl.BlockSpec((groups * nseq, RW_DIM, RW_DIM), lambda i, c: (i, 0, 0))
    return pl.pallas_call(
        functools.partial(_wkv_kernel, nseq=nseq, tlen=tlen),
        grid=(g // groups, n_chunks),
        in_specs=[blk] * 6 + [sblk],
        out_specs=[blk, sblk],
        out_shape=[jax.ShapeDtypeStruct(r.shape, F32), jax.ShapeDtypeStruct(s0_bd.shape, F32)],
        compiler_params=_cparams("parallel", "arbitrary"),
        name="wkv",
    )(r, lw, k, v, av, bv, s0_bd)


def _state_to_bd(s):
    b, h, n, _ = s.shape
    eye = jnp.eye(h, dtype=s.dtype)
    return jnp.einsum("bhvk,hg->bhvgk", s, eye).reshape(b, h * n, h * n)


def _state_from_bd(s_bd, h):
    b, hn, _ = s_bd.shape
    n = hn // h
    s5 = s_bd.reshape(b, h, n, h, n)
    return jnp.stack([s5[:, i, :, i, :] for i in range(h)], axis=1)


ATT_NEG = -1e30


def _attn_prompt_kernel(bias_ref, tq_ref, tj_ref, qt_ref, k_ref, vt_ref, o_ref,
                        z_ref, w_ref, zinit_ref, acc_ref, carry_ref):
    grp = pl.program_id(1)
    nk, gw, blk = qt_ref.shape[1:]
    n_tiles = nk * (nk + 1) // 2
    row_head = lax.broadcasted_iota(jnp.int32, (gw, blk), 0) // HEAD_DIM
    krow = lax.broadcasted_iota(jnp.int32, (blk + 8, blk), 0)
    kcol = lax.broadcasted_iota(jnp.int32, (blk + 8, blk), 1)
    neg_sums = jnp.where((kcol > krow) | (krow >= blk), -1.0, 0.0).astype(BF16)
    hidden = (lax.broadcasted_iota(jnp.int32, (blk, blk), 0) >= lax.broadcasted_iota(jnp.int32, (blk, blk), 1))
    for h in range(ATT_HEADS):
        bias2 = bias_ref[grp * ATT_HEADS + h] * LOG2E
        zinit_ref[0, h] = jnp.full((blk, blk), bias2, F32)
        zinit_ref[1, h] = jnp.where(hidden, ATT_NEG, bias2)
    z_ref[...] = jnp.zeros_like(z_ref)
    w_ref[...] = jnp.zeros_like(w_ref)
    acc_ref[...] = jnp.zeros_like(acc_ref)
    carry_ref[...] = jnp.zeros_like(carry_ref)

    def body(i, _):
        slot_new = i % 2
        slot_old = 1 - slot_new

        f3 = jnp.maximum(i - 2, 0)
        q3 = tq_ref[f3]
        j3 = tj_ref[f3]
        keep = jnp.where(j3 == q3, 0.0, 1.0)
        for h in range(ATT_HEADS):
            rows = slice(h * HEAD_DIM, (h + 1) * HEAD_DIM)
            pv = jnp.dot(vt_ref[0, j3, rows, :], w_ref[slot_old, h], preferred_element_type=F32)
            acc_ref[rows, :] = acc_ref[rows, :] * keep + pv
        o_ref[0, q3] = acc_ref[...]

        f2 = jnp.clip(i - 1, 0, n_tiles - 1)
        valid2 = (i >= 1) & (i <= n_tiles)
        first2 = tj_ref[f2] == tq_ref[f2]
        null2 = jnp.where(valid2, 0.0, ATT_NEG)
        count2 = jnp.where(valid2, 1.0, 0.0)
        for h in range(ATT_HEADS):
            z = z_ref[slot_old, h]
            sp = jnp.maximum(z, 0.0) + jnp.log(1.0 + jnp.exp2(-jnp.abs(z))) * LOG2E
            sums = jnp.dot(neg_sums, sp.astype(BF16), preferred_element_type=F32)
            carry = jnp.where(first2, 0.0, carry_ref[h])
            w_ref[slot_new, h] = jnp.exp2((z - sp) + sums[0:blk] + (carry + null2)).astype(BF16)
            carry_ref[h] = carry + sums[blk:blk + 1] * count2

        f1 = jnp.minimum(i, n_tiles - 1)
        q1 = tq_ref[f1]
        j1 = tj_ref[f1]
        diag1 = jnp.where(j1 == q1, 1, 0)
        k_blk = k_ref[0, pl.ds(pl.multiple_of(j1 * blk, blk), blk), :]
        q_grp = qt_ref[0, q1]
        for h in range(ATT_HEADS):
            q_h = jnp.where(row_head == h, q_grp, jnp.zeros_like(q_grp))
            z_ref[slot_new, h] = zinit_ref[diag1, h] + jnp.dot(k_blk, q_h, preferred_element_type=F32)
        return 0

    lax.fori_loop(0, n_tiles + 2, body, 0)


def _attn_prompt(q_t, k_bf16, v_t, sb_bias, b, t):
    _, nk, _, blk = q_t.shape
    gw = ATT_HEADS * HEAD_DIM
    k3 = k_bf16.reshape(b, t, SB_DIM)
    tiles = [(qi, j) for qi in range(nk) for j in range(qi, -1, -1)]
    tile_q = jnp.array([qi for qi, _ in tiles], jnp.int32)
    tile_j = jnp.array([j for _, j in tiles], jnp.int32)
    blocks = pl.BlockSpec((1, nk, gw, blk), lambda i, g, *_: (i, 0, g, 0))
    grid_spec = pltpu.PrefetchScalarGridSpec(
        num_scalar_prefetch=3,
        grid=(b, SB_HEADS // ATT_HEADS),
        in_specs=[blocks, pl.BlockSpec((1, t, gw), lambda i, g, *_: (i, 0, g)), blocks],
        out_specs=blocks,
        scratch_shapes=[pltpu.VMEM((2, ATT_HEADS, blk, blk), F32), pltpu.VMEM((2, ATT_HEADS, blk, blk), BF16),
                        pltpu.VMEM((2, ATT_HEADS, blk, blk), F32),
                        pltpu.VMEM((gw, blk), F32), pltpu.VMEM((ATT_HEADS, 1, blk), F32)],
    )
    return pl.pallas_call(
        _attn_prompt_kernel,
        grid_spec=grid_spec,
        out_shape=jax.ShapeDtypeStruct((b, nk, SB_DIM, blk), F32),
        compiler_params=_cparams("parallel", "parallel"),
        name="attn_prompt",
    )(sb_bias, tile_q, tile_j, q_t, k3, v_t)


def _attn_sample_kernel(pt_ref, q_ref, bias_ref, kn_ref, vn_ref, *refs, tlen, n_pages):
    k_refs = refs[:n_pages]
    v_refs = refs[n_pages:2 * n_pages]
    o_ref = refs[2 * n_pages]
    rows = SB_HEADS * tlen
    rhead = lax.broadcasted_iota(jnp.int32, (rows, SB_DIM), 0) // tlen
    lhead = lax.broadcasted_iota(jnp.int32, (rows, SB_DIM), 1) // HEAD_DIM
    head_mask = rhead == lhead
    q_s = jnp.where(head_mask, jnp.concatenate([q_ref[0]] * SB_HEADS, axis=0), 0.0).astype(BF16)
    jrow = lax.broadcasted_iota(jnp.int32, (PAGE_SIZE, 2 * PAGE_SIZE), 0)
    jcol = lax.broadcasted_iota(jnp.int32, (PAGE_SIZE, 2 * PAGE_SIZE), 1)
    tri_ones = jnp.where((jrow > jcol) | (jcol >= PAGE_SIZE), 1.0, 0.0).astype(BF16)
    bias = bias_ref[...]

    def scores(z, mask):
        z = z + bias
        sp = jnp.maximum(z, 0.0) + jnp.log(1.0 + jnp.exp(-jnp.abs(z)))
        lk = -sp
        if mask is not None:
            lk = jnp.where(mask, lk, 0.0)
        lk_hi, lk_lo = _split2(lk)
        sums = (jnp.dot(lk_hi, tri_ones, preferred_element_type=F32)
                + jnp.dot(lk_lo, tri_ones, preferred_element_type=F32))
        return (z - sp) + sums[:, 0:PAGE_SIZE], sums[:, PAGE_SIZE:]

    pad = jnp.zeros((PAGE_SIZE - tlen, SB_DIM), F32)
    k_new = jnp.concatenate([kn_ref[0], pad], axis=0)
    v_new = jnp.concatenate([vn_ref[0], pad], axis=0)
    new_mask = (lax.broadcasted_iota(jnp.int32, (rows, PAGE_SIZE), 1)
                < lax.broadcasted_iota(jnp.int32, (rows, PAGE_SIZE), 0) % tlen)
    order = list(reversed(range(n_pages)))
    zs = [_dg(q_s, k_new, 1, 1)] + [_dot(q_s, k_refs[j][0, 0]) for j in order]
    parts = [scores(zs[0], new_mask)] + [scores(z, None) for z in zs[1:]]
    carry = jnp.zeros((rows, PAGE_SIZE), F32)
    ws = []
    for n, (logw, total) in enumerate(parts):
        w = jnp.exp(logw + carry)
        ws.append(jnp.where(new_mask, w, 0.0) if n == 0 else w)
        carry = carry + total
    acc = _dot(ws[0], v_new)
    for w, j in zip(ws[1:], order):
        acc = acc + _dg(w, v_refs[j][0, 0], 1, 1)
    acc = jnp.where(head_mask, acc, 0.0)
    out = acc[0:tlen]
    for h in range(1, SB_HEADS):
        out = out + acc[h * tlen:(h + 1) * tlen]
    o_ref[0] = out


def _attn_sample(q, k_new, v_new, cache_kt, cache_vt, layer, page_table, sb_bias, b, tlen):
    n_pages = page_table.shape[1]
    rows = SB_HEADS * tlen
    bias_rows = jnp.broadcast_to(jnp.repeat(sb_bias, tlen)[:, None], (rows, PAGE_SIZE)).astype(F32)
    seq = lambda a: a.reshape(b, tlen, SB_DIM)
    tok = pl.BlockSpec((1, tlen, SB_DIM), lambda i, pt: (i, 0, 0))
    page = lambda j: pl.BlockSpec((1, 1, SB_DIM, PAGE_SIZE), lambda i, pt, j=j: (layer, pt[i, j], 0, 0))
    pages = [page(j) for j in range(n_pages)]
    grid_spec = pltpu.PrefetchScalarGridSpec(
        num_scalar_prefetch=1,
        grid=(b,),
        in_specs=[tok, pl.BlockSpec((rows, PAGE_SIZE), lambda i, pt: (0, 0)), tok, tok] + pages + pages,
        out_specs=tok,
    )
    out = pl.pallas_call(
        functools.partial(_attn_sample_kernel, tlen=tlen, n_pages=n_pages),
        grid_spec=grid_spec,
        out_shape=jax.ShapeDtypeStruct((b, tlen, SB_DIM), F32),
        compiler_params=_cparams("parallel"),
        name="attn_sample",
    )(page_table, seq(q), bias_rows, seq(k_new), seq(v_new), *([cache_kt] * n_pages), *([cache_vt] * n_pages))
    return out.reshape(b * tlen, SB_DIM)


def _tail_kernel(x_ref, y_ref, bonus_ref, g_ref, cb_ref, cc_ref, cx_ref, *refs, tiles_per_seq):
    n_before = 4 if tiles_per_seq is None else 3
    before_refs = refs[:n_before]
    (osb_ref, p_ref, gnw_ref, gnb_ref, convw_ref, ones_ref, wout_ref, gmlp_ref, wup_ref, wdown_ref,
     gple_ref, wgate_ref, wproj_ref, o_ref, x1_ref, h2_ref, acc_ref) = refs[n_before:]
    f = pl.program_id(1)
    nf = pl.num_programs(1)

    @pl.when(f == 0)
    def _():
        y = y_ref[...]
        mu = _dot_exact_rhs(y, ones_ref[...])
        yc = y - mu
        var = _dot_exact_rhs(yc * yc, ones_ref[...])
        yn = yc * lax.rsqrt(var + GN_EPS) * gnw_ref[...] + gnb_ref[...]
        o_rw = (yn + bonus_ref[...]) * g_ref[...]
        cw = convw_ref[...]
        u = cc_ref[...] * cx_ref[...]
        if tiles_per_seq is None:
            cc1_ref, cx1_ref, cc2_ref, cx2_ref = before_refs
            u1 = cc1_ref[...] * cx1_ref[...]
            u2 = cc2_ref[...] * cx2_ref[...]
        else:
            cch_ref, cxh_ref, state_ref = before_refs
            u_halo = cch_ref[...] * cxh_ref[...]
            is_first = pl.program_id(0) % tiles_per_seq == 0
            u1 = _rows_before(u, u_halo, state_ref[0], is_first, 1)
            u2 = _rows_before(u, u_halo, state_ref[0], is_first, 2)
        o_cv = cb_ref[...] * (cw[0:1] * u2 + cw[1:2] * u1 + cw[2:3] * u)
        mix = _dot(o_rw, wout_ref[0:RW_DIM, :])
        mix = mix + _dot(o_cv, wout_ref[RW_DIM:RW_DIM + CONV_DIM, :])
        w_sb = wout_ref[RW_DIM + CONV_DIM:, :]
        if len(osb_ref.shape) == 2:
            mix = mix + _dot(osb_ref[...], w_sb)
        else:
            mix = mix + jnp.concatenate(
                [_dot(osb_ref[0, s].T, w_sb) for s in range(osb_ref.shape[1])], axis=0)
        x1 = x_ref[...] + mix
        x1_ref[...] = x1
        h2_ref[...] = _rmsnorm_rows(x1, gmlp_ref[...]).astype(BF16)
        acc_ref[...] = jnp.zeros_like(acc_ref)

    up = jnp.dot(h2_ref[...], wup_ref[...], preferred_element_type=F32)
    act = jnp.square(jnp.maximum(up, 0.0))
    acc_ref[...] += _dot(act, wdown_ref[...])

    @pl.when(f == nf - 1)
    def _():
        x2 = x1_ref[...] + acc_ref[...]
        gate = _sigmoid(_dot(_rmsnorm_rows(x2, gple_ref[...]), wgate_ref[...]))
        o_ref[...] = x2 + gate * _dot(p_ref[...], wproj_ref[...])


def _tail(x2d, y, bonus, g, p_cv, conv_prev, seq_len, o_sb, p3d, layer, lw, wts, tm, tf):
    n, d = x2d.shape
    b = n // seq_len
    d_ff = wts["w_up"].shape[1]
    ones = _head_block_ones(RW_DIM, 1.0 / HEAD_DIM)
    vec = lambda a: a.reshape(1, -1)
    row = lambda w, col=0: pl.BlockSpec((tm, w), lambda i, f, col=col: (i, col))
    full = lambda a: pl.BlockSpec(a.shape, lambda i, f: (0,) * a.ndim)
    gnw, gnb, gmlp, gple = vec(lw["gn_w"]), vec(lw["gn_b"]), vec(lw["g_mlp"]), vec(lw["g_ple"])
    if o_sb.ndim == 2:
        osb_spec = row(SB_DIM)
    else:
        _, nblk, _, blk = o_sb.shape
        osb_tiles = nblk * blk // tm
        osb_spec = pl.BlockSpec((1, tm // blk, SB_DIM, blk), lambda i, f: (i // osb_tiles, i % osb_tiles, 0, 0))
    if seq_len % tm == 0:
        tiles_per_seq = seq_len // tm
        halo = lambda col: pl.BlockSpec((8, CONV_DIM), lambda i, f: (jnp.maximum(i * (tm // 8) - 1, 0), col))
        before = [p_cv, p_cv, conv_prev]
        before_specs = [halo(1), halo(2),
                        pl.BlockSpec((1, CONV_K - 1, CONV_DIM), lambda i, f: (i // tiles_per_seq, 0, 0))]
    else:
        tiles_per_seq = None
        p_cv3 = p_cv.reshape(b, seq_len, 3 * CONV_DIM)
        state_rows = lambda s: jnp.concatenate([jnp.zeros_like(s), s, jnp.ones_like(s)], axis=-1)
        p_cv1 = _shift_rows(p_cv3, state_rows(conv_prev[:, 1:2]), 1).reshape(n, 3 * CONV_DIM)
        p_cv2 = _shift_rows(p_cv3, state_rows(conv_prev), 2).reshape(n, 3 * CONV_DIM)
        before = [p_cv1, p_cv1, p_cv2, p_cv2]
        before_specs = [row(CONV_DIM, 1), row(CONV_DIM, 2), row(CONV_DIM, 1), row(CONV_DIM, 2)]
    in_specs = [row(d), row(RW_DIM), row(RW_DIM), row(RW_DIM),
                row(CONV_DIM, 0), row(CONV_DIM, 1), row(CONV_DIM, 2)] + before_specs + [
                osb_spec, pl.BlockSpec((None, tm, p3d.shape[2]), lambda i, f: (layer, i, 0)),
                full(gnw), full(gnb), full(lw["conv_w"]), full(ones), full(wts["w_out"]), full(gmlp),
                pl.BlockSpec((d, tf), lambda i, f: (0, f)), pl.BlockSpec((tf, d), lambda i, f: (f, 0)),
                full(gple), full(wts["w_ple_gate"]), full(wts["w_ple_proj"])]
    return pl.pallas_call(
        functools.partial(_tail_kernel, tiles_per_seq=tiles_per_seq),
        grid=(n // tm, d_ff // tf),
        in_specs=in_specs,
        out_specs=pl.BlockSpec((tm, d), lambda i, f: (i, 0)),
        out_shape=jax.ShapeDtypeStruct((n, d), F32),
        scratch_shapes=[pltpu.VMEM((tm, d), F32), pltpu.VMEM((tm, d), BF16), pltpu.VMEM((tm, d), F32)],
        compiler_params=_cparams("parallel", "arbitrary"),
        name="tail",
    )(x2d, y, bonus, g, p_cv, p_cv, p_cv, *before, o_sb, p3d,
      gnw, gnb, lw["conv_w"], ones, wts["w_out"], gmlp, wts["w_up"], wts["w_down"],
      gple, wts["w_ple_gate"], wts["w_ple_proj"])


def _conv_state_kernel(cv_ref, o_ref):
    o_ref[...] = cv_ref[:, :, CONV_DIM:2 * CONV_DIM] * cv_ref[:, :, 2 * CONV_DIM:3 * CONV_DIM]


def _conv_state(p_cv3):
    b, t, c = p_cv3.shape
    return pl.pallas_call(
        _conv_state_kernel,
        grid=(1,),
        in_specs=[pl.BlockSpec((b, 8, c), lambda i: (0, t // 8 - 1, 0))],
        out_specs=pl.BlockSpec((b, 8, CONV_DIM), lambda i: (0, 0, 0)),
        out_shape=jax.ShapeDtypeStruct((b, 8, CONV_DIM), F32),
        name="conv_state",
    )(p_cv3)


def _shift_rows(x3, first_rows, n):
    return jnp.concatenate([first_rows, x3[:, :-n]], axis=1)


def _layer(x2d, p3d, layer, b, t, shift_prev, wkv_prev_bd, conv_prev, attn_fn, att_blk, lw, wts, tm, tf,
           wkv_nseq, wkv_tlen, wkv_groups):
    n, d = x2d.shape
    p_rw, p_cv, k_new, v_new, *attn_ops = _in_proj(x2d, lw["g_mix"], wts["w_in"], lw["q_gain"], lw["k_gain"],
                                                   tm, t, att_blk)
    rw_cols = p_rw.shape[1]
    p_rw3 = p_rw.reshape(b, t, rw_cols)
    r, lwd, k2, v, av, bv, g, bonus = _rwkv_pre(p_rw, shift_prev, t, lw, tm)
    grp = lambda a: a.reshape(-1, (n // (b // wkv_nseq)), RW_DIM) if wkv_nseq > 1 else a.reshape(b, t, RW_DIM)
    y, s_bd = _wkv(grp(r), grp(lwd), grp(k2), grp(v), grp(av), grp(bv), wkv_prev_bd, wkv_nseq, wkv_tlen,
                   wkv_groups)
    y = y.reshape(n, RW_DIM)

    o_sb = attn_fn(k_new, v_new, *attn_ops)

    x_out = _tail(x2d, y, bonus, g, p_cv, conv_prev, t, o_sb, p3d, layer, lw, wts, tm, tf)
    conv_new = _conv_state(p_cv.reshape(b, t, 3 * CONV_DIM))[:, 8 - (CONV_K - 1):]
    return x_out, k_new, v_new, s_bd, p_rw3[:, -1], conv_new


def _run_trunk(x, p, shift, wkv, conv, attn_builder, att_blk, params, bf16_w, tm, tf,
               wkv_nseq, wkv_tlen, wkv_groups):
    b, t, d = x.shape
    depth = p.shape[0]
    x2d = x.reshape(b * t, d)
    ks, vs, wkvs, shifts, convs = [], [], [], [], []
    for l in range(depth):
        lw = {name: arr[l] for name, arr in params.items()}
        wts = {name: arr[l] for name, arr in bf16_w.items()}
        x2d, k_new, v_new, s_bd, shift_new, conv_new = _layer(
            x2d, p.reshape(depth, b * t, -1), l, b, t, shift[l], _state_to_bd(wkv[l]), conv[l],
            attn_builder(l, lw), att_blk, lw, wts, tm, tf, wkv_nseq, wkv_tlen, wkv_groups)
        ks.append(k_new.reshape(b, t, SB_HEADS, HEAD_DIM))
        vs.append(v_new.reshape(b, t, SB_HEADS, HEAD_DIM))
        wkvs.append(_state_from_bd(s_bd, RW_HEADS))
        shifts.append(shift_new)
        convs.append(conv_new)
    return (x2d.reshape(b, t, d), jnp.stack(ks), jnp.stack(vs), jnp.stack(wkvs), jnp.stack(shifts),
            jnp.stack(convs))


def kernel(x_prompt, x_sample, cache_k, cache_v, state_wkv, state_shift, state_conv, page_table,
           p_prompt, p_sample, g_mix, w_in, mu_shift, w0, w2, a0, a2, g2, k_k, k_a, r_k,
           gn_w, gn_b, conv_w, q_gain, k_gain, sb_bias, w_out, g_mlp, w_up, w_down, g_ple,
           w_ple_gate, w_ple_proj):
    params = dict(g_mix=g_mix, mu_shift=mu_shift, w0=w0, w2=w2, a0=a0, a2=a2, g2=g2,
                  k_k=k_k, k_a=k_a, r_k=r_k, gn_w=gn_w, gn_b=gn_b, conv_w=conv_w,
                  q_gain=q_gain, k_gain=k_gain, sb_bias=sb_bias, g_mlp=g_mlp, g_ple=g_ple)
    bf16_w = dict(w_in=w_in.astype(BF16), w_out=w_out.astype(BF16), w_up=w_up.astype(BF16),
                  w_down=w_down.astype(BF16), w_ple_gate=w_ple_gate.astype(BF16),
                  w_ple_proj=w_ple_proj.astype(BF16))
    depth = w_in.shape[0]
    b, t, d = x_prompt.shape
    db, dt, _ = x_sample.shape
    rw_proj = state_shift.shape[-1]
    d_ff = w_up.shape[-1]
    tf = min(1024, d_ff)

    n_phys = cache_k.shape[1]
    page_view = lambda c: c.transpose(0, 1, 3, 4, 2).reshape(depth, n_phys, SB_DIM, PAGE_SIZE)
    cache_kt, cache_vt = page_view(cache_k), page_view(cache_v)

    def prompt_attn(l, lw):
        return lambda k, v, q_t, k_bf16, v_t: _attn_prompt(q_t, k_bf16, v_t, lw["sb_bias"], b, t)

    def sample_attn(l, lw):
        return lambda k, v, q: _attn_sample(q, k, v, cache_kt, cache_vt, l, page_table, lw["sb_bias"], db, dt)

    dtp = x_prompt.dtype
    shift0 = jnp.zeros((depth, b, rw_proj), dtp)
    wkv0 = jnp.zeros((depth, b, RW_HEADS, HEAD_DIM, HEAD_DIM), dtp)
    conv0 = jnp.zeros((depth, b, CONV_K - 1, CONV_DIM), dtp)
    (y_prompt, k_prompt, v_prompt, wkv_prompt, shift_prompt, conv_prompt) = _run_trunk(
        x_prompt, p_prompt, shift0, wkv0, conv0, prompt_attn, min(ATT_BLOCK, t), params, bf16_w,
        min(512, t), tf, 1, WKV_ROWS, b)
    seqs_per_group = WKV_ROWS // dt
    (y_sample, k_sample, v_sample, wkv_sample, shift_sample, conv_sample) = _run_trunk(
        x_sample, p_sample, state_shift, state_wkv, state_conv, sample_attn, None, params, bf16_w,
        min(512, db * dt), tf, seqs_per_group, dt, min(2, db // seqs_per_group))
    return (y_prompt, y_sample, k_prompt, v_prompt, wkv_prompt, shift_prompt, conv_prompt,
            k_sample, v_sample, wkv_sample, shift_sample, conv_sample)
```

```python
import functools

import jax
import jax.numpy as jnp
from jax import lax
from jax.experimental import pallas as pl
from jax.experimental.pallas import tpu as pltpu

F32 = jnp.float32
BF16 = jnp.bfloat16

HEAD_DIM = 64
RW_HEADS = 4
RW_DIM = RW_HEADS * HEAD_DIM
LORA_WA = 128
CONV_DIM = 256
CONV_K = 3
SB_HEADS = 8
SB_DIM = SB_HEADS * HEAD_DIM
PAGE_SIZE = 128
RMS_EPS = 1e-6
GN_EPS = 64e-5
KK_EPS = 1e-12
LOG2E = 1.4426950408889634

WKV_ROWS = 64
ATT_BLOCK = 256
ATT_HEADS = 4
VMEM_LIMIT_BYTES = 56 * 1024 * 1024


def _cparams(*sem):
    return pltpu.CompilerParams(dimension_semantics=sem, vmem_limit_bytes=VMEM_LIMIT_BYTES)


def _dot(a, b):
    return jnp.dot(a.astype(BF16), b.astype(BF16), preferred_element_type=F32)


def _dg(a, b, ca, cb):
    return lax.dot_general(a.astype(BF16), b.astype(BF16), (((ca,), (cb,)), ((), ())),
                           preferred_element_type=F32)


def _split2(a):
    hi = a.astype(BF16)
    lo = (a - hi.astype(F32)).astype(BF16)
    return hi, lo


def _dot_exact_rhs(a, b_bf16):
    h1 = a.astype(BF16)
    r1 = a - h1.astype(F32)
    h2 = r1.astype(BF16)
    h3 = (r1 - h2.astype(F32)).astype(BF16)
    out = jnp.dot(h1, b_bf16, preferred_element_type=F32)
    out = out + jnp.dot(h2, b_bf16, preferred_element_type=F32)
    out = out + jnp.dot(h3, b_bf16, preferred_element_type=F32)
    return out


def _dot_exact_lhs(a_bf16, b):
    h1 = b.astype(BF16)
    r1 = b - h1.astype(F32)
    h2 = r1.astype(BF16)
    h3 = (r1 - h2.astype(F32)).astype(BF16)
    out = jnp.dot(a_bf16, h1, preferred_element_type=F32)
    out = out + jnp.dot(a_bf16, h2, preferred_element_type=F32)
    out = out + jnp.dot(a_bf16, h3, preferred_element_type=F32)
    return out


def _softplus(x):
    return jnp.maximum(x, 0.0) + jnp.log1p(jnp.exp(-jnp.abs(x)))


def _sigmoid(x):
    return 1.0 / (1.0 + jnp.exp(-x))


def _rmsnorm_rows(x, g):
    ms = jnp.mean(x * x, axis=-1, keepdims=True)
    return x * lax.rsqrt(ms + RMS_EPS) * g


def _head_block_ones(width, scale):
    i = jnp.arange(width) // HEAD_DIM
    return jnp.where(i[:, None] == i[None, :], scale, 0.0).astype(BF16)


def _in_proj_kernel(x_ref, g_ref, w_ref, ebd_ref, qg_ref, kg_ref,
                    rw_ref, cv_ref, k_ref, v_ref, *attn_refs, att_blk):
    h = _rmsnorm_rows(x_ref[...], g_ref[...]).astype(BF16)
    c0 = rw_ref.shape[1]
    c1 = c0 + cv_ref.shape[1]
    rw_ref[...] = jnp.dot(h, w_ref[:, 0:c0], preferred_element_type=F32)
    cv_ref[...] = jnp.dot(h, w_ref[:, c0:c1], preferred_element_type=F32)

    def head_norm(t, gain):
        ms = jnp.dot((t * t).astype(BF16), ebd_ref[...], preferred_element_type=F32)
        return t * lax.rsqrt(ms + RMS_EPS) * gain

    q = jnp.dot(h, w_ref[:, c1:c1 + SB_DIM], preferred_element_type=F32)
    q = head_norm(q, qg_ref[...]) * (HEAD_DIM ** -0.5)
    k = jnp.dot(h, w_ref[:, c1 + SB_DIM:c1 + 2 * SB_DIM], preferred_element_type=F32)
    k = head_norm(k, kg_ref[...])
    k_ref[...] = k
    v = jnp.dot(h, w_ref[:, c1 + 2 * SB_DIM:c1 + 3 * SB_DIM], preferred_element_type=F32)
    v_ref[...] = v
    if att_blk is None:
        (q_ref,) = attn_refs
        q_ref[...] = q
    else:
        qt_ref, kb_ref, vt_ref = attn_refs
        kb_ref[...] = k.astype(BF16)
        q_t = (q * LOG2E).T
        v_t = v.T
        for s in range(qt_ref.shape[1]):
            qt_ref[0, s] = q_t[:, s * att_blk:(s + 1) * att_blk].astype(BF16)
            vt_ref[0, s] = v_t[:, s * att_blk:(s + 1) * att_blk].astype(BF16)


def _in_proj(x2d, g_mix, w_in_bf16, q_gain, k_gain, tm, seq_len, att_blk):
    n, d = x2d.shape
    cols = w_in_bf16.shape[1]
    rw_cols = cols - 3 * CONV_DIM - 3 * SB_DIM
    ebd = _head_block_ones(SB_DIM, 1.0 / HEAD_DIM)
    qg = jnp.tile(q_gain, SB_HEADS)[None, :]
    kg = jnp.tile(k_gain, SB_HEADS)[None, :]
    row = lambda w: pl.BlockSpec((tm, w), lambda i: (i, 0))
    full = lambda a: pl.BlockSpec(a.shape, lambda i: (0,) * a.ndim)
    g2d = g_mix[None, :]
    out_specs = [row(rw_cols), row(3 * CONV_DIM), row(SB_DIM), row(SB_DIM)]
    out_shape = [jax.ShapeDtypeStruct((n, rw_cols), F32), jax.ShapeDtypeStruct((n, 3 * CONV_DIM), F32),
                 jax.ShapeDtypeStruct((n, SB_DIM), F32), jax.ShapeDtypeStruct((n, SB_DIM), F32)]
    if att_blk is None:
        out_specs.append(row(SB_DIM))
        out_shape.append(jax.ShapeDtypeStruct((n, SB_DIM), F32))
    else:
        tiles_per_seq = seq_len // tm
        blks = tm // att_blk
        t_spec = pl.BlockSpec((1, blks, SB_DIM, att_blk), lambda i: (i // tiles_per_seq, i % tiles_per_seq, 0, 0))
        t_shape = jax.ShapeDtypeStruct((n // seq_len, seq_len // att_blk, SB_DIM, att_blk), BF16)
        out_specs += [t_spec, row(SB_DIM), t_spec]
        out_shape += [t_shape, jax.ShapeDtypeStruct((n, SB_DIM), BF16), t_shape]
    return pl.pallas_call(
        functools.partial(_in_proj_kernel, att_blk=att_blk),
        grid=(n // tm,),
        in_specs=[row(d), full(g2d), full(w_in_bf16), full(ebd), full(qg), full(kg)],
        out_specs=out_specs,
        out_shape=out_shape,
        compiler_params=_cparams("parallel"),
        name="in_proj",
    )(x2d, g2d, w_in_bf16, ebd, qg, kg)


def _rows_before(cur, halo, first_rows, is_first, n):
    row = lax.broadcasted_iota(jnp.int32, cur.shape, 0)
    out = pltpu.roll(cur, n, 0)
    n_halo, n_first = halo.shape[0], first_rows.shape[0]
    for k in range(n):
        before = jnp.where(is_first, first_rows[n_first - n + k:n_first - n + k + 1],
                           halo[n_halo - n + k:n_halo - n + k + 1])
        out = jnp.where(row == k, before, out)
    return out


def _rwkv_pre_kernel(p_ref, *refs, tiles_per_seq):
    p = p_ref[...]
    if tiles_per_seq is None:
        prev = refs[0][...]
        refs = refs[1:]
    else:
        prev = _rows_before(p, refs[0][...], refs[1][0], pl.program_id(0) % tiles_per_seq == 0, 1)
        refs = refs[2:]
    (mu_ref, w0_ref, a0_ref, kk_ref, ka_ref, rk_ref, wwa_ref, g2_ref, ones_ref,
     r_ref, lw_ref, k_ref, v_ref, av_ref, bv_ref, g_ref, bonus_ref) = refs
    xs = p + (prev - p) * mu_ref[...]
    r = xs[:, 0:RW_DIM]
    k = xs[:, RW_DIM:2 * RW_DIM]
    v = xs[:, 2 * RW_DIM:3 * RW_DIM]
    d = xs[:, 3 * RW_DIM:3 * RW_DIM + LORA_WA]
    dg = xs[:, 3 * RW_DIM + LORA_WA:]
    lane = lax.broadcasted_iota(jnp.int32, d.shape, 1)
    dwa = jnp.where(lane < LORA_WA // 2, jnp.tanh(d), d)
    wa = _dot(dwa, wwa_ref[...])
    w_log = -_softplus(-(w0_ref[...] + wa[:, 0:RW_DIM])) - 0.5
    lw_ref[...] = -jnp.exp(w_log)
    a = _sigmoid(a0_ref[...] + wa[:, RW_DIM:])
    g_ref[...] = _dot(_sigmoid(dg), g2_ref[...])
    kk = k * kk_ref[...]
    ss = _dot_exact_rhs(kk * kk, ones_ref[...])
    kk = kk * lax.rsqrt(ss + KK_EPS)
    k2 = k * (1.0 + (a - 1.0) * ka_ref[...])
    r_ref[...] = r
    k_ref[...] = k2
    v_ref[...] = v
    av_ref[...] = -kk
    bv_ref[...] = kk * a
    bonus_ref[...] = _dot_exact_rhs(r * k2 * rk_ref[...], ones_ref[...]) * v


def _rwkv_pre(p_rw, shift_prev, seq_len, lw, tm):
    n, c = p_rw.shape
    b = n // seq_len
    if seq_len % tm == 0:
        tiles_per_seq = seq_len // tm
        before = [p_rw, shift_prev[:, None, :]]
        before_specs = [pl.BlockSpec((8, c), lambda i: (jnp.maximum(i * (tm // 8) - 1, 0), 0)),
                        pl.BlockSpec((1, 1, c), lambda i: (i // tiles_per_seq, 0, 0))]
    else:
        tiles_per_seq = None
        prev = _shift_rows(p_rw.reshape(b, seq_len, c), shift_prev[:, None, :], 1).reshape(n, c)
        before = [prev]
        before_specs = [pl.BlockSpec((tm, c), lambda i: (i, 0))]
    ones = _head_block_ones(RW_DIM, 1.0)
    w_lora = lw["w2"].shape[0]
    a_lora = lw["a2"].shape[0]
    wwa = jnp.zeros((w_lora + a_lora, 2 * RW_DIM), F32)
    wwa = wwa.at[:w_lora, :RW_DIM].set(lw["w2"]).at[w_lora:, RW_DIM:].set(lw["a2"]).astype(BF16)
    g2 = lw["g2"].astype(BF16)
    vec = lambda a: a.reshape(1, -1)
    smalls = [vec(lw["mu_shift"]), vec(lw["w0"]), vec(lw["a0"]), vec(lw["k_k"]), vec(lw["k_a"]),
              vec(lw["r_k"]), wwa, g2, ones]
    row = lambda w: pl.BlockSpec((tm, w), lambda i: (i, 0))
    full = lambda a: pl.BlockSpec(a.shape, lambda i: (0,) * a.ndim)
    return pl.pallas_call(
        functools.partial(_rwkv_pre_kernel, tiles_per_seq=tiles_per_seq),
        grid=(n // tm,),
        in_specs=[row(c)] + before_specs + [full(a) for a in smalls],
        out_specs=[row(RW_DIM)] * 8,
        out_shape=[jax.ShapeDtypeStruct((n, RW_DIM), F32)] * 8,
        compiler_params=_cparams("parallel"),
        name="rwkv_pre",
    )(p_rw, *before, *smalls)


def _wkv_kernel(r_ref, lw_ref, k_ref, v_ref, a_ref, b_ref, s0_ref, y_ref, s_ref, *, nseq, tlen):
    P = WKV_ROWS
    HP = RW_HEADS * P
    c = pl.program_id(1)

    @pl.when(c == 0)
    def _():
        s_ref[...] = s0_ref[...]

    prow = lax.broadcasted_iota(jnp.int32, (P, P), 0)
    pcol = lax.broadcasted_iota(jnp.int32, (P, P), 1)
    same_seq = (prow // tlen) == (pcol // tlen)
    tri_incl = jnp.where(same_seq & (pcol <= prow), 1.0, 0.0).astype(BF16)
    last_sel = jnp.where(same_seq & (pcol == (prow // tlen) * tlen + (tlen - 1)), 1.0, 0.0).astype(BF16)
    srow = lax.broadcasted_iota(jnp.int32, (HP, HP), 0)
    scol = lax.broadcasted_iota(jnp.int32, (HP, HP), 1)
    same_blk = ((srow // tlen) == (scol // tlen))
    m_strict = same_blk & (scol < srow)
    m_incl = same_blk & (scol <= srow)
    eye = jnp.where(srow == scol, 1.0, 0.0)
    hrow = lax.broadcasted_iota(jnp.int32, (HP, RW_DIM), 0)
    head_mask = (hrow // P) == (lax.broadcasted_iota(jnp.int32, (HP, RW_DIM), 1) // HEAD_DIM)
    seq_of_row = (hrow % P) // tlen

    def stack(x):
        return jnp.where(head_mask, jnp.concatenate([x] * RW_HEADS, axis=0), 0.0)

    groups = range(r_ref.shape[0])
    lw = [lw_ref[g] for g in groups]
    cw = [_dot_exact_lhs(tri_incl, x) for x in lw]
    cw_end = [_dot_exact_lhs(last_sel, x) for x in cw]
    w_inv = [jnp.exp(-x) for x in cw]
    w_toend = [jnp.exp(e - x) for e, x in zip(cw_end, cw)]
    a_s = [stack(a_ref[g] * jnp.exp(cw[g] - lw[g])) for g in groups]
    r_s = [stack(r_ref[g] * jnp.exp(cw[g])) for g in groups]
    b_s = [stack(b_ref[g] * w_inv[g]) for g in groups]
    k_s = [stack(k_ref[g] * w_inv[g]) for g in groups]
    v_s = [stack(v_ref[g]) for g in groups]
    bend_s = [stack(b_ref[g] * w_toend[g]) for g in groups]
    kend_s = [stack(k_ref[g] * w_toend[g]) for g in groups]

    n_ab = [jnp.where(m_strict, _dg(a_s[g], b_s[g], 1, 1), 0.0) for g in groups]
    a_ak = [jnp.where(m_strict, _dg(a_s[g], k_s[g], 1, 1), 0.0) for g in groups]
    a_rb = [jnp.where(m_incl, _dg(r_s[g], b_s[g], 1, 1), 0.0) for g in groups]
    a_rk = [jnp.where(m_incl, _dg(r_s[g], k_s[g], 1, 1), 0.0) for g in groups]

    t_inv = [eye + x for x in n_ab]
    pw = n_ab
    span = 2
    while span < tlen:
        pw = [_dot(x, x) for x in pw]
        t_inv = [t + _dot(t, x) for t, x in zip(t_inv, pw)]
        span *= 2

    ap_s = [_dot(t_inv[g], a_s[g]) for g in groups]
    akv = [_dot(a_ak[g], v_s[g]) for g in groups]
    u1_s = [_dot(t_inv[g], akv[g]) for g in groups]
    y_s = [_dot(a_rb[g], u1_s[g]) + _dot(a_rk[g], v_s[g]) for g in groups]
    rp_s = [r_s[g] + _dot(a_rb[g], ap_s[g]) for g in groups]
    wc_row = [jnp.exp(x) for x in cw_end]

    for i in range(nseq):
        if nseq == 1:
            sel = lambda x: x
        else:
            sel = lambda x: jnp.where(seq_of_row == i, x, 0.0)
        s_i = [s_ref[g * nseq + i] for g in groups]
        u_i = [_dg(sel(ap_s[g]), s_i[g], 1, 1) + sel(u1_s[g]) for g in groups]
        y_s = [y_s[g] + _dg(sel(rp_s[g]), s_i[g], 1, 1) for g in groups]
        for g in groups:
            wc_i = wc_row[g][i * tlen:i * tlen + 1, :]
            s_ref[g * nseq + i] = (s_i[g] * wc_i + _dg(u_i[g], sel(bend_s[g]), 0, 0)
                                   + _dg(sel(v_s[g]), sel(kend_s[g]), 0, 0))
    for g in groups:
        y = y_s[g][0:P]
        for h in range(1, RW_HEADS):
            y = y + y_s[g][h * P:(h + 1) * P]
        y_ref[g] = y


def _wkv(r, lw, k, v, av, bv, s0_bd, nseq, tlen, groups):
    g, rows, _ = r.shape
    n_chunks = rows // WKV_ROWS
    blk = pl.BlockSpec((groups, WKV_ROWS, RW_DIM), lambda i, c: (i, c, 0))
    sblk = pl.BlockSpec((groups * nseq, RW_DIM, RW_DIM), lambda i, c: (i, 0, 0))
    return pl.pallas_call(
        functools.partial(_wkv_kernel, nseq=nseq, tlen=tlen),
        grid=(g // groups, n_chunks),
        in_specs=[blk] * 6 + [sblk],
        out_specs=[blk, sblk],
        out_shape=[jax.ShapeDtypeStruct(r.shape, F32), jax.ShapeDtypeStruct(s0_bd.shape, F32)],
        compiler_params=_cparams("parallel", "arbitrary"),
        name="wkv",
    )(r, lw, k, v, av, bv, s0_bd)


def _state_to_bd(s):
    b, h, n, _ = s.shape
    eye = jnp.eye(h, dtype=s.dtype)
    return jnp.einsum("bhvk,hg->bhvgk", s, eye).reshape(b, h * n, h * n)


def _state_from_bd(s_bd, h):
    b, hn, _ = s_bd.shape
    n = hn // h
    s5 = s_bd.reshape(b, h, n, h, n)
    return jnp.stack([s5[:, i, :, i, :] for i in range(h)], axis=1)


ATT_NEG = -1e30


def _attn_prompt_kernel(bias_ref, tq_ref, tj_ref, qt_ref, k_ref, vt_ref, o_ref,
                        z_ref, sp_ref, lb_ref, w_ref, zinit_ref, acc_ref, carry_ref):
    grp = pl.program_id(1)
    nk, gw, blk = qt_ref.shape[1:]
    n_tiles = nk * (nk + 1) // 2
    row_head = lax.broadcasted_iota(jnp.int32, (gw, blk), 0) // HEAD_DIM
    krow = lax.broadcasted_iota(jnp.int32, (blk + 8, blk), 0)
    kcol = lax.broadcasted_iota(jnp.int32, (blk + 8, blk), 1)
    neg_sums = jnp.where((kcol > krow) | (krow >= blk), -1.0, 0.0).astype(BF16)
    hidden = (lax.broadcasted_iota(jnp.int32, (blk, blk), 0) >= lax.broadcasted_iota(jnp.int32, (blk, blk), 1))
    for h in range(ATT_HEADS):
        bias2 = bias_ref[grp * ATT_HEADS + h] * LOG2E
        zinit_ref[0, h] = jnp.full((blk, blk), bias2, F32)
        zinit_ref[1, h] = jnp.where(hidden, ATT_NEG, bias2)
    for ref in (z_ref, sp_ref, lb_ref, w_ref, acc_ref, carry_ref):
        ref[...] = jnp.zeros_like(ref)

    def step(i, slot_new):
        slot_old = 1 - slot_new

        f4 = jnp.clip(i - 3, 0, n_tiles - 1)
        q4 = tq_ref[f4]
        j4 = tj_ref[f4]
        keep = jnp.where(j4 == q4, 0.0, 1.0)
        for h in range(ATT_HEADS):
            rows = slice(h * HEAD_DIM, (h + 1) * HEAD_DIM)
            pv = jnp.dot(vt_ref[0, j4, rows, :], w_ref[slot_old, h], preferred_element_type=F32)
            acc_ref[rows, :] = acc_ref[rows, :] * keep + pv
        o_ref[0, q4] = acc_ref[...]

        f3 = jnp.clip(i - 2, 0, n_tiles - 1)
        valid3 = (i >= 2) & (i <= n_tiles + 1)
        first3 = tj_ref[f3] == tq_ref[f3]
        null3 = jnp.where(valid3, 0.0, ATT_NEG)
        count3 = jnp.where(valid3, 1.0, 0.0)
        for h in range(ATT_HEADS):
            sums = jnp.dot(neg_sums, sp_ref[slot_old, h], preferred_element_type=F32)
            carry = jnp.where(first3, 0.0, carry_ref[h])
            w_ref[slot_new, h] = jnp.exp2(lb_ref[slot_old, h] + sums[0:blk] + (carry + null3)).astype(BF16)
            carry_ref[h] = carry + sums[blk:blk + 1] * count3

        for h in range(ATT_HEADS):
            z = z_ref[slot_old, h]
            sp = jnp.maximum(z, 0.0) + jnp.log(1.0 + jnp.exp2(-jnp.abs(z))) * LOG2E
            sp_ref[slot_new, h] = sp.astype(BF16)
            lb_ref[slot_new, h] = z - sp

        f1 = jnp.minimum(i, n_tiles - 1)
        q1 = tq_ref[f1]
        j1 = tj_ref[f1]
        diag1 = jnp.where(j1 == q1, 1, 0)
        k_blk = k_ref[0, pl.ds(pl.multiple_of(j1 * blk, blk), blk), :]
        q_grp = qt_ref[0, q1]
        for h in range(ATT_HEADS):
            q_h = jnp.where(row_head == h, q_grp, jnp.zeros_like(q_grp))
            z_ref[slot_new, h] = zinit_ref[diag1, h] + jnp.dot(k_blk, q_h, preferred_element_type=F32)

    def body(pair, _):
        step(2 * pair, 0)
        step(2 * pair + 1, 1)
        return 0

    lax.fori_loop(0, (n_tiles + 3 + 1) // 2, body, 0)


def _attn_prompt(q_t, k_bf16, v_t, sb_bias, b, t):
    _, nk, _, blk = q_t.shape
    gw = ATT_HEADS * HEAD_DIM
    k3 = k_bf16.reshape(b, t, SB_DIM)
    tiles = [(qi, j) for qi in range(nk) for j in range(qi, -1, -1)]
    tile_q = jnp.array([qi for qi, _ in tiles], jnp.int32)
    tile_j = jnp.array([j for _, j in tiles], jnp.int32)
    blocks = pl.BlockSpec((1, nk, gw, blk), lambda i, g, *_: (i, 0, g, 0))
    grid_spec = pltpu.PrefetchScalarGridSpec(
        num_scalar_prefetch=3,
        grid=(b, SB_HEADS // ATT_HEADS),
        in_specs=[blocks, pl.BlockSpec((1, t, gw), lambda i, g, *_: (i, 0, g)), blocks],
        out_specs=blocks,
        scratch_shapes=[pltpu.VMEM((2, ATT_HEADS, blk, blk), F32), pltpu.VMEM((2, ATT_HEADS, blk, blk), BF16),
                        pltpu.VMEM((2, ATT_HEADS, blk, blk), F32), pltpu.VMEM((2, ATT_HEADS, blk, blk), BF16),
                        pltpu.VMEM((2, ATT_HEADS, blk, blk), F32),
                        pltpu.VMEM((gw, blk), F32), pltpu.VMEM((ATT_HEADS, 1, blk), F32)],
    )
    return pl.pallas_call(
        _attn_prompt_kernel,
        grid_spec=grid_spec,
        out_shape=jax.ShapeDtypeStruct((b, nk, SB_DIM, blk), F32),
        compiler_params=_cparams("parallel", "parallel"),
        name="attn_prompt",
    )(sb_bias, tile_q, tile_j, q_t, k3, v_t)


def _attn_sample_kernel(pt_ref, q_ref, bias_ref, kn_ref, vn_ref, *refs, tlen, n_pages):
    k_refs = refs[:n_pages]
    v_refs = refs[n_pages:2 * n_pages]
    o_ref = refs[2 * n_pages]
    rows = SB_HEADS * tlen
    rhead = lax.broadcasted_iota(jnp.int32, (rows, SB_DIM), 0) // tlen
    lhead = lax.broadcasted_iota(jnp.int32, (rows, SB_DIM), 1) // HEAD_DIM
    head_mask = rhead == lhead
    q_s = jnp.where(head_mask, jnp.concatenate([q_ref[0]] * SB_HEADS, axis=0), 0.0).astype(BF16)
    jrow = lax.broadcasted_iota(jnp.int32, (PAGE_SIZE, 2 * PAGE_SIZE), 0)
    jcol = lax.broadcasted_iota(jnp.int32, (PAGE_SIZE, 2 * PAGE_SIZE), 1)
    tri_ones = jnp.where((jrow > jcol) | (jcol >= PAGE_SIZE), 1.0, 0.0).astype(BF16)
    bias = bias_ref[...]

    def scores(z, mask):
        z = z + bias
        sp = jnp.maximum(z, 0.0) + jnp.log(1.0 + jnp.exp(-jnp.abs(z)))
        lk = -sp
        if mask is not None:
            lk = jnp.where(mask, lk, 0.0)
        lk_hi, lk_lo = _split2(lk)
        sums = (jnp.dot(lk_hi, tri_ones, preferred_element_type=F32)
                + jnp.dot(lk_lo, tri_ones, preferred_element_type=F32))
        return (z - sp) + sums[:, 0:PAGE_SIZE], sums[:, PAGE_SIZE:]

    pad = jnp.zeros((PAGE_SIZE - tlen, SB_DIM), F32)
    k_new = jnp.concatenate([kn_ref[0], pad], axis=0)
    v_new = jnp.concatenate([vn_ref[0], pad], axis=0)
    new_mask = (lax.broadcasted_iota(jnp.int32, (rows, PAGE_SIZE), 1)
                < lax.broadcasted_iota(jnp.int32, (rows, PAGE_SIZE), 0) % tlen)
    order = list(reversed(range(n_pages)))
    zs = [_dg(q_s, k_new, 1, 1)] + [_dot(q_s, k_refs[j][0, 0]) for j in order]
    parts = [scores(zs[0], new_mask)] + [scores(z, None) for z in zs[1:]]
    carry = jnp.zeros((rows, PAGE_SIZE), F32)
    ws = []
    for n, (logw, total) in enumerate(parts):
        w = jnp.exp(logw + carry)
        ws.append(jnp.where(new_mask, w, 0.0) if n == 0 else w)
        carry = carry + total
    acc = _dot(ws[0], v_new)
    for w, j in zip(ws[1:], order):
        acc = acc + _dg(w, v_refs[j][0, 0], 1, 1)
    acc = jnp.where(head_mask, acc, 0.0)
    out = acc[0:tlen]
    for h in range(1, SB_HEADS):
        out = out + acc[h * tlen:(h + 1) * tlen]
    o_ref[0] = out


def _attn_sample(q, k_new, v_new, cache_kt, cache_vt, layer, page_table, sb_bias, b, tlen):
    n_pages = page_table.shape[1]
    rows = SB_HEADS * tlen
    bias_rows = jnp.broadcast_to(jnp.repeat(sb_bias, tlen)[:, None], (rows, PAGE_SIZE)).astype(F32)
    seq = lambda a: a.reshape(b, tlen, SB_DIM)
    tok = pl.BlockSpec((1, tlen, SB_DIM), lambda i, pt: (i, 0, 0))
    page = lambda j: pl.BlockSpec((1, 1, SB_DIM, PAGE_SIZE), lambda i, pt, j=j: (layer, pt[i, j], 0, 0))
    pages = [page(j) for j in range(n_pages)]
    grid_spec = pltpu.PrefetchScalarGridSpec(
        num_scalar_prefetch=1,
        grid=(b,),
        in_specs=[tok, pl.BlockSpec((rows, PAGE_SIZE), lambda i, pt: (0, 0)), tok, tok] + pages + pages,
        out_specs=tok,
    )
    out = pl.pallas_call(
        functools.partial(_attn_sample_kernel, tlen=tlen, n_pages=n_pages),
        grid_spec=grid_spec,
        out_shape=jax.ShapeDtypeStruct((b, tlen, SB_DIM), F32),
        compiler_params=_cparams("parallel"),
        name="attn_sample",
    )(page_table, seq(q), bias_rows, seq(k_new), seq(v_new), *([cache_kt] * n_pages), *([cache_vt] * n_pages))
    return out.reshape(b * tlen, SB_DIM)


def _tail_kernel(x_ref, y_ref, bonus_ref, g_ref, cb_ref, cc_ref, cx_ref, *refs, tiles_per_seq):
    n_before = 4 if tiles_per_seq is None else 3
    before_refs = refs[:n_before]
    (osb_ref, p_ref, gnw_ref, gnb_ref, convw_ref, ones_ref, wout_ref, gmlp_ref, wup_ref, wdown_ref,
     gple_ref, wgate_ref, wproj_ref, o_ref, x1_ref, h2_ref, acc_ref) = refs[n_before:]
    f = pl.program_id(1)
    nf = pl.num_programs(1)

    @pl.when(f == 0)
    def _():
        y = y_ref[...]
        mu = _dot_exact_rhs(y, ones_ref[...])
        yc = y - mu
        var = _dot_exact_rhs(yc * yc, ones_ref[...])
        yn = yc * lax.rsqrt(var + GN_EPS) * gnw_ref[...] + gnb_ref[...]
        o_rw = (yn + bonus_ref[...]) * g_ref[...]
        cw = convw_ref[...]
        u = cc_ref[...] * cx_ref[...]
        if tiles_per_seq is None:
            cc1_ref, cx1_ref, cc2_ref, cx2_ref = before_refs
            u1 = cc1_ref[...] * cx1_ref[...]
            u2 = cc2_ref[...] * cx2_ref[...]
        else:
            cch_ref, cxh_ref, state_ref = before_refs
            u_halo = cch_ref[...] * cxh_ref[...]
            is_first = pl.program_id(0) % tiles_per_seq == 0
            u1 = _rows_before(u, u_halo, state_ref[0], is_first, 1)
            u2 = _rows_before(u, u_halo, state_ref[0], is_first, 2)
        o_cv = cb_ref[...] * (cw[0:1] * u2 + cw[1:2] * u1 + cw[2:3] * u)
        mix = _dot(o_rw, wout_ref[0:RW_DIM, :])
        mix = mix + _dot(o_cv, wout_ref[RW_DIM:RW_DIM + CONV_DIM, :])
        w_sb = wout_ref[RW_DIM + CONV_DIM:, :]
        if len(osb_ref.shape) == 2:
            mix = mix + _dot(osb_ref[...], w_sb)
        else:
            mix = mix + jnp.concatenate(
                [_dot(osb_ref[0, s].T, w_sb) for s in range(osb_ref.shape[1])], axis=0)
        x1 = x_ref[...] + mix
        x1_ref[...] = x1
        h2_ref[...] = _rmsnorm_rows(x1, gmlp_ref[...]).astype(BF16)
        acc_ref[...] = jnp.zeros_like(acc_ref)

    up = jnp.dot(h2_ref[...], wup_ref[...], preferred_element_type=F32)
    act = jnp.square(jnp.maximum(up, 0.0))
    acc_ref[...] += _dot(act, wdown_ref[...])

    @pl.when(f == nf - 1)
    def _():
        x2 = x1_ref[...] + acc_ref[...]
        gate = _sigmoid(_dot(_rmsnorm_rows(x2, gple_ref[...]), wgate_ref[...]))
        o_ref[...] = x2 + gate * _dot(p_ref[...], wproj_ref[...])


def _tail(x2d, y, bonus, g, p_cv, conv_prev, seq_len, o_sb, p3d, layer, lw, wts, tm, tf):
    n, d = x2d.shape
    b = n // seq_len
    d_ff = wts["w_up"].shape[1]
    ones = _head_block_ones(RW_DIM, 1.0 / HEAD_DIM)
    vec = lambda a: a.reshape(1, -1)
    row = lambda w, col=0: pl.BlockSpec((tm, w), lambda i, f, col=col: (i, col))
    full = lambda a: pl.BlockSpec(a.shape, lambda i, f: (0,) * a.ndim)
    gnw, gnb, gmlp, gple = vec(lw["gn_w"]), vec(lw["gn_b"]), vec(lw["g_mlp"]), vec(lw["g_ple"])
    if o_sb.ndim == 2:
        osb_spec = row(SB_DIM)
    else:
        _, nblk, _, blk = o_sb.shape
        osb_tiles = nblk * blk // tm
        osb_spec = pl.BlockSpec((1, tm // blk, SB_DIM, blk), lambda i, f: (i // osb_tiles, i % osb_tiles, 0, 0))
    if seq_len % tm == 0:
        tiles_per_seq = seq_len // tm
        halo = lambda col: pl.BlockSpec((8, CONV_DIM), lambda i, f: (jnp.maximum(i * (tm // 8) - 1, 0), col))
        before = [p_cv, p_cv, conv_prev]
        before_specs = [halo(1), halo(2),
                        pl.BlockSpec((1, CONV_K - 1, CONV_DIM), lambda i, f: (i // tiles_per_seq, 0, 0))]
    else:
        tiles_per_seq = None
        p_cv3 = p_cv.reshape(b, seq_len, 3 * CONV_DIM)
        state_rows = lambda s: jnp.concatenate([jnp.zeros_like(s), s, jnp.ones_like(s)], axis=-1)
        p_cv1 = _shift_rows(p_cv3, state_rows(conv_prev[:, 1:2]), 1).reshape(n, 3 * CONV_DIM)
        p_cv2 = _shift_rows(p_cv3, state_rows(conv_prev), 2).reshape(n, 3 * CONV_DIM)
        before = [p_cv1, p_cv1, p_cv2, p_cv2]
        before_specs = [row(CONV_DIM, 1), row(CONV_DIM, 2), row(CONV_DIM, 1), row(CONV_DIM, 2)]
    in_specs = [row(d), row(RW_DIM), row(RW_DIM), row(RW_DIM),
                row(CONV_DIM, 0), row(CONV_DIM, 1), row(CONV_DIM, 2)] + before_specs + [
                osb_spec, pl.BlockSpec((None, tm, p3d.shape[2]), lambda i, f: (layer, i, 0)),
                full(gnw), full(gnb), full(lw["conv_w"]), full(ones), full(wts["w_out"]), full(gmlp),
                pl.BlockSpec((d, tf), lambda i, f: (0, f)), pl.BlockSpec((tf, d), lambda i, f: (f, 0)),
                full(gple), full(wts["w_ple_gate"]), full(wts["w_ple_proj"])]
    return pl.pallas_call(
        functools.partial(_tail_kernel, tiles_per_seq=tiles_per_seq),
        grid=(n // tm, d_ff // tf),
        in_specs=in_specs,
        out_specs=pl.BlockSpec((tm, d), lambda i, f: (i, 0)),
        out_shape=jax.ShapeDtypeStruct((n, d), F32),
        scratch_shapes=[pltpu.VMEM((tm, d), F32), pltpu.VMEM((tm, d), BF16), pltpu.VMEM((tm, d), F32)],
        compiler_params=_cparams("parallel", "arbitrary"),
        name="tail",
    )(x2d, y, bonus, g, p_cv, p_cv, p_cv, *before, o_sb, p3d,
      gnw, gnb, lw["conv_w"], ones, wts["w_out"], gmlp, wts["w_up"], wts["w_down"],
      gple, wts["w_ple_gate"], wts["w_ple_proj"])


def _conv_state_kernel(cv_ref, o_ref):
    o_ref[...] = cv_ref[:, :, CONV_DIM:2 * CONV_DIM] * cv_ref[:, :, 2 * CONV_DIM:3 * CONV_DIM]


def _conv_state(p_cv3):
    b, t, c = p_cv3.shape
    return pl.pallas_call(
        _conv_state_kernel,
        grid=(1,),
        in_specs=[pl.BlockSpec((b, 8, c), lambda i: (0, t // 8 - 1, 0))],
        out_specs=pl.BlockSpec((b, 8, CONV_DIM), lambda i: (0, 0, 0)),
        out_shape=jax.ShapeDtypeStruct((b, 8, CONV_DIM), F32),
        name="conv_state",
    )(p_cv3)


def _shift_rows(x3, first_rows, n):
    return jnp.concatenate([first_rows, x3[:, :-n]], axis=1)


def _layer(x2d, p3d, layer, b, t, shift_prev, wkv_prev_bd, conv_prev, attn_fn, att_blk, lw, wts, tm, tf,
           wkv_nseq, wkv_tlen, wkv_groups):
    n, d = x2d.shape
    p_rw, p_cv, k_new, v_new, *attn_ops = _in_proj(x2d, lw["g_mix"], wts["w_in"], lw["q_gain"], lw["k_gain"],
                                                   tm, t, att_blk)
    rw_cols = p_rw.shape[1]
    p_rw3 = p_rw.reshape(b, t, rw_cols)
    r, lwd, k2, v, av, bv, g, bonus = _rwkv_pre(p_rw, shift_prev, t, lw, tm)
    grp = lambda a: a.reshape(-1, (n // (b // wkv_nseq)), RW_DIM) if wkv_nseq > 1 else a.reshape(b, t, RW_DIM)
    y, s_bd = _wkv(grp(r), grp(lwd), grp(k2), grp(v), grp(av), grp(bv), wkv_prev_bd, wkv_nseq, wkv_tlen,
                   wkv_groups)
    y = y.reshape(n, RW_DIM)

    o_sb = attn_fn(k_new, v_new, *attn_ops)

    x_out = _tail(x2d, y, bonus, g, p_cv, conv_prev, t, o_sb, p3d, layer, lw, wts, tm, tf)
    conv_new = _conv_state(p_cv.reshape(b, t, 3 * CONV_DIM))[:, 8 - (CONV_K - 1):]
    return x_out, k_new, v_new, s_bd, p_rw3[:, -1], conv_new


def _run_trunk(x, p, shift, wkv, conv, attn_builder, att_blk, params, bf16_w, tm, tf,
               wkv_nseq, wkv_tlen, wkv_groups):
    b, t, d = x.shape
    depth = p.shape[0]
    x2d = x.reshape(b * t, d)
    ks, vs, wkvs, shifts, convs = [], [], [], [], []
    for l in range(depth):
        lw = {name: arr[l] for name, arr in params.items()}
        wts = {name: arr[l] for name, arr in bf16_w.items()}
        x2d, k_new, v_new, s_bd, shift_new, conv_new = _layer(
            x2d, p.reshape(depth, b * t, -1), l, b, t, shift[l], _state_to_bd(wkv[l]), conv[l],
            attn_builder(l, lw), att_blk, lw, wts, tm, tf, wkv_nseq, wkv_tlen, wkv_groups)
        ks.append(k_new.reshape(b, t, SB_HEADS, HEAD_DIM))
        vs.append(v_new.reshape(b, t, SB_HEADS, HEAD_DIM))
        wkvs.append(_state_from_bd(s_bd, RW_HEADS))
        shifts.append(shift_new)
        convs.append(conv_new)
    return (x2d.reshape(b, t, d), jnp.stack(ks), jnp.stack(vs), jnp.stack(wkvs), jnp.stack(shifts),
            jnp.stack(convs))


def kernel(x_prompt, x_sample, cache_k, cache_v, state_wkv, state_shift, state_conv, page_table,
           p_prompt, p_sample, g_mix, w_in, mu_shift, w0, w2, a0, a2, g2, k_k, k_a, r_k,
           gn_w, gn_b, conv_w, q_gain, k_gain, sb_bias, w_out, g_mlp, w_up, w_down, g_ple,
           w_ple_gate, w_ple_proj):
    params = dict(g_mix=g_mix, mu_shift=mu_shift, w0=w0, w2=w2, a0=a0, a2=a2, g2=g2,
                  k_k=k_k, k_a=k_a, r_k=r_k, gn_w=gn_w, gn_b=gn_b, conv_w=conv_w,
                  q_gain=q_gain, k_gain=k_gain, sb_bias=sb_bias, g_mlp=g_mlp, g_ple=g_ple)
    bf16_w = dict(w_in=w_in.astype(BF16), w_out=w_out.astype(BF16), w_up=w_up.astype(BF16),
                  w_down=w_down.astype(BF16), w_ple_gate=w_ple_gate.astype(BF16),
                  w_ple_proj=w_ple_proj.astype(BF16))
    depth = w_in.shape[0]
    b, t, d = x_prompt.shape
    db, dt, _ = x_sample.shape
    rw_proj = state_shift.shape[-1]
    d_ff = w_up.shape[-1]
    tf = min(1024, d_ff)

    n_phys = cache_k.shape[1]
    page_view = lambda c: c.transpose(0, 1, 3, 4, 2).reshape(depth, n_phys, SB_DIM, PAGE_SIZE)
    cache_kt, cache_vt = page_view(cache_k), page_view(cache_v)

    def prompt_attn(l, lw):
        return lambda k, v, q_t, k_bf16, v_t: _attn_prompt(q_t, k_bf16, v_t, lw["sb_bias"], b, t)

    def sample_attn(l, lw):
        return lambda k, v, q: _attn_sample(q, k, v, cache_kt, cache_vt, l, page_table, lw["sb_bias"], db, dt)

    dtp = x_prompt.dtype
    shift0 = jnp.zeros((depth, b, rw_proj), dtp)
    wkv0 = jnp.zeros((depth, b, RW_HEADS, HEAD_DIM, HEAD_DIM), dtp)
    conv0 = jnp.zeros((depth, b, CONV_K - 1, CONV_DIM), dtp)
    (y_prompt, k_prompt, v_prompt, wkv_prompt, shift_prompt, conv_prompt) = _run_trunk(
        x_prompt, p_prompt, shift0, wkv0, conv0, prompt_attn, min(ATT_BLOCK, t), params, bf16_w,
        min(512, t), tf, 1, WKV_ROWS, b)
    seqs_per_group = WKV_ROWS // dt
    (y_sample, k_sample, v_sample, wkv_sample, shift_sample, conv_sample) = _run_trunk(
        x_sample, p_sample, state_shift, state_wkv, state_conv, sample_attn, None, params, bf16_w,
        min(512, db * dt), tf, seqs_per_group, dt, min(2, db // seqs_per_group))
    return (y_prompt, y_sample, k_prompt, v_prompt, wkv_prompt, shift_prompt, conv_prompt,
            k_sample, v_sample, wkv_sample, shift_sample, conv_sample)
```

```python
import functools

import jax
import jax.numpy as jnp
from jax import lax
from jax.experimental import pallas as pl
from jax.experimental.pallas import tpu as pltpu

F32 = jnp.float32
BF16 = jnp.bfloat16

HEAD_DIM = 64
RW_HEADS = 4
RW_DIM = RW_HEADS * HEAD_DIM
LORA_WA = 128
CONV_DIM = 256
CONV_K = 3
SB_HEADS = 8
SB_DIM = SB_HEADS * HEAD_DIM
PAGE_SIZE = 128
RMS_EPS = 1e-6
GN_EPS = 64e-5
KK_EPS = 1e-12
LOG2E = 1.4426950408889634

WKV_ROWS = 64
ATT_BLOCK = 256
ATT_HEADS = 4
VMEM_LIMIT_BYTES = 56 * 1024 * 1024


def _cparams(*sem):
    return pltpu.CompilerParams(dimension_semantics=sem, vmem_limit_bytes=VMEM_LIMIT_BYTES)


def _dot(a, b):
    return jnp.dot(a.astype(BF16), b.astype(BF16), preferred_element_type=F32)


def _dg(a, b, ca, cb):
    return lax.dot_general(a.astype(BF16), b.astype(BF16), (((ca,), (cb,)), ((), ())),
                           preferred_element_type=F32)


def _split2(a):
    hi = a.astype(BF16)
    lo = (a - hi.astype(F32)).astype(BF16)
    return hi, lo


def _dot_exact_rhs(a, b_bf16):
    h1 = a.astype(BF16)
    r1 = a - h1.astype(F32)
    h2 = r1.astype(BF16)
    h3 = (r1 - h2.astype(F32)).astype(BF16)
    out = jnp.dot(h1, b_bf16, preferred_element_type=F32)
    out = out + jnp.dot(h2, b_bf16, preferred_element_type=F32)
    out = out + jnp.dot(h3, b_bf16, preferred_element_type=F32)
    return out


def _dot_exact_lhs(a_bf16, b):
    h1 = b.astype(BF16)
    r1 = b - h1.astype(F32)
    h2 = r1.astype(BF16)
    h3 = (r1 - h2.astype(F32)).astype(BF16)
    out = jnp.dot(a_bf16, h1, preferred_element_type=F32)
    out = out + jnp.dot(a_bf16, h2, preferred_element_type=F32)
    out = out + jnp.dot(a_bf16, h3, preferred_element_type=F32)
    return out


def _softplus(x):
    return jnp.maximum(x, 0.0) + jnp.log1p(jnp.exp(-jnp.abs(x)))


def _sigmoid(x):
    return 1.0 / (1.0 + jnp.exp(-x))


def _rmsnorm_rows(x, g):
    ms = jnp.mean(x * x, axis=-1, keepdims=True)
    return x * lax.rsqrt(ms + RMS_EPS) * g


def _head_block_ones(width, scale):
    i = jnp.arange(width) // HEAD_DIM
    return jnp.where(i[:, None] == i[None, :], scale, 0.0).astype(BF16)


def _rows_before(cur, halo, first_rows, is_first, n):
    row = lax.broadcasted_iota(jnp.int32, cur.shape, 0)
    out = pltpu.roll(cur, n, 0)
    n_halo, n_first = halo.shape[0], first_rows.shape[0]
    for k in range(n):
        before = jnp.where(is_first, first_rows[n_first - n + k:n_first - n + k + 1],
                           halo[n_halo - n + k:n_halo - n + k + 1])
        out = jnp.where(row == k, before, out)
    return out


def _rwkv_heads(p, prev, mu_ref, w0_ref, a0_ref, kk_ref, ka_ref, rk_ref, wwa_ref, g2_ref, ones_ref,
                r_ref, lw_ref, k_ref, v_ref, av_ref, bv_ref, g_ref, bonus_ref):
    xs = p + (prev - p) * mu_ref[...]
    r = xs[:, 0:RW_DIM]
    k = xs[:, RW_DIM:2 * RW_DIM]
    v = xs[:, 2 * RW_DIM:3 * RW_DIM]
    d = xs[:, 3 * RW_DIM:3 * RW_DIM + LORA_WA]
    dg = xs[:, 3 * RW_DIM + LORA_WA:]
    lane = lax.broadcasted_iota(jnp.int32, d.shape, 1)
    dwa = jnp.where(lane < LORA_WA // 2, jnp.tanh(d), d)
    wa = _dot(dwa, wwa_ref[...])
    w_log = -_softplus(-(w0_ref[...] + wa[:, 0:RW_DIM])) - 0.5
    lw_ref[...] = -jnp.exp(w_log)
    a = _sigmoid(a0_ref[...] + wa[:, RW_DIM:])
    g_ref[...] = _dot(_sigmoid(dg), g2_ref[...])
    kk = k * kk_ref[...]
    ss = _dot_exact_rhs(kk * kk, ones_ref[...])
    kk = kk * lax.rsqrt(ss + KK_EPS)
    k2 = k * (1.0 + (a - 1.0) * ka_ref[...])
    r_ref[...] = r
    k_ref[...] = k2
    v_ref[...] = v
    av_ref[...] = -kk
    bv_ref[...] = kk * a
    bonus_ref[...] = _dot_exact_rhs(r * k2 * rk_ref[...], ones_ref[...]) * v


N_RWKV_PARAMS = 9
N_RWKV_OUTS = 8


def _in_proj_kernel(x_ref, g_ref, w_ref, ebd_ref, qg_ref, kg_ref, state_ref, *refs,
                    att_blk, seq_len, tiles_per_seq):
    rwkv_params = refs[:N_RWKV_PARAMS]
    rw_ref, cv_ref, k_ref, v_ref = refs[N_RWKV_PARAMS:N_RWKV_PARAMS + 4]
    rwkv_outs = refs[N_RWKV_PARAMS + 4:N_RWKV_PARAMS + 4 + N_RWKV_OUTS]
    attn_refs = refs[N_RWKV_PARAMS + 4 + N_RWKV_OUTS:]
    h = _rmsnorm_rows(x_ref[...], g_ref[...]).astype(BF16)
    c0 = rw_ref.shape[1]
    c1 = c0 + cv_ref.shape[1]
    p = jnp.dot(h, w_ref[:, 0:c0], preferred_element_type=F32)
    rw_ref[...] = p
    rows = p.shape[0]
    if tiles_per_seq is None:
        nseq = rows // seq_len
        r_i = lax.broadcasted_iota(jnp.int32, (rows, nseq), 0)
        s_i = lax.broadcasted_iota(jnp.int32, (rows, nseq), 1)
        starts = jnp.where(r_i == s_i * seq_len, 1.0, 0.0).astype(BF16)
        t_local = lax.broadcasted_iota(jnp.int32, p.shape, 0) % seq_len
        prev = jnp.where(t_local == 0, _dot_exact_lhs(starts, state_ref[...]), pltpu.roll(p, 1, 0))
    else:
        attn_refs, last_ref = attn_refs[:-1], attn_refs[-1]

        @pl.when(pl.program_id(0) == 0)
        def _():
            last_ref[...] = jnp.zeros_like(last_ref)

        prev = _rows_before(p, last_ref[...], state_ref[0], pl.program_id(0) % tiles_per_seq == 0, 1)
        last_ref[...] = p[rows - last_ref.shape[0]:rows]
    _rwkv_heads(p, prev, *rwkv_params, *rwkv_outs)
    cv_ref[...] = jnp.dot(h, w_ref[:, c0:c1], preferred_element_type=F32)

    def head_norm(t, gain):
        ms = jnp.dot((t * t).astype(BF16), ebd_ref[...], preferred_element_type=F32)
        return t * lax.rsqrt(ms + RMS_EPS) * gain

    q = jnp.dot(h, w_ref[:, c1:c1 + SB_DIM], preferred_element_type=F32)
    q = head_norm(q, qg_ref[...]) * (HEAD_DIM ** -0.5)
    k = jnp.dot(h, w_ref[:, c1 + SB_DIM:c1 + 2 * SB_DIM], preferred_element_type=F32)
    k = head_norm(k, kg_ref[...])
    k_ref[...] = k
    v = jnp.dot(h, w_ref[:, c1 + 2 * SB_DIM:c1 + 3 * SB_DIM], preferred_element_type=F32)
    v_ref[...] = v
    if att_blk is None:
        (q_ref,) = attn_refs
        q_ref[...] = q
    else:
        qt_ref, kb_ref, vt_ref = attn_refs
        kb_ref[...] = k.astype(BF16)
        q_t = (q * LOG2E).T
        v_t = v.T
        for s in range(qt_ref.shape[1]):
            qt_ref[0, s] = q_t[:, s * att_blk:(s + 1) * att_blk].astype(BF16)
            vt_ref[0, s] = v_t[:, s * att_blk:(s + 1) * att_blk].astype(BF16)


def _in_proj(x2d, shift_prev, lw, w_in_bf16, tm, seq_len, att_blk):
    n, d = x2d.shape
    cols = w_in_bf16.shape[1]
    rw_cols = cols - 3 * CONV_DIM - 3 * SB_DIM
    ebd = _head_block_ones(SB_DIM, 1.0 / HEAD_DIM)
    qg = jnp.tile(lw["q_gain"], SB_HEADS)[None, :]
    kg = jnp.tile(lw["k_gain"], SB_HEADS)[None, :]
    row = lambda w: pl.BlockSpec((tm, w), lambda i: (i, 0))
    full = lambda a: pl.BlockSpec(a.shape, lambda i: (0,) * a.ndim)
    g2d = lw["g_mix"][None, :]
    w_lora = lw["w2"].shape[0]
    a_lora = lw["a2"].shape[0]
    wwa = jnp.zeros((w_lora + a_lora, 2 * RW_DIM), F32)
    wwa = wwa.at[:w_lora, :RW_DIM].set(lw["w2"]).at[w_lora:, RW_DIM:].set(lw["a2"]).astype(BF16)
    vec = lambda a: a.reshape(1, -1)
    rwkv_params = [vec(lw["mu_shift"]), vec(lw["w0"]), vec(lw["a0"]), vec(lw["k_k"]), vec(lw["k_a"]),
                   vec(lw["r_k"]), wwa, lw["g2"].astype(BF16), _head_block_ones(RW_DIM, 1.0)]
    assert len(rwkv_params) == N_RWKV_PARAMS
    scratch = []
    if seq_len % tm == 0:
        tiles_per_seq = seq_len // tm
        state, state_spec = shift_prev[:, None, :], pl.BlockSpec((1, 1, rw_cols), lambda i: (i // tiles_per_seq, 0, 0))
        scratch = [pltpu.VMEM((8, rw_cols), F32)]
    else:
        tiles_per_seq = None
        state, state_spec = shift_prev, pl.BlockSpec((tm // seq_len, rw_cols), lambda i: (i, 0))
    out_specs = [row(rw_cols), row(3 * CONV_DIM), row(SB_DIM), row(SB_DIM)] + [row(RW_DIM)] * N_RWKV_OUTS
    out_shape = [jax.ShapeDtypeStruct((n, rw_cols), F32), jax.ShapeDtypeStruct((n, 3 * CONV_DIM), F32),
                 jax.ShapeDtypeStruct((n, SB_DIM), F32), jax.ShapeDtypeStruct((n, SB_DIM), F32)]
    out_shape += [jax.ShapeDtypeStruct((n, RW_DIM), F32)] * N_RWKV_OUTS
    if att_blk is None:
        out_specs.append(row(SB_DIM))
        out_shape.append(jax.ShapeDtypeStruct((n, SB_DIM), F32))
    else:
        att_tiles = seq_len // tm
        blks = tm // att_blk
        t_spec = pl.BlockSpec((1, blks, SB_DIM, att_blk), lambda i: (i // att_tiles, i % att_tiles, 0, 0))
        t_shape = jax.ShapeDtypeStruct((n // seq_len, seq_len // att_blk, SB_DIM, att_blk), BF16)
        out_specs += [t_spec, row(SB_DIM), t_spec]
        out_shape += [t_shape, jax.ShapeDtypeStruct((n, SB_DIM), BF16), t_shape]
    return pl.pallas_call(
        functools.partial(_in_proj_kernel, att_blk=att_blk, seq_len=seq_len, tiles_per_seq=tiles_per_seq),
        grid=(n // tm,),
        in_specs=[row(d), full(g2d), full(w_in_bf16), full(ebd), full(qg), full(kg), state_spec]
                 + [full(a) for a in rwkv_params],
        out_specs=out_specs,
        out_shape=out_shape,
        scratch_shapes=scratch,
        compiler_params=_cparams("arbitrary"),
        name="in_proj",
    )(x2d, g2d, w_in_bf16, ebd, qg, kg, state, *rwkv_params)


def _wkv_kernel(r_ref, lw_ref, k_ref, v_ref, a_ref, b_ref, s0_ref, y_ref, s_ref, *, nseq, tlen):
    P = WKV_ROWS
    HP = RW_HEADS * P
    c = pl.program_id(1)

    @pl.when(c == 0)
    def _():
        s_ref[...] = s0_ref[...]

    prow = lax.broadcasted_iota(jnp.int32, (P, P), 0)
    pcol = lax.broadcasted_iota(jnp.int32, (P, P), 1)
    same_seq = (prow // tlen) == (pcol // tlen)
    tri_incl = jnp.where(same_seq & (pcol <= prow), 1.0, 0.0).astype(BF16)
    last_sel = jnp.where(same_seq & (pcol == (prow // tlen) * tlen + (tlen - 1)), 1.0, 0.0).astype(BF16)
    srow = lax.broadcasted_iota(jnp.int32, (HP, HP), 0)
    scol = lax.broadcasted_iota(jnp.int32, (HP, HP), 1)
    same_blk = ((srow // tlen) == (scol // tlen))
    m_strict = same_blk & (scol < srow)
    m_incl = same_blk & (scol <= srow)
    eye = jnp.where(srow == scol, 1.0, 0.0)
    hrow = lax.broadcasted_iota(jnp.int32, (HP, RW_DIM), 0)
    head_mask = (hrow // P) == (lax.broadcasted_iota(jnp.int32, (HP, RW_DIM), 1) // HEAD_DIM)
    seq_of_row = (hrow % P) // tlen

    def stack(x):
        return jnp.where(head_mask, jnp.concatenate([x] * RW_HEADS, axis=0), 0.0)

    groups = range(r_ref.shape[0])
    lw = [lw_ref[g] for g in groups]
    cw = [_dot_exact_lhs(tri_incl, x) for x in lw]
    cw_end = [_dot_exact_lhs(last_sel, x) for x in cw]
    w_inv = [jnp.exp(-x) for x in cw]
    w_toend = [jnp.exp(e - x) for e, x in zip(cw_end, cw)]
    a_s = [stack(a_ref[g] * jnp.exp(cw[g] - lw[g])) for g in groups]
    r_s = [stack(r_ref[g] * jnp.exp(cw[g])) for g in groups]
    b_s = [stack(b_ref[g] * w_inv[g]) for g in groups]
    k_s = [stack(k_ref[g] * w_inv[g]) for g in groups]
    v_s = [stack(v_ref[g]) for g in groups]
    bend_s = [stack(b_ref[g] * w_toend[g]) for g in groups]
    kend_s = [stack(k_ref[g] * w_toend[g]) for g in groups]

    n_ab = [jnp.where(m_strict, _dg(a_s[g], b_s[g], 1, 1), 0.0) for g in groups]
    a_ak = [jnp.where(m_strict, _dg(a_s[g], k_s[g], 1, 1), 0.0) for g in groups]
    a_rb = [jnp.where(m_incl, _dg(r_s[g], b_s[g], 1, 1), 0.0) for g in groups]
    a_rk = [jnp.where(m_incl, _dg(r_s[g], k_s[g], 1, 1), 0.0) for g in groups]

    t_inv = [eye + x for x in n_ab]
    pw = n_ab
    span = 2
    while span < tlen:
        pw = [_dot(x, x) for x in pw]
        t_inv = [t + _dot(t, x) for t, x in zip(t_inv, pw)]
        span *= 2

    ap_s = [_dot(t_inv[g], a_s[g]) for g in groups]
    akv = [_dot(a_ak[g], v_s[g]) for g in groups]
    u1_s = [_dot(t_inv[g], akv[g]) for g in groups]
    y_s = [_dot(a_rb[g], u1_s[g]) + _dot(a_rk[g], v_s[g]) for g in groups]
    rp_s = [r_s[g] + _dot(a_rb[g], ap_s[g]) for g in groups]
    wc_row = [jnp.exp(x) for x in cw_end]

    for i in range(nseq):
        if nseq == 1:
            sel = lambda x: x
        else:
            sel = lambda x: jnp.where(seq_of_row == i, x, 0.0)
        s_i = [s_ref[g * nseq + i] for g in groups]
        u_i = [_dg(sel(ap_s[g]), s_i[g], 1, 1) + sel(u1_s[g]) for g in groups]
        y_s = [y_s[g] + _dg(sel(rp_s[g]), s_i[g], 1, 1) for g in groups]
        for g in groups:
            wc_i = wc_row[g][i * tlen:i * tlen + 1, :]
            s_ref[g * nseq + i] = (s_i[g] * wc_i + _dg(u_i[g], sel(bend_s[g]), 0, 0)
                                   + _dg(sel(v_s[g]), sel(kend_s[g]), 0, 0))
    for g in groups:
        y = y_s[g][0:P]
        for h in range(1, RW_HEADS):
            y = y + y_s[g][h * P:(h + 1) * P]
        y_ref[g] = y


def _wkv(r, lw, k, v, av, bv, s0_bd, nseq, tlen, groups):
    g, rows, _ = r.shape
    n_chunks = rows // WKV_ROWS
    blk = pl.BlockSpec((groups, WKV_ROWS, RW_DIM), lambda i, c: (i, c, 0))
    sblk = pl.BlockSpec((groups * nseq, RW_DIM, RW_DIM), lambda i, c: (i, 0, 0))
    return pl.pallas_call(
        functools.partial(_wkv_kernel, nseq=nseq, tlen=tlen),
        grid=(g // groups, n_chunks),
        in_specs=[blk] * 6 + [sblk],
        out_specs=[blk, sblk],
        out_shape=[jax.ShapeDtypeStruct(r.shape, F32), jax.ShapeDtypeStruct(s0_bd.shape, F32)],
        compiler_params=_cparams("parallel", "arbitrary"),
        name="wkv",
    )(r, lw, k, v, av, bv, s0_bd)


def _state_to_bd(s):
    b, h, n, _ = s.shape
    eye = jnp.eye(h, dtype=s.dtype)
    return jnp.einsum("bhvk,hg->bhvgk", s, eye).reshape(b, h * n, h * n)


def _state_from_bd(s_bd, h):
    b, hn, _ = s_bd.shape
    n = hn // h
    s5 = s_bd.reshape(b, h, n, h, n)
    return jnp.stack([s5[:, i, :, i, :] for i in range(h)], axis=1)


ATT_NEG = -1e30


def _attn_prompt_kernel(bias_ref, tq_ref, tj_ref, qt_ref, k_ref, vt_ref, o_ref,
                        z_ref, sp_ref, lb_ref, w_ref, zinit_ref, acc_ref, carry_ref):
    grp = pl.program_id(1)
    nk, gw, blk = qt_ref.shape[1:]
    n_tiles = nk * (nk + 1) // 2
    row_head = lax.broadcasted_iota(jnp.int32, (gw, blk), 0) // HEAD_DIM
    krow = lax.broadcasted_iota(jnp.int32, (blk + 8, blk), 0)
    kcol = lax.broadcasted_iota(jnp.int32, (blk + 8, blk), 1)
    neg_sums = jnp.where((kcol > krow) | (krow >= blk), -1.0, 0.0).astype(BF16)
    hidden = (lax.broadcasted_iota(jnp.int32, (blk, blk), 0) >= lax.broadcasted_iota(jnp.int32, (blk, blk), 1))
    for h in range(ATT_HEADS):
        bias2 = bias_ref[grp * ATT_HEADS + h] * LOG2E
        zinit_ref[0, h] = jnp.full((blk, blk), bias2, F32)
        zinit_ref[1, h] = jnp.where(hidden, ATT_NEG, bias2)
    for ref in (z_ref, sp_ref, lb_ref, w_ref, acc_ref, carry_ref):
        ref[...] = jnp.zeros_like(ref)

    def step(i, slot_new):
        slot_old = 1 - slot_new

        f4 = jnp.clip(i - 3, 0, n_tiles - 1)
        q4 = tq_ref[f4]
        j4 = tj_ref[f4]
        keep = jnp.where(j4 == q4, 0.0, 1.0)
        for h in range(ATT_HEADS):
            rows = slice(h * HEAD_DIM, (h + 1) * HEAD_DIM)
            pv = jnp.dot(vt_ref[0, j4, rows, :], w_ref[slot_old, h], preferred_element_type=F32)
            acc_ref[rows, :] = acc_ref[rows, :] * keep + pv
        o_ref[0, q4] = acc_ref[...]

        f3 = jnp.clip(i - 2, 0, n_tiles - 1)
        valid3 = (i >= 2) & (i <= n_tiles + 1)
        first3 = tj_ref[f3] == tq_ref[f3]
        null3 = jnp.where(valid3, 0.0, ATT_NEG)
        count3 = jnp.where(valid3, 1.0, 0.0)
        for h in range(ATT_HEADS):
            sums = jnp.dot(neg_sums, sp_ref[slot_old, h], preferred_element_type=F32)
            carry = jnp.where(first3, 0.0, carry_ref[h])
            w_ref[slot_new, h] = jnp.exp2(lb_ref[slot_old, h] + sums[0:blk] + (carry + null3)).astype(BF16)
            carry_ref[h] = carry + sums[blk:blk + 1] * count3

        for h in range(ATT_HEADS):
            z = z_ref[slot_old, h]
            sp = jnp.maximum(z, 0.0) + jnp.log(1.0 + jnp.exp2(-jnp.abs(z))) * LOG2E
            sp_ref[slot_new, h] = sp.astype(BF16)
            lb_ref[slot_new, h] = z - sp

        f1 = jnp.minimum(i, n_tiles - 1)
        q1 = tq_ref[f1]
        j1 = tj_ref[f1]
        diag1 = jnp.where(j1 == q1, 1, 0)
        k_blk = k_ref[0, pl.ds(pl.multiple_of(j1 * blk, blk), blk), :]
        q_grp = qt_ref[0, q1]
        for h in range(ATT_HEADS):
            q_h = jnp.where(row_head == h, q_grp, jnp.zeros_like(q_grp))
            z_ref[slot_new, h] = jnp.dot(k_blk, q_h, preferred_element_type=F32) + zinit_ref[diag1, h]

    def body(pair, _):
        step(2 * pair, 0)
        step(2 * pair + 1, 1)
        return 0

    lax.fori_loop(0, (n_tiles + 3 + 1) // 2, body, 0)


def _attn_prompt(q_t, k_bf16, v_t, sb_bias, b, t):
    _, nk, _, blk = q_t.shape
    gw = ATT_HEADS * HEAD_DIM
    k3 = k_bf16.reshape(b, t, SB_DIM)
    tiles = [(qi, j) for qi in range(nk) for j in range(qi, -1, -1)]
    tile_q = jnp.array([qi for qi, _ in tiles], jnp.int32)
    tile_j = jnp.array([j for _, j in tiles], jnp.int32)
    blocks = pl.BlockSpec((1, nk, gw, blk), lambda i, g, *_: (i, 0, g, 0))
    grid_spec = pltpu.PrefetchScalarGridSpec(
        num_scalar_prefetch=3,
        grid=(b, SB_HEADS // ATT_HEADS),
        in_specs=[blocks, pl.BlockSpec((1, t, gw), lambda i, g, *_: (i, 0, g)), blocks],
        out_specs=blocks,
        scratch_shapes=[pltpu.VMEM((2, ATT_HEADS, blk, blk), F32), pltpu.VMEM((2, ATT_HEADS, blk, blk), BF16),
                        pltpu.VMEM((2, ATT_HEADS, blk, blk), F32), pltpu.VMEM((2, ATT_HEADS, blk, blk), BF16),
                        pltpu.VMEM((2, ATT_HEADS, blk, blk), F32),
                        pltpu.VMEM((gw, blk), F32), pltpu.VMEM((ATT_HEADS, 1, blk), F32)],
    )
    return pl.pallas_call(
        _attn_prompt_kernel,
        grid_spec=grid_spec,
        out_shape=jax.ShapeDtypeStruct((b, nk, SB_DIM, blk), F32),
        compiler_params=_cparams("parallel", "parallel"),
        name="attn_prompt",
    )(sb_bias, tile_q, tile_j, q_t, k3, v_t)


def _attn_sample_kernel(pt_ref, q_ref, bias_ref, kn_ref, vn_ref, *refs, tlen, n_pages):
    k_refs = refs[:n_pages]
    v_refs = refs[n_pages:2 * n_pages]
    o_ref = refs[2 * n_pages]
    rows = SB_HEADS * tlen
    rhead = lax.broadcasted_iota(jnp.int32, (rows, SB_DIM), 0) // tlen
    lhead = lax.broadcasted_iota(jnp.int32, (rows, SB_DIM), 1) // HEAD_DIM
    head_mask = rhead == lhead
    q_s = jnp.where(head_mask, jnp.concatenate([q_ref[0]] * SB_HEADS, axis=0), 0.0).astype(BF16)
    jrow = lax.broadcasted_iota(jnp.int32, (PAGE_SIZE, 2 * PAGE_SIZE), 0)
    jcol = lax.broadcasted_iota(jnp.int32, (PAGE_SIZE, 2 * PAGE_SIZE), 1)
    tri_ones = jnp.where((jrow > jcol) | (jcol >= PAGE_SIZE), 1.0, 0.0).astype(BF16)
    bias = bias_ref[...]

    def scores(z, mask):
        z = z + bias
        sp = jnp.maximum(z, 0.0) + jnp.log(1.0 + jnp.exp(-jnp.abs(z)))
        lk = -sp
        if mask is not None:
            lk = jnp.where(mask, lk, 0.0)
        lk_hi, lk_lo = _split2(lk)
        sums = (jnp.dot(lk_hi, tri_ones, preferred_element_type=F32)
                + jnp.dot(lk_lo, tri_ones, preferred_element_type=F32))
        return (z - sp) + sums[:, 0:PAGE_SIZE], sums[:, PAGE_SIZE:]

    pad = jnp.zeros((PAGE_SIZE - tlen, SB_DIM), F32)
    k_new = jnp.concatenate([kn_ref[0], pad], axis=0)
    v_new = jnp.concatenate([vn_ref[0], pad], axis=0)
    new_mask = (lax.broadcasted_iota(jnp.int32, (rows, PAGE_SIZE), 1)
                < lax.broadcasted_iota(jnp.int32, (rows, PAGE_SIZE), 0) % tlen)
    order = list(reversed(range(n_pages)))
    zs = [_dg(q_s, k_new, 1, 1)] + [_dot(q_s, k_refs[j][0, 0]) for j in order]
    parts = [scores(zs[0], new_mask)] + [scores(z, None) for z in zs[1:]]
    carry = jnp.zeros((rows, PAGE_SIZE), F32)
    ws = []
    for n, (logw, total) in enumerate(parts):
        w = jnp.exp(logw + carry)
        ws.append(jnp.where(new_mask, w, 0.0) if n == 0 else w)
        carry = carry + total
    acc = _dot(ws[0], v_new)
    for w, j in zip(ws[1:], order):
        acc = acc + _dg(w, v_refs[j][0, 0], 1, 1)
    acc = jnp.where(head_mask, acc, 0.0)
    out = acc[0:tlen]
    for h in range(1, SB_HEADS):
        out = out + acc[h * tlen:(h + 1) * tlen]
    o_ref[0] = out


def _attn_sample(q, k_new, v_new, cache_kt, cache_vt, layer, page_table, sb_bias, b, tlen):
    n_pages = page_table.shape[1]
    rows = SB_HEADS * tlen
    bias_rows = jnp.broadcast_to(jnp.repeat(sb_bias, tlen)[:, None], (rows, PAGE_SIZE)).astype(F32)
    seq = lambda a: a.reshape(b, tlen, SB_DIM)
    tok = pl.BlockSpec((1, tlen, SB_DIM), lambda i, pt: (i, 0, 0))
    page = lambda j: pl.BlockSpec((1, 1, SB_DIM, PAGE_SIZE), lambda i, pt, j=j: (layer, pt[i, j], 0, 0))
    pages = [page(j) for j in range(n_pages)]
    grid_spec = pltpu.PrefetchScalarGridSpec(
        num_scalar_prefetch=1,
        grid=(b,),
        in_specs=[tok, pl.BlockSpec((rows, PAGE_SIZE), lambda i, pt: (0, 0)), tok, tok] + pages + pages,
        out_specs=tok,
    )
    out = pl.pallas_call(
        functools.partial(_attn_sample_kernel, tlen=tlen, n_pages=n_pages),
        grid_spec=grid_spec,
        out_shape=jax.ShapeDtypeStruct((b, tlen, SB_DIM), F32),
        compiler_params=_cparams("parallel"),
        name="attn_sample",
    )(page_table, seq(q), bias_rows, seq(k_new), seq(v_new), *([cache_kt] * n_pages), *([cache_vt] * n_pages))
    return out.reshape(b * tlen, SB_DIM)


def _tail_kernel(x_ref, y_ref, bonus_ref, g_ref, cb_ref, cc_ref, cx_ref, *refs, tiles_per_seq):
    n_before = 4 if tiles_per_seq is None else 3
    before_refs = refs[:n_before]
    (osb_ref, p_ref, gnw_ref, gnb_ref, convw_ref, ones_ref, wout_ref, gmlp_ref, wup_ref, wdown_ref,
     gple_ref, wgate_ref, wproj_ref, o_ref, x1_ref, h2_ref, acc_ref) = refs[n_before:]
    f = pl.program_id(1)
    nf = pl.num_programs(1)

    @pl.when(f == 0)
    def _():
        y = y_ref[...]
        mu = _dot_exact_rhs(y, ones_ref[...])
        yc = y - mu
        var = _dot_exact_rhs(yc * yc, ones_ref[...])
        yn = yc * lax.rsqrt(var + GN_EPS) * gnw_ref[...] + gnb_ref[...]
        o_rw = (yn + bonus_ref[...]) * g_ref[...]
        cw = convw_ref[...]
        u = cc_ref[...] * cx_ref[...]
        if tiles_per_seq is None:
            cc1_ref, cx1_ref, cc2_ref, cx2_ref = before_refs
            u1 = cc1_ref[...] * cx1_ref[...]
            u2 = cc2_ref[...] * cx2_ref[...]
        else:
            cch_ref, cxh_ref, state_ref = before_refs
            u_halo = cch_ref[...] * cxh_ref[...]
            is_first = pl.program_id(0) % tiles_per_seq == 0
            u1 = _rows_before(u, u_halo, state_ref[0], is_first, 1)
            u2 = _rows_before(u, u_halo, state_ref[0], is_first, 2)
        o_cv = cb_ref[...] * (cw[0:1] * u2 + cw[1:2] * u1 + cw[2:3] * u)
        mix = _dot(o_rw, wout_ref[0:RW_DIM, :])
        mix = mix + _dot(o_cv, wout_ref[RW_DIM:RW_DIM + CONV_DIM, :])
        w_sb = wout_ref[RW_DIM + CONV_DIM:, :]
        if len(osb_ref.shape) == 2:
            mix = mix + _dot(osb_ref[...], w_sb)
        else:
            mix = mix + jnp.concatenate(
                [_dot(osb_ref[0, s].T, w_sb) for s in range(osb_ref.shape[1])], axis=0)
        x1 = x_ref[...] + mix
        x1_ref[...] = x1
        h2_ref[...] = _rmsnorm_rows(x1, gmlp_ref[...]).astype(BF16)
        acc_ref[...] = jnp.zeros_like(acc_ref)

    up = jnp.dot(h2_ref[...], wup_ref[...], preferred_element_type=F32)
    act = jnp.square(jnp.maximum(up, 0.0))
    acc_ref[...] += _dot(act, wdown_ref[...])

    @pl.when(f == nf - 1)
    def _():
        x2 = x1_ref[...] + acc_ref[...]
        gate = _sigmoid(_dot(_rmsnorm_rows(x2, gple_ref[...]), wgate_ref[...]))
        o_ref[...] = x2 + gate * _dot(p_ref[...], wproj_ref[...])


def _tail(x2d, y, bonus, g, p_cv, conv_prev, seq_len, o_sb, p3d, layer, lw, wts, tm, tf):
    n, d = x2d.shape
    b = n // seq_len
    d_ff = wts["w_up"].shape[1]
    ones = _head_block_ones(RW_DIM, 1.0 / HEAD_DIM)
    vec = lambda a: a.reshape(1, -1)
    row = lambda w, col=0: pl.BlockSpec((tm, w), lambda i, f, col=col: (i, col))
    full = lambda a: pl.BlockSpec(a.shape, lambda i, f: (0,) * a.ndim)
    gnw, gnb, gmlp, gple = vec(lw["gn_w"]), vec(lw["gn_b"]), vec(lw["g_mlp"]), vec(lw["g_ple"])
    if o_sb.ndim == 2:
        osb_spec = row(SB_DIM)
    else:
        _, nblk, _, blk = o_sb.shape
        osb_tiles = nblk * blk // tm
        osb_spec = pl.BlockSpec((1, tm // blk, SB_DIM, blk), lambda i, f: (i // osb_tiles, i % osb_tiles, 0, 0))
    if seq_len % tm == 0:
        tiles_per_seq = seq_len // tm
        halo = lambda col: pl.BlockSpec((8, CONV_DIM), lambda i, f: (jnp.maximum(i * (tm // 8) - 1, 0), col))
        before = [p_cv, p_cv, conv_prev]
        before_specs = [halo(1), halo(2),
                        pl.BlockSpec((1, CONV_K - 1, CONV_DIM), lambda i, f: (i // tiles_per_seq, 0, 0))]
    else:
        tiles_per_seq = None
        p_cv3 = p_cv.reshape(b, seq_len, 3 * CONV_DIM)
        state_rows = lambda s: jnp.concatenate([jnp.zeros_like(s), s, jnp.ones_like(s)], axis=-1)
        p_cv1 = _shift_rows(p_cv3, state_rows(conv_prev[:, 1:2]), 1).reshape(n, 3 * CONV_DIM)
        p_cv2 = _shift_rows(p_cv3, state_rows(conv_prev), 2).reshape(n, 3 * CONV_DIM)
        before = [p_cv1, p_cv1, p_cv2, p_cv2]
        before_specs = [row(CONV_DIM, 1), row(CONV_DIM, 2), row(CONV_DIM, 1), row(CONV_DIM, 2)]
    in_specs = [row(d), row(RW_DIM), row(RW_DIM), row(RW_DIM),
                row(CONV_DIM, 0), row(CONV_DIM, 1), row(CONV_DIM, 2)] + before_specs + [
                osb_spec, pl.BlockSpec((None, tm, p3d.shape[2]), lambda i, f: (layer, i, 0)),
                full(gnw), full(gnb), full(lw["conv_w"]), full(ones), full(wts["w_out"]), full(gmlp),
                pl.BlockSpec((d, tf), lambda i, f: (0, f)), pl.BlockSpec((tf, d), lambda i, f: (f, 0)),
                full(gple), full(wts["w_ple_gate"]), full(wts["w_ple_proj"])]
    return pl.pallas_call(
        functools.partial(_tail_kernel, tiles_per_seq=tiles_per_seq),
        grid=(n // tm, d_ff // tf),
        in_specs=in_specs,
        out_specs=pl.BlockSpec((tm, d), lambda i, f: (i, 0)),
        out_shape=jax.ShapeDtypeStruct((n, d), F32),
        scratch_shapes=[pltpu.VMEM((tm, d), F32), pltpu.VMEM((tm, d), BF16), pltpu.VMEM((tm, d), F32)],
        compiler_params=_cparams("parallel", "arbitrary"),
        name="tail",
    )(x2d, y, bonus, g, p_cv, p_cv, p_cv, *before, o_sb, p3d,
      gnw, gnb, lw["conv_w"], ones, wts["w_out"], gmlp, wts["w_up"], wts["w_down"],
      gple, wts["w_ple_gate"], wts["w_ple_proj"])


def _conv_state_kernel(cv_ref, o_ref):
    o_ref[...] = cv_ref[:, :, CONV_DIM:2 * CONV_DIM] * cv_ref[:, :, 2 * CONV_DIM:3 * CONV_DIM]


def _conv_state(p_cv3):
    b, t, c = p_cv3.shape
    return pl.pallas_call(
        _conv_state_kernel,
        grid=(1,),
        in_specs=[pl.BlockSpec((b, 8, c), lambda i: (0, t // 8 - 1, 0))],
        out_specs=pl.BlockSpec((b, 8, CONV_DIM), lambda i: (0, 0, 0)),
        out_shape=jax.ShapeDtypeStruct((b, 8, CONV_DIM), F32),
        name="conv_state",
    )(p_cv3)


def _shift_rows(x3, first_rows, n):
    return jnp.concatenate([first_rows, x3[:, :-n]], axis=1)


def _layer(x2d, p3d, layer, b, t, shift_prev, wkv_prev_bd, conv_prev, attn_fn, att_blk, lw, wts, tm, tf,
           wkv_nseq, wkv_tlen, wkv_groups):
    n, d = x2d.shape
    p_rw, p_cv, k_new, v_new, r, lwd, k2, v, av, bv, g, bonus, *attn_ops = _in_proj(
        x2d, shift_prev, lw, wts["w_in"], tm, t, att_blk)
    p_rw3 = p_rw.reshape(b, t, p_rw.shape[1])
    grp = lambda a: a.reshape(-1, (n // (b // wkv_nseq)), RW_DIM) if wkv_nseq > 1 else a.reshape(b, t, RW_DIM)
    y, s_bd = _wkv(grp(r), grp(lwd), grp(k2), grp(v), grp(av), grp(bv), wkv_prev_bd, wkv_nseq, wkv_tlen,
                   wkv_groups)
    y = y.reshape(n, RW_DIM)

    o_sb = attn_fn(k_new, v_new, *attn_ops)

    x_out = _tail(x2d, y, bonus, g, p_cv, conv_prev, t, o_sb, p3d, layer, lw, wts, tm, tf)
    conv_new = _conv_state(p_cv.reshape(b, t, 3 * CONV_DIM))[:, 8 - (CONV_K - 1):]
    return x_out, k_new, v_new, s_bd, p_rw3[:, -1], conv_new


def _run_trunk(x, p, shift, wkv, conv, attn_builder, att_blk, params, bf16_w, tm, tf,
               wkv_nseq, wkv_tlen, wkv_groups):
    b, t, d = x.shape
    depth = p.shape[0]
    x2d = x.reshape(b * t, d)
    ks, vs, wkvs, shifts, convs = [], [], [], [], []
    for l in range(depth):
        lw = {name: arr[l] for name, arr in params.items()}
        wts = {name: arr[l] for name, arr in bf16_w.items()}
        x2d, k_new, v_new, s_bd, shift_new, conv_new = _layer(
            x2d, p.reshape(depth, b * t, -1), l, b, t, shift[l], _state_to_bd(wkv[l]), conv[l],
            attn_builder(l, lw), att_blk, lw, wts, tm, tf, wkv_nseq, wkv_tlen, wkv_groups)
        ks.append(k_new.reshape(b, t, SB_HEADS, HEAD_DIM))
        vs.append(v_new.reshape(b, t, SB_HEADS, HEAD_DIM))
        wkvs.append(_state_from_bd(s_bd, RW_HEADS))
        shifts.append(shift_new)
        convs.append(conv_new)
    return (x2d.reshape(b, t, d), jnp.stack(ks), jnp.stack(vs), jnp.stack(wkvs), jnp.stack(shifts),
            jnp.stack(convs))


def kernel(x_prompt, x_sample, cache_k, cache_v, state_wkv, state_shift, state_conv, page_table,
           p_prompt, p_sample, g_mix, w_in, mu_shift, w0, w2, a0, a2, g2, k_k, k_a, r_k,
           gn_w, gn_b, conv_w, q_gain, k_gain, sb_bias, w_out, g_mlp, w_up, w_down, g_ple,
           w_ple_gate, w_ple_proj):
    params = dict(g_mix=g_mix, mu_shift=mu_shift, w0=w0, w2=w2, a0=a0, a2=a2, g2=g2,
                  k_k=k_k, k_a=k_a, r_k=r_k, gn_w=gn_w, gn_b=gn_b, conv_w=conv_w,
                  q_gain=q_gain, k_gain=k_gain, sb_bias=sb_bias, g_mlp=g_mlp, g_ple=g_ple)
    bf16_w = dict(w_in=w_in.astype(BF16), w_out=w_out.astype(BF16), w_up=w_up.astype(BF16),
                  w_down=w_down.astype(BF16), w_ple_gate=w_ple_gate.astype(BF16),
                  w_ple_proj=w_ple_proj.astype(BF16))
    depth = w_in.shape[0]
    b, t, d = x_prompt.shape
    db, dt, _ = x_sample.shape
    rw_proj = state_shift.shape[-1]
    d_ff = w_up.shape[-1]
    tf = min(2048, d_ff)

    n_phys = cache_k.shape[1]
    page_view = lambda c: c.transpose(0, 1, 3, 4, 2).reshape(depth, n_phys, SB_DIM, PAGE_SIZE)
    cache_kt, cache_vt = page_view(cache_k), page_view(cache_v)

    def prompt_attn(l, lw):
        return lambda k, v, q_t, k_bf16, v_t: _attn_prompt(q_t, k_bf16, v_t, lw["sb_bias"], b, t)

    def sample_attn(l, lw):
        return lambda k, v, q: _attn_sample(q, k, v, cache_kt, cache_vt, l, page_table, lw["sb_bias"], db, dt)

    dtp = x_prompt.dtype
    shift0 = jnp.zeros((depth, b, rw_proj), dtp)
    wkv0 = jnp.zeros((depth, b, RW_HEADS, HEAD_DIM, HEAD_DIM), dtp)
    conv0 = jnp.zeros((depth, b, CONV_K - 1, CONV_DIM), dtp)
    (y_prompt, k_prompt, v_prompt, wkv_prompt, shift_prompt, conv_prompt) = _run_trunk(
        x_prompt, p_prompt, shift0, wkv0, conv0, prompt_attn, min(ATT_BLOCK, t), params, bf16_w,
        min(512, t), tf, 1, WKV_ROWS, b)
    seqs_per_group = WKV_ROWS // dt
    (y_sample, k_sample, v_sample, wkv_sample, shift_sample, conv_sample) = _run_trunk(
        x_sample, p_sample, state_shift, state_wkv, state_conv, sample_attn, None, params, bf16_w,
        min(512, db * dt), tf, seqs_per_group, dt, min(2, db // seqs_per_group))
    return (y_prompt, y_sample, k_prompt, v_prompt, wkv_prompt, shift_prompt, conv_prompt,
            k_sample, v_sample, wkv_sample, shift_sample, conv_sample)
```

```python
import functools

import jax
import jax.numpy as jnp
from jax import lax
from jax.experimental import pallas as pl
from jax.experimental.pallas import tpu as pltpu

F32 = jnp.float32
BF16 = jnp.bfloat16

HEAD_DIM = 64
RW_HEADS = 4
RW_DIM = RW_HEADS * HEAD_DIM
LORA_WA = 128
CONV_DIM = 256
CONV_K = 3
SB_HEADS = 8
SB_DIM = SB_HEADS * HEAD_DIM
PAGE_SIZE = 128
RMS_EPS = 1e-6
GN_EPS = 64e-5
KK_EPS = 1e-12
LOG2E = 1.4426950408889634

WKV_ROWS = 64
ATT_BLOCK = 256
ATT_HEADS = 4
VMEM_LIMIT_BYTES = 56 * 1024 * 1024


def _cparams(*sem):
    return pltpu.CompilerParams(dimension_semantics=sem, vmem_limit_bytes=VMEM_LIMIT_BYTES)


def _dot(a, b):
    return jnp.dot(a.astype(BF16), b.astype(BF16), preferred_element_type=F32)


def _dg(a, b, ca, cb):
    return lax.dot_general(a.astype(BF16), b.astype(BF16), (((ca,), (cb,)), ((), ())),
                           preferred_element_type=F32)


def _split2(a):
    hi = a.astype(BF16)
    lo = (a - hi.astype(F32)).astype(BF16)
    return hi, lo


def _dot_exact_rhs(a, b_bf16):
    h1 = a.astype(BF16)
    r1 = a - h1.astype(F32)
    h2 = r1.astype(BF16)
    h3 = (r1 - h2.astype(F32)).astype(BF16)
    out = jnp.dot(h1, b_bf16, preferred_element_type=F32)
    out = out + jnp.dot(h2, b_bf16, preferred_element_type=F32)
    out = out + jnp.dot(h3, b_bf16, preferred_element_type=F32)
    return out


def _dot_exact_lhs(a_bf16, b):
    h1 = b.astype(BF16)
    r1 = b - h1.astype(F32)
    h2 = r1.astype(BF16)
    h3 = (r1 - h2.astype(F32)).astype(BF16)
    out = jnp.dot(a_bf16, h1, preferred_element_type=F32)
    out = out + jnp.dot(a_bf16, h2, preferred_element_type=F32)
    out = out + jnp.dot(a_bf16, h3, preferred_element_type=F32)
    return out


def _softplus(x):
    return jnp.maximum(x, 0.0) + jnp.log1p(jnp.exp(-jnp.abs(x)))


def _sigmoid(x):
    return 1.0 / (1.0 + jnp.exp(-x))


def _rmsnorm_rows(x, g):
    ms = jnp.mean(x * x, axis=-1, keepdims=True)
    return x * lax.rsqrt(ms + RMS_EPS) * g


def _head_block_ones(width, scale):
    i = jnp.arange(width) // HEAD_DIM
    return jnp.where(i[:, None] == i[None, :], scale, 0.0).astype(BF16)


def _rows_before(cur, halo, first_rows, is_first, n):
    row = lax.broadcasted_iota(jnp.int32, cur.shape, 0)
    out = pltpu.roll(cur, n, 0)
    n_halo, n_first = halo.shape[0], first_rows.shape[0]
    for k in range(n):
        before = jnp.where(is_first, first_rows[n_first - n + k:n_first - n + k + 1],
                           halo[n_halo - n + k:n_halo - n + k + 1])
        out = jnp.where(row == k, before, out)
    return out


def _rwkv_heads(p, prev, mu_ref, w0_ref, a0_ref, kk_ref, ka_ref, rk_ref, wwa_ref, g2_ref, ones_ref,
                r_ref, lw_ref, k_ref, v_ref, av_ref, bv_ref, g_ref, bonus_ref):
    xs = p + (prev - p) * mu_ref[...]
    r = xs[:, 0:RW_DIM]
    k = xs[:, RW_DIM:2 * RW_DIM]
    v = xs[:, 2 * RW_DIM:3 * RW_DIM]
    d = xs[:, 3 * RW_DIM:3 * RW_DIM + LORA_WA]
    dg = xs[:, 3 * RW_DIM + LORA_WA:]
    lane = lax.broadcasted_iota(jnp.int32, d.shape, 1)
    dwa = jnp.where(lane < LORA_WA // 2, jnp.tanh(d), d)
    wa = _dot(dwa, wwa_ref[...])
    w_log = -_softplus(-(w0_ref[...] + wa[:, 0:RW_DIM])) - 0.5
    lw_ref[...] = -jnp.exp(w_log)
    a = _sigmoid(a0_ref[...] + wa[:, RW_DIM:])
    g_ref[...] = _dot(_sigmoid(dg), g2_ref[...])
    kk = k * kk_ref[...]
    ss = _dot_exact_rhs(kk * kk, ones_ref[...])
    kk = kk * lax.rsqrt(ss + KK_EPS)
    k2 = k * (1.0 + (a - 1.0) * ka_ref[...])
    r_ref[...] = r
    k_ref[...] = k2
    v_ref[...] = v
    av_ref[...] = -kk
    bv_ref[...] = kk * a
    bonus_ref[...] = _dot_exact_rhs(r * k2 * rk_ref[...], ones_ref[...]) * v


N_RWKV_PARAMS = 9
N_RWKV_OUTS = 8


def _in_proj_kernel(x_ref, g_ref, w_ref, ebd_ref, qg_ref, kg_ref, state_ref, *refs,
                    att_blk, seq_len, tiles_per_seq):
    rwkv_params = refs[:N_RWKV_PARAMS]
    rw_ref, cv_ref, k_ref, v_ref = refs[N_RWKV_PARAMS:N_RWKV_PARAMS + 4]
    rwkv_outs = refs[N_RWKV_PARAMS + 4:N_RWKV_PARAMS + 4 + N_RWKV_OUTS]
    attn_refs = refs[N_RWKV_PARAMS + 4 + N_RWKV_OUTS:]
    h = _rmsnorm_rows(x_ref[...], g_ref[...]).astype(BF16)
    c0 = rw_ref.shape[1]
    c1 = c0 + cv_ref.shape[1]
    p = jnp.dot(h, w_ref[:, 0:c0], preferred_element_type=F32)
    rw_ref[...] = p
    rows = p.shape[0]
    if tiles_per_seq is None:
        nseq = rows // seq_len
        r_i = lax.broadcasted_iota(jnp.int32, (rows, nseq), 0)
        s_i = lax.broadcasted_iota(jnp.int32, (rows, nseq), 1)
        starts = jnp.where(r_i == s_i * seq_len, 1.0, 0.0).astype(BF16)
        t_local = lax.broadcasted_iota(jnp.int32, p.shape, 0) % seq_len
        prev = jnp.where(t_local == 0, _dot_exact_lhs(starts, state_ref[...]), pltpu.roll(p, 1, 0))
    else:
        attn_refs, last_ref = attn_refs[:-1], attn_refs[-1]

        @pl.when(pl.program_id(0) == 0)
        def _():
            last_ref[...] = jnp.zeros_like(last_ref)

        prev = _rows_before(p, last_ref[...], state_ref[0], pl.program_id(0) % tiles_per_seq == 0, 1)
        last_ref[...] = p[rows - last_ref.shape[0]:rows]
    _rwkv_heads(p, prev, *rwkv_params, *rwkv_outs)
    cv_ref[...] = jnp.dot(h, w_ref[:, c0:c1], preferred_element_type=F32)

    def head_norm(t, gain):
        ms = jnp.dot((t * t).astype(BF16), ebd_ref[...], preferred_element_type=F32)
        return t * lax.rsqrt(ms + RMS_EPS) * gain

    q = jnp.dot(h, w_ref[:, c1:c1 + SB_DIM], preferred_element_type=F32)
    q = head_norm(q, qg_ref[...]) * (HEAD_DIM ** -0.5)
    k = jnp.dot(h, w_ref[:, c1 + SB_DIM:c1 + 2 * SB_DIM], preferred_element_type=F32)
    k = head_norm(k, kg_ref[...])
    k_ref[...] = k
    v = jnp.dot(h, w_ref[:, c1 + 2 * SB_DIM:c1 + 3 * SB_DIM], preferred_element_type=F32)
    v_ref[...] = v
    if att_blk is None:
        (q_ref,) = attn_refs
        q_ref[...] = q
    else:
        qt_ref, kb_ref, vt_ref = attn_refs
        kb_ref[...] = k.astype(BF16)
        q_t = (q * LOG2E).T
        v_t = v.T
        for s in range(qt_ref.shape[1]):
            qt_ref[0, s] = q_t[:, s * att_blk:(s + 1) * att_blk].astype(BF16)
            vt_ref[0, s] = v_t[:, s * att_blk:(s + 1) * att_blk].astype(BF16)


def _in_proj(x2d, shift_prev, lw, w_in_bf16, tm, seq_len, att_blk):
    n, d = x2d.shape
    cols = w_in_bf16.shape[1]
    rw_cols = cols - 3 * CONV_DIM - 3 * SB_DIM
    ebd = _head_block_ones(SB_DIM, 1.0 / HEAD_DIM)
    qg = jnp.tile(lw["q_gain"], SB_HEADS)[None, :]
    kg = jnp.tile(lw["k_gain"], SB_HEADS)[None, :]
    row = lambda w: pl.BlockSpec((tm, w), lambda i: (i, 0))
    full = lambda a: pl.BlockSpec(a.shape, lambda i: (0,) * a.ndim)
    g2d = lw["g_mix"][None, :]
    w_lora = lw["w2"].shape[0]
    a_lora = lw["a2"].shape[0]
    wwa = jnp.zeros((w_lora + a_lora, 2 * RW_DIM), F32)
    wwa = wwa.at[:w_lora, :RW_DIM].set(lw["w2"]).at[w_lora:, RW_DIM:].set(lw["a2"]).astype(BF16)
    vec = lambda a: a.reshape(1, -1)
    rwkv_params = [vec(lw["mu_shift"]), vec(lw["w0"]), vec(lw["a0"]), vec(lw["k_k"]), vec(lw["k_a"]),
                   vec(lw["r_k"]), wwa, lw["g2"].astype(BF16), _head_block_ones(RW_DIM, 1.0)]
    assert len(rwkv_params) == N_RWKV_PARAMS
    scratch = []
    if seq_len % tm == 0:
        tiles_per_seq = seq_len // tm
        state, state_spec = shift_prev[:, None, :], pl.BlockSpec((1, 1, rw_cols), lambda i: (i // tiles_per_seq, 0, 0))
        scratch = [pltpu.VMEM((8, rw_cols), F32)]
    else:
        tiles_per_seq = None
        state, state_spec = shift_prev, pl.BlockSpec((tm // seq_len, rw_cols), lambda i: (i, 0))
    out_specs = [row(rw_cols), row(3 * CONV_DIM), row(SB_DIM), row(SB_DIM)] + [row(RW_DIM)] * N_RWKV_OUTS
    out_shape = [jax.ShapeDtypeStruct((n, rw_cols), F32), jax.ShapeDtypeStruct((n, 3 * CONV_DIM), F32),
                 jax.ShapeDtypeStruct((n, SB_DIM), F32), jax.ShapeDtypeStruct((n, SB_DIM), F32)]
    out_shape += [jax.ShapeDtypeStruct((n, RW_DIM), F32)] * N_RWKV_OUTS
    if att_blk is None:
        out_specs.append(row(SB_DIM))
        out_shape.append(jax.ShapeDtypeStruct((n, SB_DIM), F32))
    else:
        att_tiles = seq_len // tm
        blks = tm // att_blk
        t_spec = pl.BlockSpec((1, blks, SB_DIM, att_blk), lambda i: (i // att_tiles, i % att_tiles, 0, 0))
        t_shape = jax.ShapeDtypeStruct((n // seq_len, seq_len // att_blk, SB_DIM, att_blk), BF16)
        out_specs += [t_spec, row(SB_DIM), t_spec]
        out_shape += [t_shape, jax.ShapeDtypeStruct((n, SB_DIM), BF16), t_shape]
    return pl.pallas_call(
        functools.partial(_in_proj_kernel, att_blk=att_blk, seq_len=seq_len, tiles_per_seq=tiles_per_seq),
        grid=(n // tm,),
        in_specs=[row(d), full(g2d), full(w_in_bf16), full(ebd), full(qg), full(kg), state_spec]
                 + [full(a) for a in rwkv_params],
        out_specs=out_specs,
        out_shape=out_shape,
        scratch_shapes=scratch,
        compiler_params=_cparams("arbitrary"),
        name="in_proj",
    )(x2d, g2d, w_in_bf16, ebd, qg, kg, state, *rwkv_params)


def _wkv_kernel(r_ref, lw_ref, k_ref, v_ref, a_ref, b_ref, s0_ref, y_ref, sout_ref, s_ref, *, nseq, tlen):
    P = WKV_ROWS
    HP = RW_HEADS * P
    c = pl.program_id(1)
    spread = jnp.where(lax.broadcasted_iota(jnp.int32, (HEAD_DIM, RW_DIM), 0)
                       == lax.broadcasted_iota(jnp.int32, (HEAD_DIM, RW_DIM), 1) % HEAD_DIM, 1.0, 0.0).astype(BF16)
    fold = jnp.where(lax.broadcasted_iota(jnp.int32, (RW_DIM, HEAD_DIM), 0) % HEAD_DIM
                     == lax.broadcasted_iota(jnp.int32, (RW_DIM, HEAD_DIM), 1), 1.0, 0.0).astype(BF16)
    own_head = (lax.broadcasted_iota(jnp.int32, (RW_DIM, RW_DIM), 0) // HEAD_DIM
                == lax.broadcasted_iota(jnp.int32, (RW_DIM, RW_DIM), 1) // HEAD_DIM)

    @pl.when(c == 0)
    def _():
        for j in range(s_ref.shape[0]):
            s_ref[j] = jnp.where(own_head, _dot_exact_rhs(s0_ref[j], spread), 0.0)

    prow = lax.broadcasted_iota(jnp.int32, (P, P), 0)
    pcol = lax.broadcasted_iota(jnp.int32, (P, P), 1)
    same_seq = (prow // tlen) == (pcol // tlen)
    tri_incl = jnp.where(same_seq & (pcol <= prow), 1.0, 0.0).astype(BF16)
    last_sel = jnp.where(same_seq & (pcol == (prow // tlen) * tlen + (tlen - 1)), 1.0, 0.0).astype(BF16)
    srow = lax.broadcasted_iota(jnp.int32, (HP, HP), 0)
    scol = lax.broadcasted_iota(jnp.int32, (HP, HP), 1)
    same_blk = ((srow // tlen) == (scol // tlen))
    m_strict = same_blk & (scol < srow)
    m_incl = same_blk & (scol <= srow)
    eye = jnp.where(srow == scol, 1.0, 0.0)
    hrow = lax.broadcasted_iota(jnp.int32, (HP, RW_DIM), 0)
    head_mask = (hrow // P) == (lax.broadcasted_iota(jnp.int32, (HP, RW_DIM), 1) // HEAD_DIM)
    seq_of_row = (hrow % P) // tlen

    def stack(x):
        return jnp.where(head_mask, jnp.concatenate([x] * RW_HEADS, axis=0), 0.0)

    groups = range(r_ref.shape[0])
    lw = [lw_ref[g] for g in groups]
    cw = [_dot_exact_lhs(tri_incl, x) for x in lw]
    cw_end = [_dot_exact_lhs(last_sel, x) for x in cw]
    w_inv = [jnp.exp(-x) for x in cw]
    w_toend = [jnp.exp(e - x) for e, x in zip(cw_end, cw)]
    a_s = [stack(a_ref[g] * jnp.exp(cw[g] - lw[g])) for g in groups]
    r_s = [stack(r_ref[g] * jnp.exp(cw[g])) for g in groups]
    b_s = [stack(b_ref[g] * w_inv[g]) for g in groups]
    k_s = [stack(k_ref[g] * w_inv[g]) for g in groups]
    v_s = [stack(v_ref[g]) for g in groups]
    bend_s = [stack(b_ref[g] * w_toend[g]) for g in groups]
    kend_s = [stack(k_ref[g] * w_toend[g]) for g in groups]

    n_ab = [jnp.where(m_strict, _dg(a_s[g], b_s[g], 1, 1), 0.0) for g in groups]
    a_ak = [jnp.where(m_strict, _dg(a_s[g], k_s[g], 1, 1), 0.0) for g in groups]
    a_rb = [jnp.where(m_incl, _dg(r_s[g], b_s[g], 1, 1), 0.0) for g in groups]
    a_rk = [jnp.where(m_incl, _dg(r_s[g], k_s[g], 1, 1), 0.0) for g in groups]

    t_inv = [eye + x for x in n_ab]
    pw = n_ab
    span = 2
    while span < tlen:
        pw = [_dot(x, x) for x in pw]
        t_inv = [t + _dot(t, x) for t, x in zip(t_inv, pw)]
        span *= 2

    ap_s = [_dot(t_inv[g], a_s[g]) for g in groups]
    akv = [_dot(a_ak[g], v_s[g]) for g in groups]
    u1_s = [_dot(t_inv[g], akv[g]) for g in groups]
    y_s = [_dot(a_rb[g], u1_s[g]) + _dot(a_rk[g], v_s[g]) for g in groups]
    rp_s = [r_s[g] + _dot(a_rb[g], ap_s[g]) for g in groups]
    wc_row = [jnp.exp(x) for x in cw_end]

    for i in range(nseq):
        if nseq == 1:
            sel = lambda x: x
        else:
            sel = lambda x: jnp.where(seq_of_row == i, x, 0.0)
        s_i = [s_ref[g * nseq + i] for g in groups]
        u_i = [_dg(sel(ap_s[g]), s_i[g], 1, 1) + sel(u1_s[g]) for g in groups]
        y_s = [y_s[g] + _dg(sel(rp_s[g]), s_i[g], 1, 1) for g in groups]
        for g in groups:
            wc_i = wc_row[g][i * tlen:i * tlen + 1, :]
            s_ref[g * nseq + i] = (s_i[g] * wc_i + _dg(u_i[g], sel(bend_s[g]), 0, 0)
                                   + _dg(sel(v_s[g]), sel(kend_s[g]), 0, 0))
    for g in groups:
        y = y_s[g][0:P]
        for h in range(1, RW_HEADS):
            y = y + y_s[g][h * P:(h + 1) * P]
        y_ref[g] = y

    @pl.when(c == pl.num_programs(1) - 1)
    def _():
        for j in range(s_ref.shape[0]):
            sout_ref[j] = _dot_exact_rhs(jnp.where(own_head, s_ref[j], 0.0), fold)


def _wkv(r, lw, k, v, av, bv, s0, nseq, tlen, groups):
    g, rows, _ = r.shape
    n_chunks = rows // WKV_ROWS
    blk = pl.BlockSpec((groups, WKV_ROWS, RW_DIM), lambda i, c: (i, c, 0))
    sblk = pl.BlockSpec((groups * nseq, RW_DIM, HEAD_DIM), lambda i, c: (i, 0, 0))
    return pl.pallas_call(
        functools.partial(_wkv_kernel, nseq=nseq, tlen=tlen),
        grid=(g // groups, n_chunks),
        in_specs=[blk] * 6 + [sblk],
        out_specs=[blk, sblk],
        out_shape=[jax.ShapeDtypeStruct(r.shape, F32), jax.ShapeDtypeStruct(s0.shape, F32)],
        scratch_shapes=[pltpu.VMEM((groups * nseq, RW_DIM, RW_DIM), F32)],
        compiler_params=_cparams("parallel", "arbitrary"),
        name="wkv",
    )(r, lw, k, v, av, bv, s0)


ATT_NEG = -1e30


def _attn_prompt_kernel(bias_ref, tq_ref, tj_ref, qt_ref, k_ref, vt_ref, o_ref,
                        z_ref, sp_ref, lb_ref, w_ref, zinit_ref, acc_ref, carry_ref):
    grp = pl.program_id(1)
    nk, gw, blk = qt_ref.shape[1:]
    n_tiles = nk * (nk + 1) // 2
    row_head = lax.broadcasted_iota(jnp.int32, (gw, blk), 0) // HEAD_DIM
    krow = lax.broadcasted_iota(jnp.int32, (blk + 8, blk), 0)
    kcol = lax.broadcasted_iota(jnp.int32, (blk + 8, blk), 1)
    neg_sums = jnp.where((kcol > krow) | (krow >= blk), -1.0, 0.0).astype(BF16)
    hidden = (lax.broadcasted_iota(jnp.int32, (blk, blk), 0) >= lax.broadcasted_iota(jnp.int32, (blk, blk), 1))
    for h in range(ATT_HEADS):
        bias2 = bias_ref[grp * ATT_HEADS + h] * LOG2E
        zinit_ref[0, h] = jnp.full((blk, blk), bias2, F32)
        zinit_ref[1, h] = jnp.where(hidden, ATT_NEG, bias2)
    for ref in (z_ref, sp_ref, lb_ref, w_ref, acc_ref, carry_ref):
        ref[...] = jnp.zeros_like(ref)

    def step(i, slot_new):
        slot_old = 1 - slot_new

        f4 = jnp.clip(i - 3, 0, n_tiles - 1)
        q4 = tq_ref[f4]
        j4 = tj_ref[f4]
        keep = jnp.where(j4 == q4, 0.0, 1.0)
        for h in range(ATT_HEADS):
            rows = slice(h * HEAD_DIM, (h + 1) * HEAD_DIM)
            pv = jnp.dot(vt_ref[0, j4, rows, :], w_ref[slot_old, h], preferred_element_type=F32)
            acc_ref[rows, :] = acc_ref[rows, :] * keep + pv
        o_ref[0, q4] = acc_ref[...]

        f3 = jnp.clip(i - 2, 0, n_tiles - 1)
        valid3 = (i >= 2) & (i <= n_tiles + 1)
        first3 = tj_ref[f3] == tq_ref[f3]
        null3 = jnp.where(valid3, 0.0, ATT_NEG)
        count3 = jnp.where(valid3, 1.0, 0.0)
        for h in range(ATT_HEADS):
            sums = jnp.dot(neg_sums, sp_ref[slot_old, h], preferred_element_type=F32)
            carry = jnp.where(first3, 0.0, carry_ref[h])
            w_ref[slot_new, h] = jnp.exp2(lb_ref[slot_old, h] + sums[0:blk] + (carry + null3)).astype(BF16)
            carry_ref[h] = carry + sums[blk:blk + 1] * count3

        for h in range(ATT_HEADS):
            z = z_ref[slot_old, h]
            sp = jnp.maximum(z, 0.0) + jnp.log(1.0 + jnp.exp2(-jnp.abs(z))) * LOG2E
            sp_ref[slot_new, h] = sp.astype(BF16)
            lb_ref[slot_new, h] = z - sp

        f1 = jnp.minimum(i, n_tiles - 1)
        q1 = tq_ref[f1]
        j1 = tj_ref[f1]
        diag1 = jnp.where(j1 == q1, 1, 0)
        k_blk = k_ref[0, pl.ds(pl.multiple_of(j1 * blk, blk), blk), :]
        q_grp = qt_ref[0, q1]
        for h in range(ATT_HEADS):
            q_h = jnp.where(row_head == h, q_grp, jnp.zeros_like(q_grp))
            z_ref[slot_new, h] = jnp.dot(k_blk, q_h, preferred_element_type=F32) + zinit_ref[diag1, h]

    def body(pair, _):
        step(2 * pair, 0)
        step(2 * pair + 1, 1)
        return 0

    lax.fori_loop(0, (n_tiles + 3 + 1) // 2, body, 0)


def _attn_prompt(q_t, k_bf16, v_t, sb_bias, b, t):
    _, nk, _, blk = q_t.shape
    gw = ATT_HEADS * HEAD_DIM
    k3 = k_bf16.reshape(b, t, SB_DIM)
    tiles = [(qi, j) for qi in range(nk) for j in range(qi, -1, -1)]
    tile_q = jnp.array([qi for qi, _ in tiles], jnp.int32)
    tile_j = jnp.array([j for _, j in tiles], jnp.int32)
    blocks = pl.BlockSpec((1, nk, gw, blk), lambda i, g, *_: (i, 0, g, 0))
    grid_spec = pltpu.PrefetchScalarGridSpec(
        num_scalar_prefetch=3,
        grid=(b, SB_HEADS // ATT_HEADS),
        in_specs=[blocks, pl.BlockSpec((1, t, gw), lambda i, g, *_: (i, 0, g)), blocks],
        out_specs=blocks,
        scratch_shapes=[pltpu.VMEM((2, ATT_HEADS, blk, blk), F32), pltpu.VMEM((2, ATT_HEADS, blk, blk), BF16),
                        pltpu.VMEM((2, ATT_HEADS, blk, blk), F32), pltpu.VMEM((2, ATT_HEADS, blk, blk), BF16),
                        pltpu.VMEM((2, ATT_HEADS, blk, blk), F32),
                        pltpu.VMEM((gw, blk), F32), pltpu.VMEM((ATT_HEADS, 1, blk), F32)],
    )
    return pl.pallas_call(
        _attn_prompt_kernel,
        grid_spec=grid_spec,
        out_shape=jax.ShapeDtypeStruct((b, nk, SB_DIM, blk), F32),
        compiler_params=_cparams("parallel", "parallel"),
        name="attn_prompt",
    )(sb_bias, tile_q, tile_j, q_t, k3, v_t)


SAMPLE_SEQS = 2


def _attn_sample_kernel(pt_ref, q_ref, bias_ref, kn_ref, vn_ref, *refs, tlen, n_pages):
    n_seqs = q_ref.shape[0]
    o_ref = refs[2 * n_seqs * n_pages]
    for s in range(n_seqs):
        _attn_sample_sequence(q_ref.at[s], bias_ref, kn_ref.at[s], vn_ref.at[s],
                              refs[s * n_pages:(s + 1) * n_pages],
                              refs[(n_seqs + s) * n_pages:(n_seqs + s + 1) * n_pages], o_ref.at[s], tlen)


def _attn_sample_sequence(q_ref, bias_ref, kn_ref, vn_ref, k_refs, v_refs, o_ref, tlen):
    n_pages = len(k_refs)
    rows = SB_HEADS * tlen
    rhead = lax.broadcasted_iota(jnp.int32, (rows, SB_DIM), 0) // tlen
    lhead = lax.broadcasted_iota(jnp.int32, (rows, SB_DIM), 1) // HEAD_DIM
    head_mask = rhead == lhead
    q_s = jnp.where(head_mask, jnp.concatenate([q_ref[...]] * SB_HEADS, axis=0), 0.0).astype(BF16)
    jrow = lax.broadcasted_iota(jnp.int32, (PAGE_SIZE, 2 * PAGE_SIZE), 0)
    jcol = lax.broadcasted_iota(jnp.int32, (PAGE_SIZE, 2 * PAGE_SIZE), 1)
    tri_ones = jnp.where((jrow > jcol) | (jcol >= PAGE_SIZE), 1.0, 0.0).astype(BF16)
    bias = bias_ref[...]

    def scores(z, mask):
        z = z + bias
        sp = jnp.maximum(z, 0.0) + jnp.log(1.0 + jnp.exp(-jnp.abs(z)))
        lk = -sp
        if mask is not None:
            lk = jnp.where(mask, lk, 0.0)
        lk_hi, lk_lo = _split2(lk)
        sums = (jnp.dot(lk_hi, tri_ones, preferred_element_type=F32)
                + jnp.dot(lk_lo, tri_ones, preferred_element_type=F32))
        return (z - sp) + sums[:, 0:PAGE_SIZE], sums[:, PAGE_SIZE:]

    pad = jnp.zeros((PAGE_SIZE - tlen, SB_DIM), F32)
    k_new = jnp.concatenate([kn_ref[...], pad], axis=0)
    v_new = jnp.concatenate([vn_ref[...], pad], axis=0)
    new_mask = (lax.broadcasted_iota(jnp.int32, (rows, PAGE_SIZE), 1)
                < lax.broadcasted_iota(jnp.int32, (rows, PAGE_SIZE), 0) % tlen)
    order = list(reversed(range(n_pages)))
    zs = [_dg(q_s, k_new, 1, 1)] + [_dot(q_s, k_refs[j][0, 0]) for j in order]
    parts = [scores(zs[0], new_mask)] + [scores(z, None) for z in zs[1:]]
    carry = jnp.zeros((rows, PAGE_SIZE), F32)
    ws = []
    for n, (logw, total) in enumerate(parts):
        w = jnp.exp(logw + carry)
        ws.append(jnp.where(new_mask, w, 0.0) if n == 0 else w)
        carry = carry + total
    acc = _dot(ws[0], v_new)
    for w, j in zip(ws[1:], order):
        acc = acc + _dg(w, v_refs[j][0, 0], 1, 1)
    acc = jnp.where(head_mask, acc, 0.0)
    out = acc[0:tlen]
    for h in range(1, SB_HEADS):
        out = out + acc[h * tlen:(h + 1) * tlen]
    o_ref[...] = out


def _attn_sample(q, k_new, v_new, cache_kt, cache_vt, layer, page_table, sb_bias, b, tlen):
    n_pages = page_table.shape[1]
    rows = SB_HEADS * tlen
    n_seqs = SAMPLE_SEQS if b % SAMPLE_SEQS == 0 else 1
    bias_rows = jnp.broadcast_to(jnp.repeat(sb_bias, tlen)[:, None], (rows, PAGE_SIZE)).astype(F32)
    seq = lambda a: a.reshape(b, tlen, SB_DIM)
    tok = pl.BlockSpec((n_seqs, tlen, SB_DIM), lambda i, pt: (i, 0, 0))
    page = lambda s, j: pl.BlockSpec((1, 1, SB_DIM, PAGE_SIZE),
                                     lambda i, pt, s=s, j=j: (layer, pt[i * n_seqs + s, j], 0, 0))
    pages = [page(s, j) for s in range(n_seqs) for j in range(n_pages)]
    grid_spec = pltpu.PrefetchScalarGridSpec(
        num_scalar_prefetch=1,
        grid=(b // n_seqs,),
        in_specs=[tok, pl.BlockSpec((rows, PAGE_SIZE), lambda i, pt: (0, 0)), tok, tok] + pages + pages,
        out_specs=tok,
    )
    out = pl.pallas_call(
        functools.partial(_attn_sample_kernel, tlen=tlen, n_pages=n_pages),
        grid_spec=grid_spec,
        out_shape=jax.ShapeDtypeStruct((b, tlen, SB_DIM), F32),
        compiler_params=_cparams("parallel"),
        name="attn_sample",
    )(page_table, seq(q), bias_rows, seq(k_new), seq(v_new),
      *([cache_kt] * len(pages)), *([cache_vt] * len(pages)))
    return out.reshape(b * tlen, SB_DIM)


def _tail_kernel(x_ref, y_ref, bonus_ref, g_ref, cb_ref, cc_ref, cx_ref, *refs, tiles_per_seq):
    n_before = 4 if tiles_per_seq is None else 3
    before_refs = refs[:n_before]
    (osb_ref, p_ref, gnw_ref, gnb_ref, convw_ref, ones_ref, wout_ref, gmlp_ref, wup_ref, wdown_ref,
     gple_ref, wgate_ref, wproj_ref, o_ref, x1_ref, h2_ref, acc_ref) = refs[n_before:]
    f = pl.program_id(1)
    nf = pl.num_programs(1)

    @pl.when(f == 0)
    def _():
        y = y_ref[...]
        mu = _dot_exact_rhs(y, ones_ref[...])
        yc = y - mu
        var = _dot_exact_rhs(yc * yc, ones_ref[...])
        yn = yc * lax.rsqrt(var + GN_EPS) * gnw_ref[...] + gnb_ref[...]
        o_rw = (yn + bonus_ref[...]) * g_ref[...]
        cw = convw_ref[...]
        u = cc_ref[...] * cx_ref[...]
        if tiles_per_seq is None:
            cc1_ref, cx1_ref, cc2_ref, cx2_ref = before_refs
            u1 = cc1_ref[...] * cx1_ref[...]
            u2 = cc2_ref[...] * cx2_ref[...]
        else:
            cch_ref, cxh_ref, state_ref = before_refs
            u_halo = cch_ref[...] * cxh_ref[...]
            is_first = pl.program_id(0) % tiles_per_seq == 0
            u1 = _rows_before(u, u_halo, state_ref[0], is_first, 1)
            u2 = _rows_before(u, u_halo, state_ref[0], is_first, 2)
        o_cv = cb_ref[...] * (cw[0:1] * u2 + cw[1:2] * u1 + cw[2:3] * u)
        mix = _dot(o_rw, wout_ref[0:RW_DIM, :])
        mix = mix + _dot(o_cv, wout_ref[RW_DIM:RW_DIM + CONV_DIM, :])
        w_sb = wout_ref[RW_DIM + CONV_DIM:, :]
        if len(osb_ref.shape) == 2:
            mix = mix + _dot(osb_ref[...], w_sb)
        else:
            mix = mix + jnp.concatenate(
                [_dot(osb_ref[0, s].T, w_sb) for s in range(osb_ref.shape[1])], axis=0)
        x1 = x_ref[...] + mix
        x1_ref[...] = x1
        h2_ref[...] = _rmsnorm_rows(x1, gmlp_ref[...]).astype(BF16)
        acc_ref[...] = jnp.zeros_like(acc_ref)

    up = jnp.dot(h2_ref[...], wup_ref[...], preferred_element_type=F32)
    act = jnp.square(jnp.maximum(up, 0.0))
    acc_ref[...] += _dot(act, wdown_ref[...])

    @pl.when(f == nf - 1)
    def _():
        x2 = x1_ref[...] + acc_ref[...]
        gate = _sigmoid(_dot(_rmsnorm_rows(x2, gple_ref[...]), wgate_ref[...]))
        o_ref[...] = x2 + gate * _dot(p_ref[...], wproj_ref[...])


def _tail(x2d, y, bonus, g, p_cv, conv_prev, seq_len, o_sb, p3d, layer, lw, wts, tm, tf):
    n, d = x2d.shape
    b = n // seq_len
    d_ff = wts["w_up"].shape[1]
    ones = _head_block_ones(RW_DIM, 1.0 / HEAD_DIM)
    vec = lambda a: a.reshape(1, -1)
    row = lambda w, col=0: pl.BlockSpec((tm, w), lambda i, f, col=col: (i, col))
    full = lambda a: pl.BlockSpec(a.shape, lambda i, f: (0,) * a.ndim)
    gnw, gnb, gmlp, gple = vec(lw["gn_w"]), vec(lw["gn_b"]), vec(lw["g_mlp"]), vec(lw["g_ple"])
    if o_sb.ndim == 2:
        osb_spec = row(SB_DIM)
    else:
        _, nblk, _, blk = o_sb.shape
        osb_tiles = nblk * blk // tm
        osb_spec = pl.BlockSpec((1, tm // blk, SB_DIM, blk), lambda i, f: (i // osb_tiles, i % osb_tiles, 0, 0))
    if seq_len % tm == 0:
        tiles_per_seq = seq_len // tm
        halo = lambda col: pl.BlockSpec((8, CONV_DIM), lambda i, f: (jnp.maximum(i * (tm // 8) - 1, 0), col))
        before = [p_cv, p_cv, conv_prev]
        before_specs = [halo(1), halo(2),
                        pl.BlockSpec((1, CONV_K - 1, CONV_DIM), lambda i, f: (i // tiles_per_seq, 0, 0))]
    else:
        tiles_per_seq = None
        p_cv3 = p_cv.reshape(b, seq_len, 3 * CONV_DIM)
        state_rows = lambda s: jnp.concatenate([jnp.zeros_like(s), s, jnp.ones_like(s)], axis=-1)
        p_cv1 = _shift_rows(p_cv3, state_rows(conv_prev[:, 1:2]), 1).reshape(n, 3 * CONV_DIM)
        p_cv2 = _shift_rows(p_cv3, state_rows(conv_prev), 2).reshape(n, 3 * CONV_DIM)
        before = [p_cv1, p_cv1, p_cv2, p_cv2]
        before_specs = [row(CONV_DIM, 1), row(CONV_DIM, 2), row(CONV_DIM, 1), row(CONV_DIM, 2)]
    in_specs = [row(d), row(RW_DIM), row(RW_DIM), row(RW_DIM),
                row(CONV_DIM, 0), row(CONV_DIM, 1), row(CONV_DIM, 2)] + before_specs + [
                osb_spec, pl.BlockSpec((None, tm, p3d.shape[2]), lambda i, f: (layer, i, 0)),
                full(gnw), full(gnb), full(lw["conv_w"]), full(ones), full(wts["w_out"]), full(gmlp),
                pl.BlockSpec((d, tf), lambda i, f: (0, f)), pl.BlockSpec((tf, d), lambda i, f: (f, 0)),
                full(gple), full(wts["w_ple_gate"]), full(wts["w_ple_proj"])]
    return pl.pallas_call(
        functools.partial(_tail_kernel, tiles_per_seq=tiles_per_seq),
        grid=(n // tm, d_ff // tf),
        in_specs=in_specs,
        out_specs=pl.BlockSpec((tm, d), lambda i, f: (i, 0)),
        out_shape=jax.ShapeDtypeStruct((n, d), F32),
        scratch_shapes=[pltpu.VMEM((tm, d), F32), pltpu.VMEM((tm, d), BF16), pltpu.VMEM((tm, d), F32)],
        compiler_params=_cparams("parallel", "arbitrary"),
        name="tail",
    )(x2d, y, bonus, g, p_cv, p_cv, p_cv, *before, o_sb, p3d,
      gnw, gnb, lw["conv_w"], ones, wts["w_out"], gmlp, wts["w_up"], wts["w_down"],
      gple, wts["w_ple_gate"], wts["w_ple_proj"])


def _conv_state_kernel(cv_ref, o_ref):
    o_ref[...] = cv_ref[:, :, CONV_DIM:2 * CONV_DIM] * cv_ref[:, :, 2 * CONV_DIM:3 * CONV_DIM]


def _conv_state(p_cv3):
    b, t, c = p_cv3.shape
    return pl.pallas_call(
        _conv_state_kernel,
        grid=(1,),
        in_specs=[pl.BlockSpec((b, 8, c), lambda i: (0, t // 8 - 1, 0))],
        out_specs=pl.BlockSpec((b, 8, CONV_DIM), lambda i: (0, 0, 0)),
        out_shape=jax.ShapeDtypeStruct((b, 8, CONV_DIM), F32),
        name="conv_state",
    )(p_cv3)


def _shift_rows(x3, first_rows, n):
    return jnp.concatenate([first_rows, x3[:, :-n]], axis=1)


def _layer(x2d, p3d, layer, b, t, shift_prev, wkv_prev, conv_prev, attn_fn, att_blk, lw, wts, tm, tf,
           wkv_nseq, wkv_tlen, wkv_groups):
    n, d = x2d.shape
    p_rw, p_cv, k_new, v_new, r, lwd, k2, v, av, bv, g, bonus, *attn_ops = _in_proj(
        x2d, shift_prev, lw, wts["w_in"], tm, t, att_blk)
    p_rw3 = p_rw.reshape(b, t, p_rw.shape[1])
    grp = lambda a: a.reshape(-1, (n // (b // wkv_nseq)), RW_DIM) if wkv_nseq > 1 else a.reshape(b, t, RW_DIM)
    y, wkv_new = _wkv(grp(r), grp(lwd), grp(k2), grp(v), grp(av), grp(bv), wkv_prev, wkv_nseq, wkv_tlen,
                   wkv_groups)
    y = y.reshape(n, RW_DIM)

    o_sb = attn_fn(k_new, v_new, *attn_ops)

    x_out = _tail(x2d, y, bonus, g, p_cv, conv_prev, t, o_sb, p3d, layer, lw, wts, tm, tf)
    conv_new = _conv_state(p_cv.reshape(b, t, 3 * CONV_DIM))[:, 8 - (CONV_K - 1):]
    return x_out, k_new, v_new, wkv_new, p_rw3[:, -1], conv_new


def _run_trunk(x, p, shift, wkv, conv, attn_builder, att_blk, params, bf16_w, tm, tf,
               wkv_nseq, wkv_tlen, wkv_groups):
    b, t, d = x.shape
    depth = p.shape[0]
    x2d = x.reshape(b * t, d)
    ks, vs, wkvs, shifts, convs = [], [], [], [], []
    for l in range(depth):
        lw = {name: arr[l] for name, arr in params.items()}
        wts = {name: arr[l] for name, arr in bf16_w.items()}
        x2d, k_new, v_new, wkv_new, shift_new, conv_new = _layer(
            x2d, p.reshape(depth, b * t, -1), l, b, t, shift[l], wkv[l].reshape(b, RW_DIM, HEAD_DIM), conv[l],
            attn_builder(l, lw), att_blk, lw, wts, tm, tf, wkv_nseq, wkv_tlen, wkv_groups)
        ks.append(k_new.reshape(b, t, SB_HEADS, HEAD_DIM))
        vs.append(v_new.reshape(b, t, SB_HEADS, HEAD_DIM))
        wkvs.append(wkv_new.reshape(b, RW_HEADS, HEAD_DIM, HEAD_DIM))
        shifts.append(shift_new)
        convs.append(conv_new)
    return (x2d.reshape(b, t, d), jnp.stack(ks), jnp.stack(vs), jnp.stack(wkvs), jnp.stack(shifts),
            jnp.stack(convs))


def kernel(x_prompt, x_sample, cache_k, cache_v, state_wkv, state_shift, state_conv, page_table,
           p_prompt, p_sample, g_mix, w_in, mu_shift, w0, w2, a0, a2, g2, k_k, k_a, r_k,
           gn_w, gn_b, conv_w, q_gain, k_gain, sb_bias, w_out, g_mlp, w_up, w_down, g_ple,
           w_ple_gate, w_ple_proj):
    params = dict(g_mix=g_mix, mu_shift=mu_shift, w0=w0, w2=w2, a0=a0, a2=a2, g2=g2,
                  k_k=k_k, k_a=k_a, r_k=r_k, gn_w=gn_w, gn_b=gn_b, conv_w=conv_w,
                  q_gain=q_gain, k_gain=k_gain, sb_bias=sb_bias, g_mlp=g_mlp, g_ple=g_ple)
    bf16_w = dict(w_in=w_in.astype(BF16), w_out=w_out.astype(BF16), w_up=w_up.astype(BF16),
                  w_down=w_down.astype(BF16), w_ple_gate=w_ple_gate.astype(BF16),
                  w_ple_proj=w_ple_proj.astype(BF16))
    depth = w_in.shape[0]
    b, t, d = x_prompt.shape
    db, dt, _ = x_sample.shape
    rw_proj = state_shift.shape[-1]
    d_ff = w_up.shape[-1]
    tf = min(2048, d_ff)

    n_phys = cache_k.shape[1]
    page_view = lambda c: c.transpose(0, 1, 3, 4, 2).reshape(depth, n_phys, SB_DIM, PAGE_SIZE)
    cache_kt, cache_vt = page_view(cache_k), page_view(cache_v)

    def prompt_attn(l, lw):
        return lambda k, v, q_t, k_bf16, v_t: _attn_prompt(q_t, k_bf16, v_t, lw["sb_bias"], b, t)

    def sample_attn(l, lw):
        return lambda k, v, q: _attn_sample(q, k, v, cache_kt, cache_vt, l, page_table, lw["sb_bias"], db, dt)

    dtp = x_prompt.dtype
    shift0 = jnp.zeros((depth, b, rw_proj), dtp)
    wkv0 = jnp.zeros((depth, b, RW_HEADS, HEAD_DIM, HEAD_DIM), dtp)
    conv0 = jnp.zeros((depth, b, CONV_K - 1, CONV_DIM), dtp)
    (y_prompt, k_prompt, v_prompt, wkv_prompt, shift_prompt, conv_prompt) = _run_trunk(
        x_prompt, p_prompt, shift0, wkv0, conv0, prompt_attn, min(ATT_BLOCK, t), params, bf16_w,
        min(512, t), tf, 1, WKV_ROWS, b)
    seqs_per_group = WKV_ROWS // dt
    (y_sample, k_sample, v_sample, wkv_sample, shift_sample, conv_sample) = _run_trunk(
        x_sample, p_sample, state_shift, state_wkv, state_conv, sample_attn, None, params, bf16_w,
        min(512, db * dt), tf, seqs_per_group, dt, min(2, db // seqs_per_group))
    return (y_prompt, y_sample, k_prompt, v_prompt, wkv_prompt, shift_prompt, conv_prompt,
            k_sample, v_sample, wkv_sample, shift_sample, conv_sample)
```

```python
import functools

import jax
import jax.numpy as jnp
from jax import lax
from jax.experimental import pallas as pl
from jax.experimental.pallas import tpu as pltpu

F32 = jnp.float32
BF16 = jnp.bfloat16

HEAD_DIM = 64
RW_HEADS = 4
RW_DIM = RW_HEADS * HEAD_DIM
LORA_WA = 128
CONV_DIM = 256
CONV_K = 3
SB_HEADS = 8
SB_DIM = SB_HEADS * HEAD_DIM
PAGE_SIZE = 128
RMS_EPS = 1e-6
GN_EPS = 64e-5
KK_EPS = 1e-12
LOG2E = 1.4426950408889634

WKV_ROWS = 64
ATT_BLOCK = 256
ATT_HEADS = 4
VMEM_LIMIT_BYTES = 56 * 1024 * 1024


def _cparams(*sem):
    return pltpu.CompilerParams(dimension_semantics=sem, vmem_limit_bytes=VMEM_LIMIT_BYTES)


def _dot(a, b):
    return jnp.dot(a.astype(BF16), b.astype(BF16), preferred_element_type=F32)


def _dg(a, b, ca, cb):
    return lax.dot_general(a.astype(BF16), b.astype(BF16), (((ca,), (cb,)), ((), ())),
                           preferred_element_type=F32)


def _split2(a):
    hi = a.astype(BF16)
    lo = (a - hi.astype(F32)).astype(BF16)
    return hi, lo


def _dot_exact_rhs(a, b_bf16):
    h1 = a.astype(BF16)
    r1 = a - h1.astype(F32)
    h2 = r1.astype(BF16)
    h3 = (r1 - h2.astype(F32)).astype(BF16)
    out = jnp.dot(h1, b_bf16, preferred_element_type=F32)
    out = out + jnp.dot(h2, b_bf16, preferred_element_type=F32)
    out = out + jnp.dot(h3, b_bf16, preferred_element_type=F32)
    return out


def _dot_exact_lhs(a_bf16, b):
    h1 = b.astype(BF16)
    r1 = b - h1.astype(F32)
    h2 = r1.astype(BF16)
    h3 = (r1 - h2.astype(F32)).astype(BF16)
    out = jnp.dot(a_bf16, h1, preferred_element_type=F32)
    out = out + jnp.dot(a_bf16, h2, preferred_element_type=F32)
    out = out + jnp.dot(a_bf16, h3, preferred_element_type=F32)
    return out


def _softplus(x):
    return jnp.maximum(x, 0.0) + jnp.log1p(jnp.exp(-jnp.abs(x)))


def _sigmoid(x):
    return 1.0 / (1.0 + jnp.exp(-x))


def _rmsnorm_rows(x, g):
    ms = jnp.mean(x * x, axis=-1, keepdims=True)
    return x * lax.rsqrt(ms + RMS_EPS) * g


def _head_block_ones(width, scale):
    i = jnp.arange(width) // HEAD_DIM
    return jnp.where(i[:, None] == i[None, :], scale, 0.0).astype(BF16)


def _rows_before(cur, halo, first_rows, is_first, n):
    row = lax.broadcasted_iota(jnp.int32, cur.shape, 0)
    out = pltpu.roll(cur, n, 0)
    n_halo, n_first = halo.shape[0], first_rows.shape[0]
    for k in range(n):
        before = jnp.where(is_first, first_rows[n_first - n + k:n_first - n + k + 1],
                           halo[n_halo - n + k:n_halo - n + k + 1])
        out = jnp.where(row == k, before, out)
    return out


def _rwkv_heads(p, prev, mu_ref, w0_ref, a0_ref, kk_ref, ka_ref, rk_ref, wwa_ref, g2_ref, ones_ref,
                r_ref, lw_ref, k_ref, v_ref, av_ref, bv_ref, g_ref, bonus_ref):
    xs = p + (prev - p) * mu_ref[...]
    r = xs[:, 0:RW_DIM]
    k = xs[:, RW_DIM:2 * RW_DIM]
    v = xs[:, 2 * RW_DIM:3 * RW_DIM]
    d = xs[:, 3 * RW_DIM:3 * RW_DIM + LORA_WA]
    dg = xs[:, 3 * RW_DIM + LORA_WA:]
    lane = lax.broadcasted_iota(jnp.int32, d.shape, 1)
    dwa = jnp.where(lane < LORA_WA // 2, jnp.tanh(d), d)
    wa = _dot(dwa, wwa_ref[...])
    w_log = -_softplus(-(w0_ref[...] + wa[:, 0:RW_DIM])) - 0.5
    lw_ref[...] = -jnp.exp(w_log)
    a = _sigmoid(a0_ref[...] + wa[:, RW_DIM:])
    g_ref[...] = _dot(_sigmoid(dg), g2_ref[...])
    kk = k * kk_ref[...]
    ss = _dot_exact_rhs(kk * kk, ones_ref[...])
    kk = kk * lax.rsqrt(ss + KK_EPS)
    k2 = k * (1.0 + (a - 1.0) * ka_ref[...])
    r_ref[...] = r
    k_ref[...] = k2
    v_ref[...] = v
    av_ref[...] = -kk
    bv_ref[...] = kk * a
    bonus_ref[...] = _dot_exact_rhs(r * k2 * rk_ref[...], ones_ref[...]) * v


N_RWKV_PARAMS = 9
N_RWKV_OUTS = 8


def _in_proj_kernel(x_ref, g_ref, w_ref, ebd_ref, qg_ref, kg_ref, state_ref, *refs,
                    att_blk, seq_len, tiles_per_seq, layer):
    rwkv_params = refs[:N_RWKV_PARAMS]
    refs = refs[N_RWKV_PARAMS:]
    kv_prev = ()
    if layer > 0:
        kv_prev, refs = refs[:2], refs[2:]
    rw_ref, cv_ref, k_ref, v_ref = refs[:4]
    rwkv_outs = refs[4:4 + N_RWKV_OUTS]
    attn_refs = refs[4 + N_RWKV_OUTS:]
    for j in range(k_ref.shape[0]):
        if j != layer:
            k_ref[j] = kv_prev[0][j] if kv_prev else jnp.zeros(k_ref.shape[1:], F32)
            v_ref[j] = kv_prev[1][j] if kv_prev else jnp.zeros(v_ref.shape[1:], F32)
    h = _rmsnorm_rows(x_ref[...], g_ref[...]).astype(BF16)
    c0 = rw_ref.shape[1]
    c1 = c0 + cv_ref.shape[1]
    p = jnp.dot(h, w_ref[:, 0:c0], preferred_element_type=F32)
    rw_ref[...] = p
    rows = p.shape[0]
    if tiles_per_seq is None:
        nseq = rows // seq_len
        r_i = lax.broadcasted_iota(jnp.int32, (rows, nseq), 0)
        s_i = lax.broadcasted_iota(jnp.int32, (rows, nseq), 1)
        starts = jnp.where(r_i == s_i * seq_len, 1.0, 0.0).astype(BF16)
        t_local = lax.broadcasted_iota(jnp.int32, p.shape, 0) % seq_len
        prev = jnp.where(t_local == 0, _dot_exact_lhs(starts, state_ref[...]), pltpu.roll(p, 1, 0))
    else:
        attn_refs, last_ref = attn_refs[:-1], attn_refs[-1]

        @pl.when(pl.program_id(0) == 0)
        def _():
            last_ref[...] = jnp.zeros_like(last_ref)

        prev = _rows_before(p, last_ref[...], state_ref[0], pl.program_id(0) % tiles_per_seq == 0, 1)
        last_ref[...] = p[rows - last_ref.shape[0]:rows]
    _rwkv_heads(p, prev, *rwkv_params, *rwkv_outs)
    cv_ref[...] = jnp.dot(h, w_ref[:, c0:c1], preferred_element_type=F32)

    def head_norm(t, gain):
        ms = jnp.dot((t * t).astype(BF16), ebd_ref[...], preferred_element_type=F32)
        return t * lax.rsqrt(ms + RMS_EPS) * gain

    q = jnp.dot(h, w_ref[:, c1:c1 + SB_DIM], preferred_element_type=F32)
    q = head_norm(q, qg_ref[...]) * (HEAD_DIM ** -0.5)
    k = jnp.dot(h, w_ref[:, c1 + SB_DIM:c1 + 2 * SB_DIM], preferred_element_type=F32)
    k = head_norm(k, kg_ref[...])
    k_ref[layer] = k
    v = jnp.dot(h, w_ref[:, c1 + 2 * SB_DIM:c1 + 3 * SB_DIM], preferred_element_type=F32)
    v_ref[layer] = v
    if att_blk is None:
        (q_ref,) = attn_refs
        q_ref[...] = q
    else:
        qt_ref, kb_ref, vt_ref = attn_refs
        kb_ref[...] = k.astype(BF16)
        q_t = (q * LOG2E).T
        v_t = v.T
        for s in range(qt_ref.shape[1]):
            qt_ref[0, s] = q_t[:, s * att_blk:(s + 1) * att_blk].astype(BF16)
            vt_ref[0, s] = v_t[:, s * att_blk:(s + 1) * att_blk].astype(BF16)


def _in_proj(x2d, shift_prev, lw, w_in_bf16, tm, seq_len, att_blk, layer, depth, kv_prev):
    n, d = x2d.shape
    cols = w_in_bf16.shape[1]
    rw_cols = cols - 3 * CONV_DIM - 3 * SB_DIM
    ebd = _head_block_ones(SB_DIM, 1.0 / HEAD_DIM)
    qg = jnp.tile(lw["q_gain"], SB_HEADS)[None, :]
    kg = jnp.tile(lw["k_gain"], SB_HEADS)[None, :]
    row = lambda w: pl.BlockSpec((tm, w), lambda i: (i, 0))
    full = lambda a: pl.BlockSpec(a.shape, lambda i: (0,) * a.ndim)
    g2d = lw["g_mix"][None, :]
    w_lora = lw["w2"].shape[0]
    a_lora = lw["a2"].shape[0]
    wwa = jnp.zeros((w_lora + a_lora, 2 * RW_DIM), F32)
    wwa = wwa.at[:w_lora, :RW_DIM].set(lw["w2"]).at[w_lora:, RW_DIM:].set(lw["a2"]).astype(BF16)
    vec = lambda a: a.reshape(1, -1)
    rwkv_params = [vec(lw["mu_shift"]), vec(lw["w0"]), vec(lw["a0"]), vec(lw["k_k"]), vec(lw["k_a"]),
                   vec(lw["r_k"]), wwa, lw["g2"].astype(BF16), _head_block_ones(RW_DIM, 1.0)]
    assert len(rwkv_params) == N_RWKV_PARAMS
    scratch = []
    if seq_len % tm == 0:
        tiles_per_seq = seq_len // tm
        state, state_spec = shift_prev[:, None, :], pl.BlockSpec((1, 1, rw_cols), lambda i: (i // tiles_per_seq, 0, 0))
        scratch = [pltpu.VMEM((8, rw_cols), F32)]
    else:
        tiles_per_seq = None
        state, state_spec = shift_prev, pl.BlockSpec((tm // seq_len, rw_cols), lambda i: (i, 0))
    kv_spec = pl.BlockSpec((depth, tm, SB_DIM), lambda i: (0, i, 0))
    kv_shape = jax.ShapeDtypeStruct((depth, n, SB_DIM), F32)
    out_specs = [row(rw_cols), row(3 * CONV_DIM), kv_spec, kv_spec] + [row(RW_DIM)] * N_RWKV_OUTS
    out_shape = [jax.ShapeDtypeStruct((n, rw_cols), F32), jax.ShapeDtypeStruct((n, 3 * CONV_DIM), F32),
                 kv_shape, kv_shape]
    out_shape += [jax.ShapeDtypeStruct((n, RW_DIM), F32)] * N_RWKV_OUTS
    inputs = [x2d, g2d, w_in_bf16, ebd, qg, kg, state, *rwkv_params]
    in_specs = [row(d), full(g2d), full(w_in_bf16), full(ebd), full(qg), full(kg), state_spec]
    in_specs += [full(a) for a in rwkv_params]
    aliases = {}
    if layer > 0:
        aliases = {len(inputs): 2, len(inputs) + 1: 3}
        inputs += list(kv_prev)
        in_specs += [kv_spec, kv_spec]
    if att_blk is None:
        out_specs.append(row(SB_DIM))
        out_shape.append(jax.ShapeDtypeStruct((n, SB_DIM), F32))
    else:
        att_tiles = seq_len // tm
        blks = tm // att_blk
        t_spec = pl.BlockSpec((1, blks, SB_DIM, att_blk), lambda i: (i // att_tiles, i % att_tiles, 0, 0))
        t_shape = jax.ShapeDtypeStruct((n // seq_len, seq_len // att_blk, SB_DIM, att_blk), BF16)
        out_specs += [t_spec, row(SB_DIM), t_spec]
        out_shape += [t_shape, jax.ShapeDtypeStruct((n, SB_DIM), BF16), t_shape]
    return pl.pallas_call(
        functools.partial(_in_proj_kernel, att_blk=att_blk, seq_len=seq_len, tiles_per_seq=tiles_per_seq,
                          layer=layer),
        grid=(n // tm,),
        in_specs=in_specs,
        out_specs=out_specs,
        out_shape=out_shape,
        scratch_shapes=scratch,
        input_output_aliases=aliases,
        compiler_params=_cparams("arbitrary"),
        name="in_proj",
    )(*inputs)


def _wkv_kernel(r_ref, lw_ref, k_ref, v_ref, a_ref, b_ref, s0_ref, y_ref, sout_ref, s_ref, *, nseq, tlen):
    P = WKV_ROWS
    HP = RW_HEADS * P
    c = pl.program_id(1)
    own_head = (lax.broadcasted_iota(jnp.int32, (RW_DIM, RW_DIM), 0) // HEAD_DIM
                == lax.broadcasted_iota(jnp.int32, (RW_DIM, RW_DIM), 1) // HEAD_DIM)

    @pl.when(c == 0)
    def _():
        for j in range(s_ref.shape[0]):
            s_ref[j] = jnp.where(own_head, jnp.concatenate([s0_ref[j]] * RW_HEADS, axis=1), 0.0)

    prow = lax.broadcasted_iota(jnp.int32, (P, P), 0)
    pcol = lax.broadcasted_iota(jnp.int32, (P, P), 1)
    same_seq = (prow // tlen) == (pcol // tlen)
    tri_incl = jnp.where(same_seq & (pcol <= prow), 1.0, 0.0).astype(BF16)
    last_sel = jnp.where(same_seq & (pcol == (prow // tlen) * tlen + (tlen - 1)), 1.0, 0.0).astype(BF16)
    srow = lax.broadcasted_iota(jnp.int32, (HP, HP), 0)
    scol = lax.broadcasted_iota(jnp.int32, (HP, HP), 1)
    same_blk = ((srow // tlen) == (scol // tlen))
    m_strict = same_blk & (scol < srow)
    m_incl = same_blk & (scol <= srow)
    eye = jnp.where(srow == scol, 1.0, 0.0)
    hrow = lax.broadcasted_iota(jnp.int32, (HP, RW_DIM), 0)
    head_mask = (hrow // P) == (lax.broadcasted_iota(jnp.int32, (HP, RW_DIM), 1) // HEAD_DIM)
    seq_of_row = (hrow % P) // tlen

    def stack(x):
        return jnp.where(head_mask, jnp.concatenate([x] * RW_HEADS, axis=0), 0.0)

    groups = range(r_ref.shape[0])
    lw = [lw_ref[g] for g in groups]
    cw = [_dot_exact_lhs(tri_incl, x) for x in lw]
    cw_end = [_dot_exact_lhs(last_sel, x) for x in cw]
    w_inv = [jnp.exp(-x) for x in cw]
    w_toend = [jnp.exp(e - x) for e, x in zip(cw_end, cw)]
    a_s = [stack(a_ref[g] * jnp.exp(cw[g] - lw[g])) for g in groups]
    r_s = [stack(r_ref[g] * jnp.exp(cw[g])) for g in groups]
    b_s = [stack(b_ref[g] * w_inv[g]) for g in groups]
    k_s = [stack(k_ref[g] * w_inv[g]) for g in groups]
    v_s = [stack(v_ref[g]) for g in groups]
    bend_s = [stack(b_ref[g] * w_toend[g]) for g in groups]
    kend_s = [stack(k_ref[g] * w_toend[g]) for g in groups]

    n_ab = [jnp.where(m_strict, _dg(a_s[g], b_s[g], 1, 1), 0.0) for g in groups]
    a_ak = [jnp.where(m_strict, _dg(a_s[g], k_s[g], 1, 1), 0.0) for g in groups]
    a_rb = [jnp.where(m_incl, _dg(r_s[g], b_s[g], 1, 1), 0.0) for g in groups]
    a_rk = [jnp.where(m_incl, _dg(r_s[g], k_s[g], 1, 1), 0.0) for g in groups]

    t_inv = [eye + x for x in n_ab]
    pw = n_ab
    span = 2
    while span < tlen:
        pw = [_dot(x, x) for x in pw]
        t_inv = [t + _dot(t, x) for t, x in zip(t_inv, pw)]
        span *= 2

    ap_s = [_dot(t_inv[g], a_s[g]) for g in groups]
    akv = [_dot(a_ak[g], v_s[g]) for g in groups]
    u1_s = [_dot(t_inv[g], akv[g]) for g in groups]
    y_s = [_dot(a_rb[g], u1_s[g]) + _dot(a_rk[g], v_s[g]) for g in groups]
    rp_s = [r_s[g] + _dot(a_rb[g], ap_s[g]) for g in groups]
    wc_row = [jnp.exp(x) for x in cw_end]

    for i in range(nseq):
        if nseq == 1:
            sel = lambda x: x
        else:
            sel = lambda x: jnp.where(seq_of_row == i, x, 0.0)
        s_i = [s_ref[g * nseq + i] for g in groups]
        u_i = [_dg(sel(ap_s[g]), s_i[g], 1, 1) + sel(u1_s[g]) for g in groups]
        y_s = [y_s[g] + _dg(sel(rp_s[g]), s_i[g], 1, 1) for g in groups]
        for g in groups:
            wc_i = wc_row[g][i * tlen:i * tlen + 1, :]
            s_ref[g * nseq + i] = (s_i[g] * wc_i + _dg(u_i[g], sel(bend_s[g]), 0, 0)
                                   + _dg(sel(v_s[g]), sel(kend_s[g]), 0, 0))
    for g in groups:
        y = y_s[g][0:P]
        for h in range(1, RW_HEADS):
            y = y + y_s[g][h * P:(h + 1) * P]
        y_ref[g] = y

    @pl.when(c == pl.num_programs(1) - 1)
    def _():
        for j in range(s_ref.shape[0]):
            own = jnp.where(own_head, s_ref[j], 0.0)
            folded = own[:, 0:HEAD_DIM]
            for h in range(1, RW_HEADS):
                folded = folded + own[:, h * HEAD_DIM:(h + 1) * HEAD_DIM]
            sout_ref[j] = folded


def _wkv(r, lw, k, v, av, bv, s0, nseq, tlen, groups):
    g, rows, _ = r.shape
    n_chunks = rows // WKV_ROWS
    blk = pl.BlockSpec((groups, WKV_ROWS, RW_DIM), lambda i, c: (i, c, 0))
    sblk = pl.BlockSpec((groups * nseq, RW_DIM, HEAD_DIM), lambda i, c: (i, 0, 0))
    return pl.pallas_call(
        functools.partial(_wkv_kernel, nseq=nseq, tlen=tlen),
        grid=(g // groups, n_chunks),
        in_specs=[blk] * 6 + [sblk],
        out_specs=[blk, sblk],
        out_shape=[jax.ShapeDtypeStruct(r.shape, F32), jax.ShapeDtypeStruct(s0.shape, F32)],
        scratch_shapes=[pltpu.VMEM((groups * nseq, RW_DIM, RW_DIM), F32)],
        compiler_params=_cparams("parallel", "arbitrary"),
        name="wkv",
    )(r, lw, k, v, av, bv, s0)


ATT_NEG = -1e30


def _attn_prompt_kernel(bias_ref, tq_ref, tj_ref, qt_ref, k_ref, vt_ref, o_ref,
                        z_ref, sp_ref, lb_ref, w_ref, zinit_ref, acc_ref, carry_ref):
    grp = pl.program_id(1)
    nk, gw, blk = qt_ref.shape[1:]
    n_tiles = nk * (nk + 1) // 2
    row_head = lax.broadcasted_iota(jnp.int32, (gw, blk), 0) // HEAD_DIM
    krow = lax.broadcasted_iota(jnp.int32, (blk + 8, blk), 0)
    kcol = lax.broadcasted_iota(jnp.int32, (blk + 8, blk), 1)
    neg_sums = jnp.where((kcol > krow) | (krow >= blk), -1.0, 0.0).astype(BF16)
    hidden = (lax.broadcasted_iota(jnp.int32, (blk, blk), 0) >= lax.broadcasted_iota(jnp.int32, (blk, blk), 1))
    for h in range(ATT_HEADS):
        bias2 = bias_ref[grp * ATT_HEADS + h] * LOG2E
        zinit_ref[0, h] = jnp.full((blk, blk), bias2, F32)
        zinit_ref[1, h] = jnp.where(hidden, ATT_NEG, bias2)
    for ref in (z_ref, sp_ref, lb_ref, w_ref, acc_ref, carry_ref):
        ref[...] = jnp.zeros_like(ref)

    def step(i, slot_new):
        slot_old = 1 - slot_new

        f4 = jnp.clip(i - 3, 0, n_tiles - 1)
        q4 = tq_ref[f4]
        j4 = tj_ref[f4]
        keep = jnp.where(j4 == q4, 0.0, 1.0)
        for h in range(ATT_HEADS):
            rows = slice(h * HEAD_DIM, (h + 1) * HEAD_DIM)
            pv = jnp.dot(vt_ref[0, j4, rows, :], w_ref[slot_old, h], preferred_element_type=F32)
            acc_ref[rows, :] = acc_ref[rows, :] * keep + pv
        o_ref[0, q4] = acc_ref[...]

        f3 = jnp.clip(i - 2, 0, n_tiles - 1)
        valid3 = (i >= 2) & (i <= n_tiles + 1)
        first3 = tj_ref[f3] == tq_ref[f3]
        null3 = jnp.where(valid3, 0.0, ATT_NEG)
        count3 = jnp.where(valid3, 1.0, 0.0)
        for h in range(ATT_HEADS):
            sums = jnp.dot(neg_sums, sp_ref[slot_old, h], preferred_element_type=F32)
            carry = jnp.where(first3, 0.0, carry_ref[h])
            w_ref[slot_new, h] = jnp.exp2(lb_ref[slot_old, h] + sums[0:blk] + (carry + null3)).astype(BF16)
            carry_ref[h] = carry + sums[blk:blk + 1] * count3

        for h in range(ATT_HEADS):
            z = z_ref[slot_old, h]
            sp = jnp.maximum(z, 0.0) + jnp.log(1.0 + jnp.exp2(-jnp.abs(z))) * LOG2E
            sp_ref[slot_new, h] = sp.astype(BF16)
            lb_ref[slot_new, h] = z - sp

        f1 = jnp.minimum(i, n_tiles - 1)
        q1 = tq_ref[f1]
        j1 = tj_ref[f1]
        diag1 = jnp.where(j1 == q1, 1, 0)
        k_blk = k_ref[0, pl.ds(pl.multiple_of(j1 * blk, blk), blk), :]
        q_grp = qt_ref[0, q1]
        for h in range(ATT_HEADS):
            q_h = jnp.where(row_head == h, q_grp, jnp.zeros_like(q_grp))
            z_ref[slot_new, h] = jnp.dot(k_blk, q_h, preferred_element_type=F32) + zinit_ref[diag1, h]

    def body(pair, _):
        step(2 * pair, 0)
        step(2 * pair + 1, 1)
        return 0

    lax.fori_loop(0, (n_tiles + 3 + 1) // 2, body, 0)


def _attn_prompt(q_t, k_bf16, v_t, sb_bias, b, t):
    _, nk, _, blk = q_t.shape
    gw = ATT_HEADS * HEAD_DIM
    k3 = k_bf16.reshape(b, t, SB_DIM)
    tiles = [(qi, j) for qi in range(nk) for j in range(qi, -1, -1)]
    tile_q = jnp.array([qi for qi, _ in tiles], jnp.int32)
    tile_j = jnp.array([j for _, j in tiles], jnp.int32)
    blocks = pl.BlockSpec((1, nk, gw, blk), lambda i, g, *_: (i, 0, g, 0))
    grid_spec = pltpu.PrefetchScalarGridSpec(
        num_scalar_prefetch=3,
        grid=(b, SB_HEADS // ATT_HEADS),
        in_specs=[blocks, pl.BlockSpec((1, t, gw), lambda i, g, *_: (i, 0, g)), blocks],
        out_specs=blocks,
        scratch_shapes=[pltpu.VMEM((2, ATT_HEADS, blk, blk), F32), pltpu.VMEM((2, ATT_HEADS, blk, blk), BF16),
                        pltpu.VMEM((2, ATT_HEADS, blk, blk), F32), pltpu.VMEM((2, ATT_HEADS, blk, blk), BF16),
                        pltpu.VMEM((2, ATT_HEADS, blk, blk), F32),
                        pltpu.VMEM((gw, blk), F32), pltpu.VMEM((ATT_HEADS, 1, blk), F32)],
    )
    return pl.pallas_call(
        _attn_prompt_kernel,
        grid_spec=grid_spec,
        out_shape=jax.ShapeDtypeStruct((b, nk, SB_DIM, blk), F32),
        compiler_params=_cparams("parallel", "parallel"),
        name="attn_prompt",
    )(sb_bias, tile_q, tile_j, q_t, k3, v_t)


SAMPLE_SEQS = 2


def _attn_sample_kernel(pt_ref, q_ref, bias_ref, kn_ref, vn_ref, *refs, tlen, n_pages):
    n_seqs = q_ref.shape[0]
    o_ref = refs[2 * n_seqs * n_pages]
    for s in range(n_seqs):
        _attn_sample_sequence(q_ref.at[s], bias_ref, kn_ref.at[s], vn_ref.at[s],
                              refs[s * n_pages:(s + 1) * n_pages],
                              refs[(n_seqs + s) * n_pages:(n_seqs + s + 1) * n_pages], o_ref.at[s], tlen)


def _attn_sample_sequence(q_ref, bias_ref, kn_ref, vn_ref, k_refs, v_refs, o_ref, tlen):
    n_pages = len(k_refs)
    rows = SB_HEADS * tlen
    rhead = lax.broadcasted_iota(jnp.int32, (rows, SB_DIM), 0) // tlen
    lhead = lax.broadcasted_iota(jnp.int32, (rows, SB_DIM), 1) // HEAD_DIM
    head_mask = rhead == lhead
    q_s = jnp.where(head_mask, jnp.concatenate([q_ref[...]] * SB_HEADS, axis=0), 0.0).astype(BF16)
    jrow = lax.broadcasted_iota(jnp.int32, (PAGE_SIZE, 2 * PAGE_SIZE), 0)
    jcol = lax.broadcasted_iota(jnp.int32, (PAGE_SIZE, 2 * PAGE_SIZE), 1)
    tri_ones = jnp.where((jrow > jcol) | (jcol >= PAGE_SIZE), 1.0, 0.0).astype(BF16)
    bias = bias_ref[...]

    def scores(z, mask):
        z = z + bias
        sp = jnp.maximum(z, 0.0) + jnp.log(1.0 + jnp.exp(-jnp.abs(z)))
        lk = -sp
        if mask is not None:
            lk = jnp.where(mask, lk, 0.0)
        lk_hi, lk_lo = _split2(lk)
        sums = (jnp.dot(lk_hi, tri_ones, preferred_element_type=F32)
                + jnp.dot(lk_lo, tri_ones, preferred_element_type=F32))
        return (z - sp) + sums[:, 0:PAGE_SIZE], sums[:, PAGE_SIZE:]

    pad = jnp.zeros((PAGE_SIZE - tlen, SB_DIM), F32)
    k_new = jnp.concatenate([kn_ref[...], pad], axis=0)
    v_new = jnp.concatenate([vn_ref[...], pad], axis=0)
    new_mask = (lax.broadcasted_iota(jnp.int32, (rows, PAGE_SIZE), 1)
                < lax.broadcasted_iota(jnp.int32, (rows, PAGE_SIZE), 0) % tlen)
    order = list(reversed(range(n_pages)))
    zs = [_dg(q_s, k_new, 1, 1)] + [_dot(q_s, k_refs[j][0, 0]) for j in order]
    parts = [scores(zs[0], new_mask)] + [scores(z, None) for z in zs[1:]]
    carry = jnp.zeros((rows, PAGE_SIZE), F32)
    ws = []
    for n, (logw, total) in enumerate(parts):
        w = jnp.exp(logw + carry)
        ws.append(jnp.where(new_mask, w, 0.0) if n == 0 else w)
        carry = carry + total
    acc = _dot(ws[0], v_new)
    for w, j in zip(ws[1:], order):
        acc = acc + _dg(w, v_refs[j][0, 0], 1, 1)
    acc = jnp.where(head_mask, acc, 0.0)
    out = acc[0:tlen]
    for h in range(1, SB_HEADS):
        out = out + acc[h * tlen:(h + 1) * tlen]
    o_ref[...] = out


def _attn_sample(q, k_new, v_new, cache_kt, cache_vt, layer, page_table, sb_bias, b, tlen):
    n_pages = page_table.shape[1]
    rows = SB_HEADS * tlen
    n_seqs = SAMPLE_SEQS if b % SAMPLE_SEQS == 0 else 1
    bias_rows = jnp.broadcast_to(jnp.repeat(sb_bias, tlen)[:, None], (rows, PAGE_SIZE)).astype(F32)
    seq = lambda a: a.reshape(b, tlen, SB_DIM)
    tok = pl.BlockSpec((n_seqs, tlen, SB_DIM), lambda i, pt: (i, 0, 0))
    page = lambda s, j: pl.BlockSpec((1, 1, SB_DIM, PAGE_SIZE),
                                     lambda i, pt, s=s, j=j: (layer, pt[i * n_seqs + s, j], 0, 0))
    pages = [page(s, j) for s in range(n_seqs) for j in range(n_pages)]
    grid_spec = pltpu.PrefetchScalarGridSpec(
        num_scalar_prefetch=1,
        grid=(b // n_seqs,),
        in_specs=[tok, pl.BlockSpec((rows, PAGE_SIZE), lambda i, pt: (0, 0)), tok, tok] + pages + pages,
        out_specs=tok,
    )
    out = pl.pallas_call(
        functools.partial(_attn_sample_kernel, tlen=tlen, n_pages=n_pages),
        grid_spec=grid_spec,
        out_shape=jax.ShapeDtypeStruct((b, tlen, SB_DIM), F32),
        compiler_params=_cparams("parallel"),
        name="attn_sample",
    )(page_table, seq(q), bias_rows, seq(k_new), seq(v_new),
      *([cache_kt] * len(pages)), *([cache_vt] * len(pages)))
    return out.reshape(b * tlen, SB_DIM)


def _tail_kernel(x_ref, y_ref, bonus_ref, g_ref, cb_ref, cc_ref, cx_ref, *refs, tiles_per_seq):
    n_before = 4 if tiles_per_seq is None else 3
    before_refs = refs[:n_before]
    (osb_ref, p_ref, gnw_ref, gnb_ref, convw_ref, ones_ref, wout_ref, gmlp_ref, wup_ref, wdown_ref,
     gple_ref, wgate_ref, wproj_ref, o_ref, x1_ref, h2_ref, acc_ref) = refs[n_before:]
    f = pl.program_id(1)
    nf = pl.num_programs(1)

    @pl.when(f == 0)
    def _():
        y = y_ref[...]
        mu = _dot_exact_rhs(y, ones_ref[...])
        yc = y - mu
        var = _dot_exact_rhs(yc * yc, ones_ref[...])
        yn = yc * lax.rsqrt(var + GN_EPS) * gnw_ref[...] + gnb_ref[...]
        o_rw = (yn + bonus_ref[...]) * g_ref[...]
        cw = convw_ref[...]
        u = cc_ref[...] * cx_ref[...]
        if tiles_per_seq is None:
            cc1_ref, cx1_ref, cc2_ref, cx2_ref = before_refs
            u1 = cc1_ref[...] * cx1_ref[...]
            u2 = cc2_ref[...] * cx2_ref[...]
        else:
            cch_ref, cxh_ref, state_ref = before_refs
            u_halo = cch_ref[...] * cxh_ref[...]
            is_first = pl.program_id(0) % tiles_per_seq == 0
            u1 = _rows_before(u, u_halo, state_ref[0], is_first, 1)
            u2 = _rows_before(u, u_halo, state_ref[0], is_first, 2)
        o_cv = cb_ref[...] * (cw[0:1] * u2 + cw[1:2] * u1 + cw[2:3] * u)
        mix = _dot(o_rw, wout_ref[0:RW_DIM, :])
        mix = mix + _dot(o_cv, wout_ref[RW_DIM:RW_DIM + CONV_DIM, :])
        w_sb = wout_ref[RW_DIM + CONV_DIM:, :]
        if len(osb_ref.shape) == 2:
            mix = mix + _dot(osb_ref[...], w_sb)
        else:
            mix = mix + jnp.concatenate(
                [_dot(osb_ref[0, s].T, w_sb) for s in range(osb_ref.shape[1])], axis=0)
        x1 = x_ref[...] + mix
        x1_ref[...] = x1
        h2_ref[...] = _rmsnorm_rows(x1, gmlp_ref[...]).astype(BF16)
        acc_ref[...] = jnp.zeros_like(acc_ref)

    up = jnp.dot(h2_ref[...], wup_ref[...], preferred_element_type=F32)
    act = jnp.square(jnp.maximum(up, 0.0))
    acc_ref[...] += _dot(act, wdown_ref[...])

    @pl.when(f == nf - 1)
    def _():
        x2 = x1_ref[...] + acc_ref[...]
        gate = _sigmoid(_dot(_rmsnorm_rows(x2, gple_ref[...]), wgate_ref[...]))
        o_ref[...] = x2 + gate * _dot(p_ref[...], wproj_ref[...])


def _tail(x2d, y, bonus, g, p_cv, conv_prev, seq_len, o_sb, p3d, layer, lw, wts, tm, tf):
    n, d = x2d.shape
    b = n // seq_len
    d_ff = wts["w_up"].shape[1]
    ones = _head_block_ones(RW_DIM, 1.0 / HEAD_DIM)
    vec = lambda a: a.reshape(1, -1)
    row = lambda w, col=0: pl.BlockSpec((tm, w), lambda i, f, col=col: (i, col))
    full = lambda a: pl.BlockSpec(a.shape, lambda i, f: (0,) * a.ndim)
    gnw, gnb, gmlp, gple = vec(lw["gn_w"]), vec(lw["gn_b"]), vec(lw["g_mlp"]), vec(lw["g_ple"])
    if o_sb.ndim == 2:
        osb_spec = row(SB_DIM)
    else:
        _, nblk, _, blk = o_sb.shape
        osb_tiles = nblk * blk // tm
        osb_spec = pl.BlockSpec((1, tm // blk, SB_DIM, blk), lambda i, f: (i // osb_tiles, i % osb_tiles, 0, 0))
    if seq_len % tm == 0:
        tiles_per_seq = seq_len // tm
        halo = lambda col: pl.BlockSpec((8, CONV_DIM), lambda i, f: (jnp.maximum(i * (tm // 8) - 1, 0), col))
        before = [p_cv, p_cv, conv_prev]
        before_specs = [halo(1), halo(2),
                        pl.BlockSpec((1, CONV_K - 1, CONV_DIM), lambda i, f: (i // tiles_per_seq, 0, 0))]
    else:
        tiles_per_seq = None
        p_cv3 = p_cv.reshape(b, seq_len, 3 * CONV_DIM)
        state_rows = lambda s: jnp.concatenate([jnp.zeros_like(s), s, jnp.ones_like(s)], axis=-1)
        p_cv1 = _shift_rows(p_cv3, state_rows(conv_prev[:, 1:2]), 1).reshape(n, 3 * CONV_DIM)
        p_cv2 = _shift_rows(p_cv3, state_rows(conv_prev), 2).reshape(n, 3 * CONV_DIM)
        before = [p_cv1, p_cv1, p_cv2, p_cv2]
        before_specs = [row(CONV_DIM, 1), row(CONV_DIM, 2), row(CONV_DIM, 1), row(CONV_DIM, 2)]
    in_specs = [row(d), row(RW_DIM), row(RW_DIM), row(RW_DIM),
                row(CONV_DIM, 0), row(CONV_DIM, 1), row(CONV_DIM, 2)] + before_specs + [
                osb_spec, pl.BlockSpec((None, tm, p3d.shape[2]), lambda i, f: (layer, i, 0)),
                full(gnw), full(gnb), full(lw["conv_w"]), full(ones), full(wts["w_out"]), full(gmlp),
                pl.BlockSpec((d, tf), lambda i, f: (0, f)), pl.BlockSpec((tf, d), lambda i, f: (f, 0)),
                full(gple), full(wts["w_ple_gate"]), full(wts["w_ple_proj"])]
    return pl.pallas_call(
        functools.partial(_tail_kernel, tiles_per_seq=tiles_per_seq),
        grid=(n // tm, d_ff // tf),
        in_specs=in_specs,
        out_specs=pl.BlockSpec((tm, d), lambda i, f: (i, 0)),
        out_shape=jax.ShapeDtypeStruct((n, d), F32),
        scratch_shapes=[pltpu.VMEM((tm, d), F32), pltpu.VMEM((tm, d), BF16), pltpu.VMEM((tm, d), F32)],
        compiler_params=_cparams("parallel", "arbitrary"),
        name="tail",
    )(x2d, y, bonus, g, p_cv, p_cv, p_cv, *before, o_sb, p3d,
      gnw, gnb, lw["conv_w"], ones, wts["w_out"], gmlp, wts["w_up"], wts["w_down"],
      gple, wts["w_ple_gate"], wts["w_ple_proj"])


def _conv_state_kernel(cv_ref, o_ref):
    o_ref[...] = cv_ref[:, :, CONV_DIM:2 * CONV_DIM] * cv_ref[:, :, 2 * CONV_DIM:3 * CONV_DIM]


def _conv_state(p_cv3):
    b, t, c = p_cv3.shape
    return pl.pallas_call(
        _conv_state_kernel,
        grid=(1,),
        in_specs=[pl.BlockSpec((b, 8, c), lambda i: (0, t // 8 - 1, 0))],
        out_specs=pl.BlockSpec((b, 8, CONV_DIM), lambda i: (0, 0, 0)),
        out_shape=jax.ShapeDtypeStruct((b, 8, CONV_DIM), F32),
        name="conv_state",
    )(p_cv3)


def _shift_rows(x3, first_rows, n):
    return jnp.concatenate([first_rows, x3[:, :-n]], axis=1)


def _layer(x2d, p3d, layer, b, t, shift_prev, wkv_prev, conv_prev, kv_prev, attn_fn, att_blk, lw, wts, tm, tf,
           wkv_nseq, wkv_tlen, wkv_groups):
    n, d = x2d.shape
    p_rw, p_cv, k_all, v_all, r, lwd, k2, v, av, bv, g, bonus, *attn_ops = _in_proj(
        x2d, shift_prev, lw, wts["w_in"], tm, t, att_blk, layer, p3d.shape[0], kv_prev)
    k_new, v_new = k_all[layer], v_all[layer]
    p_rw3 = p_rw.reshape(b, t, p_rw.shape[1])
    grp = lambda a: a.reshape(-1, (n // (b // wkv_nseq)), RW_DIM) if wkv_nseq > 1 else a.reshape(b, t, RW_DIM)
    y, wkv_new = _wkv(grp(r), grp(lwd), grp(k2), grp(v), grp(av), grp(bv), wkv_prev, wkv_nseq, wkv_tlen,
                   wkv_groups)
    y = y.reshape(n, RW_DIM)

    o_sb = attn_fn(k_new, v_new, *attn_ops)

    x_out = _tail(x2d, y, bonus, g, p_cv, conv_prev, t, o_sb, p3d, layer, lw, wts, tm, tf)
    conv_new = _conv_state(p_cv.reshape(b, t, 3 * CONV_DIM))[:, 8 - (CONV_K - 1):]
    return x_out, (k_all, v_all), wkv_new, p_rw3[:, -1], conv_new


def _run_trunk(x, p, shift, wkv, conv, attn_builder, att_blk, params, bf16_w, tm, tf,
               wkv_nseq, wkv_tlen, wkv_groups):
    b, t, d = x.shape
    depth = p.shape[0]
    x2d = x.reshape(b * t, d)
    wkvs, shifts, convs = [], [], []
    kv_all = None
    for l in range(depth):
        lw = {name: arr[l] for name, arr in params.items()}
        wts = {name: arr[l] for name, arr in bf16_w.items()}
        x2d, kv_all, wkv_new, shift_new, conv_new = _layer(
            x2d, p.reshape(depth, b * t, -1), l, b, t, shift[l], wkv[l].reshape(b, RW_DIM, HEAD_DIM), conv[l],
            kv_all, attn_builder(l, lw), att_blk, lw, wts, tm, tf, wkv_nseq, wkv_tlen, wkv_groups)
        wkvs.append(wkv_new.reshape(b, RW_HEADS, HEAD_DIM, HEAD_DIM))
        shifts.append(shift_new)
        convs.append(conv_new)
    heads = lambda a: a.reshape(depth, b, t, SB_HEADS, HEAD_DIM)
    return (x2d.reshape(b, t, d), heads(kv_all[0]), heads(kv_all[1]), jnp.stack(wkvs), jnp.stack(shifts),
            jnp.stack(convs))


def kernel(x_prompt, x_sample, cache_k, cache_v, state_wkv, state_shift, state_conv, page_table,
           p_prompt, p_sample, g_mix, w_in, mu_shift, w0, w2, a0, a2, g2, k_k, k_a, r_k,
           gn_w, gn_b, conv_w, q_gain, k_gain, sb_bias, w_out, g_mlp, w_up, w_down, g_ple,
           w_ple_gate, w_ple_proj):
    params = dict(g_mix=g_mix, mu_shift=mu_shift, w0=w0, w2=w2, a0=a0, a2=a2, g2=g2,
                  k_k=k_k, k_a=k_a, r_k=r_k, gn_w=gn_w, gn_b=gn_b, conv_w=conv_w,
                  q_gain=q_gain, k_gain=k_gain, sb_bias=sb_bias, g_mlp=g_mlp, g_ple=g_ple)
    bf16_w = dict(w_in=w_in.astype(BF16), w_out=w_out.astype(BF16), w_up=w_up.astype(BF16),
                  w_down=w_down.astype(BF16), w_ple_gate=w_ple_gate.astype(BF16),
                  w_ple_proj=w_ple_proj.astype(BF16))
    depth = w_in.shape[0]
    b, t, d = x_prompt.shape
    db, dt, _ = x_sample.shape
    rw_proj = state_shift.shape[-1]
    d_ff = w_up.shape[-1]
    tf = min(2048, d_ff)

    n_phys = cache_k.shape[1]
    page_view = lambda c: c.transpose(0, 1, 3, 4, 2).reshape(depth, n_phys, SB_DIM, PAGE_SIZE)
    cache_kt, cache_vt = page_view(cache_k), page_view(cache_v)

    def prompt_attn(l, lw):
        return lambda k, v, q_t, k_bf16, v_t: _attn_prompt(q_t, k_bf16, v_t, lw["sb_bias"], b, t)

    def sample_attn(l, lw):
        return lambda k, v, q: _attn_sample(q, k, v, cache_kt, cache_vt, l, page_table, lw["sb_bias"], db, dt)

    dtp = x_prompt.dtype
    shift0 = jnp.zeros((depth, b, rw_proj), dtp)
    wkv0 = jnp.zeros((depth, b, RW_HEADS, HEAD_DIM, HEAD_DIM), dtp)
    conv0 = jnp.zeros((depth, b, CONV_K - 1, CONV_DIM), dtp)
    (y_prompt, k_prompt, v_prompt, wkv_prompt, shift_prompt, conv_prompt) = _run_trunk(
        x_prompt, p_prompt, shift0, wkv0, conv0, prompt_attn, min(ATT_BLOCK, t), params, bf16_w,
        min(512, t), tf, 1, WKV_ROWS, b)
    seqs_per_group = WKV_ROWS // dt
    (y_sample, k_sample, v_sample, wkv_sample, shift_sample, conv_sample) = _run_trunk(
        x_sample, p_sample, state_shift, state_wkv, state_conv, sample_attn, None, params, bf16_w,
        min(512, db * dt), tf, seqs_per_group, dt, min(2, db // seqs_per_group))
    return (y_prompt, y_sample, k_prompt, v_prompt, wkv_prompt, shift_prompt, conv_prompt,
            k_sample, v_sample, wkv_sample, shift_sample, conv_sample)
```

```python
import functools

import jax
import jax.numpy as jnp
from jax import lax
from jax.experimental import pallas as pl
from jax.experimental.pallas import tpu as pltpu

F32 = jnp.float32
BF16 = jnp.bfloat16

HEAD_DIM = 64
RW_HEADS = 4
RW_DIM = RW_HEADS * HEAD_DIM
LORA_WA = 128
CONV_DIM = 256
CONV_K = 3
SB_HEADS = 8
SB_DIM = SB_HEADS * HEAD_DIM
PAGE_SIZE = 128
RMS_EPS = 1e-6
GN_EPS = 64e-5
KK_EPS = 1e-12
LOG2E = 1.4426950408889634

WKV_ROWS = 64
ATT_BLOCK = 256
ATT_HEADS = 4
MLP_CHUNK = 1024
VMEM_LIMIT_BYTES = 56 * 1024 * 1024


def _cparams(*sem):
    return pltpu.CompilerParams(dimension_semantics=sem, vmem_limit_bytes=VMEM_LIMIT_BYTES)


def _dot(a, b):
    return jnp.dot(a.astype(BF16), b.astype(BF16), preferred_element_type=F32)


def _dg(a, b, ca, cb):
    return lax.dot_general(a.astype(BF16), b.astype(BF16), (((ca,), (cb,)), ((), ())),
                           preferred_element_type=F32)


def _split2(a):
    hi = a.astype(BF16)
    lo = (a - hi.astype(F32)).astype(BF16)
    return hi, lo


def _dot_exact_rhs(a, b_bf16):
    h1 = a.astype(BF16)
    r1 = a - h1.astype(F32)
    h2 = r1.astype(BF16)
    h3 = (r1 - h2.astype(F32)).astype(BF16)
    out = jnp.dot(h1, b_bf16, preferred_element_type=F32)
    out = out + jnp.dot(h2, b_bf16, preferred_element_type=F32)
    out = out + jnp.dot(h3, b_bf16, preferred_element_type=F32)
    return out


def _dot_exact_lhs(a_bf16, b):
    h1 = b.astype(BF16)
    r1 = b - h1.astype(F32)
    h2 = r1.astype(BF16)
    h3 = (r1 - h2.astype(F32)).astype(BF16)
    out = jnp.dot(a_bf16, h1, preferred_element_type=F32)
    out = out + jnp.dot(a_bf16, h2, preferred_element_type=F32)
    out = out + jnp.dot(a_bf16, h3, preferred_element_type=F32)
    return out


def _softplus(x):
    return jnp.maximum(x, 0.0) + jnp.log1p(jnp.exp(-jnp.abs(x)))


def _sigmoid(x):
    return 1.0 / (1.0 + jnp.exp(-x))


def _rmsnorm_rows(x, g):
    ms = jnp.mean(x * x, axis=-1, keepdims=True)
    return x * lax.rsqrt(ms + RMS_EPS) * g


def _head_block_ones(width, scale):
    i = jnp.arange(width) // HEAD_DIM
    return jnp.where(i[:, None] == i[None, :], scale, 0.0).astype(BF16)


def _rows_before(cur, halo, first_rows, is_first, n):
    row = lax.broadcasted_iota(jnp.int32, cur.shape, 0)
    out = pltpu.roll(cur, n, 0)
    n_halo, n_first = halo.shape[0], first_rows.shape[0]
    for k in range(n):
        before = jnp.where(is_first, first_rows[n_first - n + k:n_first - n + k + 1],
                           halo[n_halo - n + k:n_halo - n + k + 1])
        out = jnp.where(row == k, before, out)
    return out


def _rwkv_heads(p, prev, mu_ref, w0_ref, a0_ref, kk_ref, ka_ref, rk_ref, wwa_ref, g2_ref, ones_ref,
                r_ref, lw_ref, k_ref, v_ref, av_ref, bv_ref, g_ref, bonus_ref):
    xs = p + (prev - p) * mu_ref[...]
    r = xs[:, 0:RW_DIM]
    k = xs[:, RW_DIM:2 * RW_DIM]
    v = xs[:, 2 * RW_DIM:3 * RW_DIM]
    d = xs[:, 3 * RW_DIM:3 * RW_DIM + LORA_WA]
    dg = xs[:, 3 * RW_DIM + LORA_WA:]
    lane = lax.broadcasted_iota(jnp.int32, d.shape, 1)
    dwa = jnp.where(lane < LORA_WA // 2, jnp.tanh(d), d)
    wa = _dot(dwa, wwa_ref[...])
    w_log = -_softplus(-(w0_ref[...] + wa[:, 0:RW_DIM])) - 0.5
    lw_ref[...] = -jnp.exp(w_log)
    a = _sigmoid(a0_ref[...] + wa[:, RW_DIM:])
    g_ref[...] = _dot(_sigmoid(dg), g2_ref[...])
    kk = k * kk_ref[...]
    ss = _dot_exact_rhs(kk * kk, ones_ref[...])
    kk = kk * lax.rsqrt(ss + KK_EPS)
    k2 = k * (1.0 + (a - 1.0) * ka_ref[...])
    r_ref[...] = r
    k_ref[...] = k2
    v_ref[...] = v
    av_ref[...] = -kk
    bv_ref[...] = kk * a
    bonus_ref[...] = _dot_exact_rhs(r * k2 * rk_ref[...], ones_ref[...]) * v


N_RWKV_PARAMS = 9
N_RWKV_OUTS = 8


def _in_proj_kernel(x_ref, g_ref, w_ref, ebd_ref, qg_ref, kg_ref, state_ref, *refs,
                    att_blk, seq_len, tiles_per_seq, layer):
    rwkv_params = refs[:N_RWKV_PARAMS]
    refs = refs[N_RWKV_PARAMS:]
    kv_prev = ()
    if layer > 0:
        kv_prev, refs = refs[:2], refs[2:]
    rw_ref, cv_ref, k_ref, v_ref = refs[:4]
    rwkv_outs = refs[4:4 + N_RWKV_OUTS]
    attn_refs = refs[4 + N_RWKV_OUTS:]
    for j in range(k_ref.shape[0]):
        if j != layer:
            k_ref[j] = kv_prev[0][j] if kv_prev else jnp.zeros(k_ref.shape[1:], F32)
            v_ref[j] = kv_prev[1][j] if kv_prev else jnp.zeros(v_ref.shape[1:], F32)
    h = _rmsnorm_rows(x_ref[...], g_ref[...]).astype(BF16)
    c0 = rw_ref.shape[1]
    c1 = c0 + cv_ref.shape[1]
    p = jnp.dot(h, w_ref[:, 0:c0], preferred_element_type=F32)
    rw_ref[...] = p
    rows = p.shape[0]
    if tiles_per_seq is None:
        nseq = rows // seq_len
        r_i = lax.broadcasted_iota(jnp.int32, (rows, nseq), 0)
        s_i = lax.broadcasted_iota(jnp.int32, (rows, nseq), 1)
        starts = jnp.where(r_i == s_i * seq_len, 1.0, 0.0).astype(BF16)
        t_local = lax.broadcasted_iota(jnp.int32, p.shape, 0) % seq_len
        prev = jnp.where(t_local == 0, _dot_exact_lhs(starts, state_ref[...]), pltpu.roll(p, 1, 0))
    else:
        attn_refs, last_ref = attn_refs[:-1], attn_refs[-1]

        @pl.when(pl.program_id(0) == 0)
        def _():
            last_ref[...] = jnp.zeros_like(last_ref)

        prev = _rows_before(p, last_ref[...], state_ref[0], pl.program_id(0) % tiles_per_seq == 0, 1)
        last_ref[...] = p[rows - last_ref.shape[0]:rows]
    _rwkv_heads(p, prev, *rwkv_params, *rwkv_outs)
    cv_ref[...] = jnp.dot(h, w_ref[:, c0:c1], preferred_element_type=F32)

    def head_norm(t, gain):
        ms = jnp.dot((t * t).astype(BF16), ebd_ref[...], preferred_element_type=F32)
        return t * lax.rsqrt(ms + RMS_EPS) * gain

    q = jnp.dot(h, w_ref[:, c1:c1 + SB_DIM], preferred_element_type=F32)
    q = head_norm(q, qg_ref[...]) * (HEAD_DIM ** -0.5)
    k = jnp.dot(h, w_ref[:, c1 + SB_DIM:c1 + 2 * SB_DIM], preferred_element_type=F32)
    k = head_norm(k, kg_ref[...])
    k_ref[layer] = k
    v = jnp.dot(h, w_ref[:, c1 + 2 * SB_DIM:c1 + 3 * SB_DIM], preferred_element_type=F32)
    v_ref[layer] = v
    if att_blk is None:
        (q_ref,) = attn_refs
        q_ref[...] = q
    else:
        qt_ref, kb_ref, vt_ref = attn_refs
        kb_ref[...] = k.astype(BF16)
        q_t = (q * LOG2E).T
        v_t = v.T
        for s in range(qt_ref.shape[1]):
            qt_ref[0, s] = q_t[:, s * att_blk:(s + 1) * att_blk].astype(BF16)
            vt_ref[0, s] = v_t[:, s * att_blk:(s + 1) * att_blk].astype(BF16)


def _in_proj(x2d, shift_prev, lw, w_in_bf16, tm, seq_len, att_blk, layer, depth, kv_prev):
    n, d = x2d.shape
    cols = w_in_bf16.shape[1]
    rw_cols = cols - 3 * CONV_DIM - 3 * SB_DIM
    ebd = _head_block_ones(SB_DIM, 1.0 / HEAD_DIM)
    qg = jnp.tile(lw["q_gain"], SB_HEADS)[None, :]
    kg = jnp.tile(lw["k_gain"], SB_HEADS)[None, :]
    row = lambda w: pl.BlockSpec((tm, w), lambda i: (i, 0))
    full = lambda a: pl.BlockSpec(a.shape, lambda i: (0,) * a.ndim)
    g2d = lw["g_mix"][None, :]
    w_lora = lw["w2"].shape[0]
    a_lora = lw["a2"].shape[0]
    wwa = jnp.zeros((w_lora + a_lora, 2 * RW_DIM), F32)
    wwa = wwa.at[:w_lora, :RW_DIM].set(lw["w2"]).at[w_lora:, RW_DIM:].set(lw["a2"]).astype(BF16)
    vec = lambda a: a.reshape(1, -1)
    rwkv_params = [vec(lw["mu_shift"]), vec(lw["w0"]), vec(lw["a0"]), vec(lw["k_k"]), vec(lw["k_a"]),
                   vec(lw["r_k"]), wwa, lw["g2"].astype(BF16), _head_block_ones(RW_DIM, 1.0)]
    assert len(rwkv_params) == N_RWKV_PARAMS
    scratch = []
    if seq_len % tm == 0:
        tiles_per_seq = seq_len // tm
        state, state_spec = shift_prev[:, None, :], pl.BlockSpec((1, 1, rw_cols), lambda i: (i // tiles_per_seq, 0, 0))
        scratch = [pltpu.VMEM((8, rw_cols), F32)]
    else:
        tiles_per_seq = None
        state, state_spec = shift_prev, pl.BlockSpec((tm // seq_len, rw_cols), lambda i: (i, 0))
    kv_spec = pl.BlockSpec((depth, tm, SB_DIM), lambda i: (0, i, 0))
    kv_shape = jax.ShapeDtypeStruct((depth, n, SB_DIM), F32)
    out_specs = [row(rw_cols), row(3 * CONV_DIM), kv_spec, kv_spec] + [row(RW_DIM)] * N_RWKV_OUTS
    out_shape = [jax.ShapeDtypeStruct((n, rw_cols), F32), jax.ShapeDtypeStruct((n, 3 * CONV_DIM), F32),
                 kv_shape, kv_shape]
    out_shape += [jax.ShapeDtypeStruct((n, RW_DIM), F32)] * N_RWKV_OUTS
    inputs = [x2d, g2d, w_in_bf16, ebd, qg, kg, state, *rwkv_params]
    in_specs = [row(d), full(g2d), full(w_in_bf16), full(ebd), full(qg), full(kg), state_spec]
    in_specs += [full(a) for a in rwkv_params]
    aliases = {}
    if layer > 0:
        aliases = {len(inputs): 2, len(inputs) + 1: 3}
        inputs += list(kv_prev)
        in_specs += [kv_spec, kv_spec]
    if att_blk is None:
        out_specs.append(row(SB_DIM))
        out_shape.append(jax.ShapeDtypeStruct((n, SB_DIM), F32))
    else:
        att_tiles = seq_len // tm
        blks = tm // att_blk
        t_spec = pl.BlockSpec((1, blks, SB_DIM, att_blk), lambda i: (i // att_tiles, i % att_tiles, 0, 0))
        t_shape = jax.ShapeDtypeStruct((n // seq_len, seq_len // att_blk, SB_DIM, att_blk), BF16)
        out_specs += [t_spec, row(SB_DIM), t_spec]
        out_shape += [t_shape, jax.ShapeDtypeStruct((n, SB_DIM), BF16), t_shape]
    return pl.pallas_call(
        functools.partial(_in_proj_kernel, att_blk=att_blk, seq_len=seq_len, tiles_per_seq=tiles_per_seq,
                          layer=layer),
        grid=(n // tm,),
        in_specs=in_specs,
        out_specs=out_specs,
        out_shape=out_shape,
        scratch_shapes=scratch,
        input_output_aliases=aliases,
        compiler_params=_cparams("arbitrary"),
        name="in_proj",
    )(*inputs)


def _wkv_kernel(r_ref, lw_ref, k_ref, v_ref, a_ref, b_ref, s0_ref, y_ref, sout_ref, s_ref, *, nseq, tlen):
    P = WKV_ROWS
    HP = RW_HEADS * P
    c = pl.program_id(1)
    own_head = (lax.broadcasted_iota(jnp.int32, (RW_DIM, RW_DIM), 0) // HEAD_DIM
                == lax.broadcasted_iota(jnp.int32, (RW_DIM, RW_DIM), 1) // HEAD_DIM)

    @pl.when(c == 0)
    def _():
        for j in range(s_ref.shape[0]):
            s_ref[j] = jnp.where(own_head, jnp.concatenate([s0_ref[j]] * RW_HEADS, axis=1), 0.0)

    prow = lax.broadcasted_iota(jnp.int32, (P, P), 0)
    pcol = lax.broadcasted_iota(jnp.int32, (P, P), 1)
    same_seq = (prow // tlen) == (pcol // tlen)
    tri_incl = jnp.where(same_seq & (pcol <= prow), 1.0, 0.0).astype(BF16)
    last_sel = jnp.where(same_seq & (pcol == (prow // tlen) * tlen + (tlen - 1)), 1.0, 0.0).astype(BF16)
    srow = lax.broadcasted_iota(jnp.int32, (HP, HP), 0)
    scol = lax.broadcasted_iota(jnp.int32, (HP, HP), 1)
    same_blk = ((srow // tlen) == (scol // tlen))
    m_strict = same_blk & (scol < srow)
    m_incl = same_blk & (scol <= srow)
    eye = jnp.where(srow == scol, 1.0, 0.0)
    hrow = lax.broadcasted_iota(jnp.int32, (HP, RW_DIM), 0)
    head_mask = (hrow // P) == (lax.broadcasted_iota(jnp.int32, (HP, RW_DIM), 1) // HEAD_DIM)
    seq_of_row = (hrow % P) // tlen

    def stack(x):
        return jnp.where(head_mask, jnp.concatenate([x] * RW_HEADS, axis=0), 0.0)

    groups = range(r_ref.shape[0])
    lw = [lw_ref[g] for g in groups]
    cw = [_dot_exact_lhs(tri_incl, x) for x in lw]
    cw_end = [_dot_exact_lhs(last_sel, x) for x in cw]
    w_inv = [jnp.exp(-x) for x in cw]
    w_toend = [jnp.exp(e - x) for e, x in zip(cw_end, cw)]
    a_s = [stack(a_ref[g] * jnp.exp(cw[g] - lw[g])) for g in groups]
    r_s = [stack(r_ref[g] * jnp.exp(cw[g])) for g in groups]
    b_s = [stack(b_ref[g] * w_inv[g]) for g in groups]
    k_s = [stack(k_ref[g] * w_inv[g]) for g in groups]
    v_s = [stack(v_ref[g]) for g in groups]
    bend_s = [stack(b_ref[g] * w_toend[g]) for g in groups]
    kend_s = [stack(k_ref[g] * w_toend[g]) for g in groups]

    n_ab = [jnp.where(m_strict, _dg(a_s[g], b_s[g], 1, 1), 0.0) for g in groups]
    a_ak = [jnp.where(m_strict, _dg(a_s[g], k_s[g], 1, 1), 0.0) for g in groups]
    a_rb = [jnp.where(m_incl, _dg(r_s[g], b_s[g], 1, 1), 0.0) for g in groups]
    a_rk = [jnp.where(m_incl, _dg(r_s[g], k_s[g], 1, 1), 0.0) for g in groups]

    t_inv = [eye + x for x in n_ab]
    pw = n_ab
    span = 2
    while span < tlen:
        pw = [_dot(x, x) for x in pw]
        t_inv = [t + _dot(t, x) for t, x in zip(t_inv, pw)]
        span *= 2

    ap_s = [_dot(t_inv[g], a_s[g]) for g in groups]
    akv = [_dot(a_ak[g], v_s[g]) for g in groups]
    u1_s = [_dot(t_inv[g], akv[g]) for g in groups]
    y_s = [_dot(a_rb[g], u1_s[g]) + _dot(a_rk[g], v_s[g]) for g in groups]
    rp_s = [r_s[g] + _dot(a_rb[g], ap_s[g]) for g in groups]
    wc_row = [jnp.exp(x) for x in cw_end]

    for i in range(nseq):
        if nseq == 1:
            sel = lambda x: x
        else:
            sel = lambda x: jnp.where(seq_of_row == i, x, 0.0)
        s_i = [s_ref[g * nseq + i] for g in groups]
        u_i = [_dg(sel(ap_s[g]), s_i[g], 1, 1) + sel(u1_s[g]) for g in groups]
        y_s = [y_s[g] + _dg(sel(rp_s[g]), s_i[g], 1, 1) for g in groups]
        for g in groups:
            wc_i = wc_row[g][i * tlen:i * tlen + 1, :]
            s_ref[g * nseq + i] = (s_i[g] * wc_i + _dg(u_i[g], sel(bend_s[g]), 0, 0)
                                   + _dg(sel(v_s[g]), sel(kend_s[g]), 0, 0))
    for g in groups:
        y = y_s[g][0:P]
        for h in range(1, RW_HEADS):
            y = y + y_s[g][h * P:(h + 1) * P]
        y_ref[g] = y

    @pl.when(c == pl.num_programs(1) - 1)
    def _():
        for j in range(s_ref.shape[0]):
            own = jnp.where(own_head, s_ref[j], 0.0)
            folded = own[:, 0:HEAD_DIM]
            for h in range(1, RW_HEADS):
                folded = folded + own[:, h * HEAD_DIM:(h + 1) * HEAD_DIM]
            sout_ref[j] = folded


def _wkv(r, lw, k, v, av, bv, s0, nseq, tlen, groups):
    g, rows, _ = r.shape
    n_chunks = rows // WKV_ROWS
    blk = pl.BlockSpec((groups, WKV_ROWS, RW_DIM), lambda i, c: (i, c, 0))
    sblk = pl.BlockSpec((groups * nseq, RW_DIM, HEAD_DIM), lambda i, c: (i, 0, 0))
    return pl.pallas_call(
        functools.partial(_wkv_kernel, nseq=nseq, tlen=tlen),
        grid=(g // groups, n_chunks),
        in_specs=[blk] * 6 + [sblk],
        out_specs=[blk, sblk],
        out_shape=[jax.ShapeDtypeStruct(r.shape, F32), jax.ShapeDtypeStruct(s0.shape, F32)],
        scratch_shapes=[pltpu.VMEM((groups * nseq, RW_DIM, RW_DIM), F32)],
        compiler_params=_cparams("parallel", "arbitrary"),
        name="wkv",
    )(r, lw, k, v, av, bv, s0)


ATT_NEG = -1e30


def _attn_prompt_kernel(bias_ref, tq_ref, tj_ref, qt_ref, k_ref, vt_ref, o_ref,
                        z_ref, sp_ref, lb_ref, w_ref, zinit_ref, acc_ref, carry_ref):
    grp = pl.program_id(1)
    nk, gw, blk = qt_ref.shape[1:]
    n_tiles = nk * (nk + 1) // 2
    row_head = lax.broadcasted_iota(jnp.int32, (gw, blk), 0) // HEAD_DIM
    krow = lax.broadcasted_iota(jnp.int32, (blk + 8, blk), 0)
    kcol = lax.broadcasted_iota(jnp.int32, (blk + 8, blk), 1)
    neg_sums = jnp.where((kcol > krow) | (krow >= blk), -1.0, 0.0).astype(BF16)
    hidden = (lax.broadcasted_iota(jnp.int32, (blk, blk), 0) >= lax.broadcasted_iota(jnp.int32, (blk, blk), 1))
    for h in range(ATT_HEADS):
        bias2 = bias_ref[grp * ATT_HEADS + h] * LOG2E
        zinit_ref[0, h] = jnp.full((blk, blk), bias2, F32)
        zinit_ref[1, h] = jnp.where(hidden, ATT_NEG, bias2)
    for ref in (z_ref, sp_ref, lb_ref, w_ref, acc_ref, carry_ref):
        ref[...] = jnp.zeros_like(ref)

    def step(i, slot_new):
        slot_old = 1 - slot_new

        f4 = jnp.clip(i - 3, 0, n_tiles - 1)
        q4 = tq_ref[f4]
        j4 = tj_ref[f4]
        keep = jnp.where(j4 == q4, 0.0, 1.0)
        for h in range(ATT_HEADS):
            rows = slice(h * HEAD_DIM, (h + 1) * HEAD_DIM)
            pv = jnp.dot(vt_ref[0, j4, rows, :], w_ref[slot_old, h], preferred_element_type=F32)
            acc_ref[rows, :] = acc_ref[rows, :] * keep + pv
        o_ref[0, q4] = acc_ref[...]

        f3 = jnp.clip(i - 2, 0, n_tiles - 1)
        valid3 = (i >= 2) & (i <= n_tiles + 1)
        first3 = tj_ref[f3] == tq_ref[f3]
        null3 = jnp.where(valid3, 0.0, ATT_NEG)
        count3 = jnp.where(valid3, 1.0, 0.0)
        for h in range(ATT_HEADS):
            sums = jnp.dot(neg_sums, sp_ref[slot_old, h], preferred_element_type=F32)
            carry = jnp.where(first3, 0.0, carry_ref[h])
            w_ref[slot_new, h] = jnp.exp2(lb_ref[slot_old, h] + sums[0:blk] + (carry + null3)).astype(BF16)
            carry_ref[h] = carry + sums[blk:blk + 1] * count3

        for h in range(ATT_HEADS):
            z = z_ref[slot_old, h]
            sp = jnp.maximum(z, 0.0) + jnp.log(1.0 + jnp.exp2(-jnp.abs(z))) * LOG2E
            sp_ref[slot_new, h] = sp.astype(BF16)
            lb_ref[slot_new, h] = z - sp

        f1 = jnp.minimum(i, n_tiles - 1)
        q1 = tq_ref[f1]
        j1 = tj_ref[f1]
        diag1 = jnp.where(j1 == q1, 1, 0)
        k_blk = k_ref[0, pl.ds(pl.multiple_of(j1 * blk, blk), blk), :]
        q_grp = qt_ref[0, q1]
        for h in range(ATT_HEADS):
            q_h = jnp.where(row_head == h, q_grp, jnp.zeros_like(q_grp))
            z_ref[slot_new, h] = jnp.dot(k_blk, q_h, preferred_element_type=F32) + zinit_ref[diag1, h]

    def body(pair, _):
        step(2 * pair, 0)
        step(2 * pair + 1, 1)
        return 0

    lax.fori_loop(0, (n_tiles + 3 + 1) // 2, body, 0)


def _attn_prompt(q_t, k_bf16, v_t, sb_bias, b, t):
    _, nk, _, blk = q_t.shape
    gw = ATT_HEADS * HEAD_DIM
    k3 = k_bf16.reshape(b, t, SB_DIM)
    tiles = [(qi, j) for qi in range(nk) for j in range(qi, -1, -1)]
    tile_q = jnp.array([qi for qi, _ in tiles], jnp.int32)
    tile_j = jnp.array([j for _, j in tiles], jnp.int32)
    blocks = pl.BlockSpec((1, nk, gw, blk), lambda i, g, *_: (i, 0, g, 0))
    grid_spec = pltpu.PrefetchScalarGridSpec(
        num_scalar_prefetch=3,
        grid=(b, SB_HEADS // ATT_HEADS),
        in_specs=[blocks, pl.BlockSpec((1, t, gw), lambda i, g, *_: (i, 0, g)), blocks],
        out_specs=blocks,
        scratch_shapes=[pltpu.VMEM((2, ATT_HEADS, blk, blk), F32), pltpu.VMEM((2, ATT_HEADS, blk, blk), BF16),
                        pltpu.VMEM((2, ATT_HEADS, blk, blk), F32), pltpu.VMEM((2, ATT_HEADS, blk, blk), BF16),
                        pltpu.VMEM((2, ATT_HEADS, blk, blk), F32),
                        pltpu.VMEM((gw, blk), F32), pltpu.VMEM((ATT_HEADS, 1, blk), F32)],
    )
    return pl.pallas_call(
        _attn_prompt_kernel,
        grid_spec=grid_spec,
        out_shape=jax.ShapeDtypeStruct((b, nk, SB_DIM, blk), F32),
        compiler_params=_cparams("parallel", "parallel"),
        name="attn_prompt",
    )(sb_bias, tile_q, tile_j, q_t, k3, v_t)


SAMPLE_SEQS = 2


def _attn_sample_kernel(pt_ref, q_ref, bias_ref, kn_ref, vn_ref, *refs, tlen, n_pages):
    n_seqs = q_ref.shape[0]
    o_ref = refs[2 * n_seqs * n_pages]
    for s in range(n_seqs):
        _attn_sample_sequence(q_ref.at[s], bias_ref, kn_ref.at[s], vn_ref.at[s],
                              refs[s * n_pages:(s + 1) * n_pages],
                              refs[(n_seqs + s) * n_pages:(n_seqs + s + 1) * n_pages], o_ref.at[s], tlen)


def _attn_sample_sequence(q_ref, bias_ref, kn_ref, vn_ref, k_refs, v_refs, o_ref, tlen):
    n_pages = len(k_refs)
    rows = SB_HEADS * tlen
    rhead = lax.broadcasted_iota(jnp.int32, (rows, SB_DIM), 0) // tlen
    lhead = lax.broadcasted_iota(jnp.int32, (rows, SB_DIM), 1) // HEAD_DIM
    head_mask = rhead == lhead
    q_s = jnp.where(head_mask, jnp.concatenate([q_ref[...]] * SB_HEADS, axis=0), 0.0).astype(BF16)
    jrow = lax.broadcasted_iota(jnp.int32, (PAGE_SIZE, 2 * PAGE_SIZE), 0)
    jcol = lax.broadcasted_iota(jnp.int32, (PAGE_SIZE, 2 * PAGE_SIZE), 1)
    tri_ones = jnp.where((jrow > jcol) | (jcol >= PAGE_SIZE), 1.0, 0.0).astype(BF16)
    bias = bias_ref[...]

    def scores(z, mask):
        z = z + bias
        sp = jnp.maximum(z, 0.0) + jnp.log(1.0 + jnp.exp(-jnp.abs(z)))
        lk = -sp
        if mask is not None:
            lk = jnp.where(mask, lk, 0.0)
        lk_hi, lk_lo = _split2(lk)
        sums = (jnp.dot(lk_hi, tri_ones, preferred_element_type=F32)
                + jnp.dot(lk_lo, tri_ones, preferred_element_type=F32))
        return (z - sp) + sums[:, 0:PAGE_SIZE], sums[:, PAGE_SIZE:]

    pad = jnp.zeros((PAGE_SIZE - tlen, SB_DIM), F32)
    k_new = jnp.concatenate([kn_ref[...], pad], axis=0)
    v_new = jnp.concatenate([vn_ref[...], pad], axis=0)
    new_mask = (lax.broadcasted_iota(jnp.int32, (rows, PAGE_SIZE), 1)
                < lax.broadcasted_iota(jnp.int32, (rows, PAGE_SIZE), 0) % tlen)
    order = list(reversed(range(n_pages)))
    zs = [_dg(q_s, k_new, 1, 1)] + [_dot(q_s, k_refs[j][0, 0]) for j in order]
    parts = [scores(zs[0], new_mask)] + [scores(z, None) for z in zs[1:]]
    carry = jnp.zeros((rows, PAGE_SIZE), F32)
    ws = []
    for n, (logw, total) in enumerate(parts):
        w = jnp.exp(logw + carry)
        ws.append(jnp.where(new_mask, w, 0.0) if n == 0 else w)
        carry = carry + total
    acc = _dot(ws[0], v_new)
    for w, j in zip(ws[1:], order):
        acc = acc + _dg(w, v_refs[j][0, 0], 1, 1)
    acc = jnp.where(head_mask, acc, 0.0)
    out = acc[0:tlen]
    for h in range(1, SB_HEADS):
        out = out + acc[h * tlen:(h + 1) * tlen]
    o_ref[...] = out


def _attn_sample(q, k_new, v_new, cache_kt, cache_vt, layer, page_table, sb_bias, b, tlen):
    n_pages = page_table.shape[1]
    rows = SB_HEADS * tlen
    n_seqs = SAMPLE_SEQS if b % SAMPLE_SEQS == 0 else 1
    bias_rows = jnp.broadcast_to(jnp.repeat(sb_bias, tlen)[:, None], (rows, PAGE_SIZE)).astype(F32)
    seq = lambda a: a.reshape(b, tlen, SB_DIM)
    tok = pl.BlockSpec((n_seqs, tlen, SB_DIM), lambda i, pt: (i, 0, 0))
    page = lambda s, j: pl.BlockSpec((1, 1, SB_DIM, PAGE_SIZE),
                                     lambda i, pt, s=s, j=j: (layer, pt[i * n_seqs + s, j], 0, 0))
    pages = [page(s, j) for s in range(n_seqs) for j in range(n_pages)]
    grid_spec = pltpu.PrefetchScalarGridSpec(
        num_scalar_prefetch=1,
        grid=(b // n_seqs,),
        in_specs=[tok, pl.BlockSpec((rows, PAGE_SIZE), lambda i, pt: (0, 0)), tok, tok] + pages + pages,
        out_specs=tok,
    )
    out = pl.pallas_call(
        functools.partial(_attn_sample_kernel, tlen=tlen, n_pages=n_pages),
        grid_spec=grid_spec,
        out_shape=jax.ShapeDtypeStruct((b, tlen, SB_DIM), F32),
        compiler_params=_cparams("parallel"),
        name="attn_sample",
    )(page_table, seq(q), bias_rows, seq(k_new), seq(v_new),
      *([cache_kt] * len(pages)), *([cache_vt] * len(pages)))
    return out.reshape(b * tlen, SB_DIM)


def _tail_kernel(x_ref, y_ref, bonus_ref, g_ref, cb_ref, cc_ref, cx_ref, *refs, tiles_per_seq):
    n_before = 4 if tiles_per_seq is None else 3
    before_refs = refs[:n_before]
    (osb_ref, p_ref, gnw_ref, gnb_ref, convw_ref, ones_ref, wout_ref, gmlp_ref, wup_ref, wdown_ref,
     gple_ref, wgate_ref, wproj_ref, o_ref) = refs[n_before:]
    y = y_ref[...]
    mu = _dot_exact_rhs(y, ones_ref[...])
    yc = y - mu
    var = _dot_exact_rhs(yc * yc, ones_ref[...])
    yn = yc * lax.rsqrt(var + GN_EPS) * gnw_ref[...] + gnb_ref[...]
    o_rw = (yn + bonus_ref[...]) * g_ref[...]
    cw = convw_ref[...]
    u = cc_ref[...] * cx_ref[...]
    if tiles_per_seq is None:
        cc1_ref, cx1_ref, cc2_ref, cx2_ref = before_refs
        u1 = cc1_ref[...] * cx1_ref[...]
        u2 = cc2_ref[...] * cx2_ref[...]
    else:
        cch_ref, cxh_ref, state_ref = before_refs
        u_halo = cch_ref[...] * cxh_ref[...]
        is_first = pl.program_id(0) % tiles_per_seq == 0
        u1 = _rows_before(u, u_halo, state_ref[0], is_first, 1)
        u2 = _rows_before(u, u_halo, state_ref[0], is_first, 2)
    o_cv = cb_ref[...] * (cw[0:1] * u2 + cw[1:2] * u1 + cw[2:3] * u)
    mix = _dot(o_rw, wout_ref[0:RW_DIM, :])
    mix = mix + _dot(o_cv, wout_ref[RW_DIM:RW_DIM + CONV_DIM, :])
    w_sb = wout_ref[RW_DIM + CONV_DIM:, :]
    if len(osb_ref.shape) == 2:
        mix = mix + _dot(osb_ref[...], w_sb)
    else:
        mix = mix + jnp.concatenate(
            [_dot(osb_ref[0, s].T, w_sb) for s in range(osb_ref.shape[1])], axis=0)
    x1 = x_ref[...] + mix
    h2 = _rmsnorm_rows(x1, gmlp_ref[...]).astype(BF16)
    x2 = x1
    for c in range(0, wup_ref.shape[1], MLP_CHUNK):
        up = jnp.dot(h2, wup_ref[:, c:c + MLP_CHUNK], preferred_element_type=F32)
        x2 = x2 + _dot(jnp.square(jnp.maximum(up, 0.0)), wdown_ref[c:c + MLP_CHUNK, :])
    gate = _sigmoid(_dot(_rmsnorm_rows(x2, gple_ref[...]), wgate_ref[...]))
    o_ref[...] = x2 + gate * _dot(p_ref[...], wproj_ref[...])


def _tail(x2d, y, bonus, g, p_cv, conv_prev, seq_len, o_sb, p3d, layer, lw, wts, tm):
    n, d = x2d.shape
    b = n // seq_len
    ones = _head_block_ones(RW_DIM, 1.0 / HEAD_DIM)
    vec = lambda a: a.reshape(1, -1)
    row = lambda w, col=0: pl.BlockSpec((tm, w), lambda i, f, col=col: (i, col))
    full = lambda a: pl.BlockSpec(a.shape, lambda i, f: (0,) * a.ndim)
    once = lambda a: pl.BlockSpec(a.shape, lambda i, f: (0,) * a.ndim, pipeline_mode=pl.Buffered(1))
    gnw, gnb, gmlp, gple = vec(lw["gn_w"]), vec(lw["gn_b"]), vec(lw["g_mlp"]), vec(lw["g_ple"])
    if o_sb.ndim == 2:
        osb_spec = row(SB_DIM)
    else:
        _, nblk, _, blk = o_sb.shape
        osb_tiles = nblk * blk // tm
        osb_spec = pl.BlockSpec((1, tm // blk, SB_DIM, blk), lambda i, f: (i // osb_tiles, i % osb_tiles, 0, 0))
    if seq_len % tm == 0:
        tiles_per_seq = seq_len // tm
        halo = lambda col: pl.BlockSpec((8, CONV_DIM), lambda i, f: (jnp.maximum(i * (tm // 8) - 1, 0), col))
        before = [p_cv, p_cv, conv_prev]
        before_specs = [halo(1), halo(2),
                        pl.BlockSpec((1, CONV_K - 1, CONV_DIM), lambda i, f: (i // tiles_per_seq, 0, 0))]
    else:
        tiles_per_seq = None
        p_cv3 = p_cv.reshape(b, seq_len, 3 * CONV_DIM)
        state_rows = lambda s: jnp.concatenate([jnp.zeros_like(s), s, jnp.ones_like(s)], axis=-1)
        p_cv1 = _shift_rows(p_cv3, state_rows(conv_prev[:, 1:2]), 1).reshape(n, 3 * CONV_DIM)
        p_cv2 = _shift_rows(p_cv3, state_rows(conv_prev), 2).reshape(n, 3 * CONV_DIM)
        before = [p_cv1, p_cv1, p_cv2, p_cv2]
        before_specs = [row(CONV_DIM, 1), row(CONV_DIM, 2), row(CONV_DIM, 1), row(CONV_DIM, 2)]
    in_specs = [row(d), row(RW_DIM), row(RW_DIM), row(RW_DIM),
                row(CONV_DIM, 0), row(CONV_DIM, 1), row(CONV_DIM, 2)] + before_specs + [
                osb_spec, pl.BlockSpec((None, tm, p3d.shape[2]), lambda i, f: (layer, i, 0)),
                full(gnw), full(gnb), full(lw["conv_w"]), full(ones), once(wts["w_out"]), full(gmlp),
                once(wts["w_up"]), once(wts["w_down"]),
                full(gple), once(wts["w_ple_gate"]), once(wts["w_ple_proj"])]
    return pl.pallas_call(
        functools.partial(_tail_kernel, tiles_per_seq=tiles_per_seq),
        grid=(n // tm, 1),
        in_specs=in_specs,
        out_specs=pl.BlockSpec((tm, d), lambda i, f: (i, 0)),
        out_shape=jax.ShapeDtypeStruct((n, d), F32),
        compiler_params=_cparams("parallel", "arbitrary"),
        name="tail",
    )(x2d, y, bonus, g, p_cv, p_cv, p_cv, *before, o_sb, p3d,
      gnw, gnb, lw["conv_w"], ones, wts["w_out"], gmlp, wts["w_up"], wts["w_down"],
      gple, wts["w_ple_gate"], wts["w_ple_proj"])


def _conv_state_kernel(cv_ref, o_ref):
    o_ref[...] = cv_ref[:, :, CONV_DIM:2 * CONV_DIM] * cv_ref[:, :, 2 * CONV_DIM:3 * CONV_DIM]


def _conv_state(p_cv3):
    b, t, c = p_cv3.shape
    return pl.pallas_call(
        _conv_state_kernel,
        grid=(1,),
        in_specs=[pl.BlockSpec((b, 8, c), lambda i: (0, t // 8 - 1, 0))],
        out_specs=pl.BlockSpec((b, 8, CONV_DIM), lambda i: (0, 0, 0)),
        out_shape=jax.ShapeDtypeStruct((b, 8, CONV_DIM), F32),
        name="conv_state",
    )(p_cv3)


def _shift_rows(x3, first_rows, n):
    return jnp.concatenate([first_rows, x3[:, :-n]], axis=1)


def _layer(x2d, p3d, layer, b, t, shift_prev, wkv_prev, conv_prev, kv_prev, attn_fn, att_blk, lw, wts, tm,
           wkv_nseq, wkv_tlen, wkv_groups):
    n, d = x2d.shape
    p_rw, p_cv, k_all, v_all, r, lwd, k2, v, av, bv, g, bonus, *attn_ops = _in_proj(
        x2d, shift_prev, lw, wts["w_in"], tm, t, att_blk, layer, p3d.shape[0], kv_prev)
    k_new, v_new = k_all[layer], v_all[layer]
    p_rw3 = p_rw.reshape(b, t, p_rw.shape[1])
    grp = lambda a: a.reshape(-1, (n // (b // wkv_nseq)), RW_DIM) if wkv_nseq > 1 else a.reshape(b, t, RW_DIM)
    y, wkv_new = _wkv(grp(r), grp(lwd), grp(k2), grp(v), grp(av), grp(bv), wkv_prev, wkv_nseq, wkv_tlen,
                   wkv_groups)
    y = y.reshape(n, RW_DIM)

    o_sb = attn_fn(k_new, v_new, *attn_ops)

    x_out = _tail(x2d, y, bonus, g, p_cv, conv_prev, t, o_sb, p3d, layer, lw, wts, tm)
    conv_new = _conv_state(p_cv.reshape(b, t, 3 * CONV_DIM))[:, 8 - (CONV_K - 1):]
    return x_out, (k_all, v_all), wkv_new, p_rw3[:, -1], conv_new


def _run_trunk(x, p, shift, wkv, conv, attn_builder, att_blk, params, bf16_w, tm,
               wkv_nseq, wkv_tlen, wkv_groups):
    b, t, d = x.shape
    depth = p.shape[0]
    x2d = x.reshape(b * t, d)
    wkvs, shifts, convs = [], [], []
    kv_all = None
    for l in range(depth):
        lw = {name: arr[l] for name, arr in params.items()}
        wts = {name: arr[l] for name, arr in bf16_w.items()}
        x2d, kv_all, wkv_new, shift_new, conv_new = _layer(
            x2d, p.reshape(depth, b * t, -1), l, b, t, shift[l], wkv[l].reshape(b, RW_DIM, HEAD_DIM), conv[l],
            kv_all, attn_builder(l, lw), att_blk, lw, wts, tm, wkv_nseq, wkv_tlen, wkv_groups)
        wkvs.append(wkv_new.reshape(b, RW_HEADS, HEAD_DIM, HEAD_DIM))
        shifts.append(shift_new)
        convs.append(conv_new)
    heads = lambda a: a.reshape(depth, b, t, SB_HEADS, HEAD_DIM)
    return (x2d.reshape(b, t, d), heads(kv_all[0]), heads(kv_all[1]), jnp.stack(wkvs), jnp.stack(shifts),
            jnp.stack(convs))


def kernel(x_prompt, x_sample, cache_k, cache_v, state_wkv, state_shift, state_conv, page_table,
           p_prompt, p_sample, g_mix, w_in, mu_shift, w0, w2, a0, a2, g2, k_k, k_a, r_k,
           gn_w, gn_b, conv_w, q_gain, k_gain, sb_bias, w_out, g_mlp, w_up, w_down, g_ple,
           w_ple_gate, w_ple_proj):
    params = dict(g_mix=g_mix, mu_shift=mu_shift, w0=w0, w2=w2, a0=a0, a2=a2, g2=g2,
                  k_k=k_k, k_a=k_a, r_k=r_k, gn_w=gn_w, gn_b=gn_b, conv_w=conv_w,
                  q_gain=q_gain, k_gain=k_gain, sb_bias=sb_bias, g_mlp=g_mlp, g_ple=g_ple)
    bf16_w = dict(w_in=w_in.astype(BF16), w_out=w_out.astype(BF16), w_up=w_up.astype(BF16),
                  w_down=w_down.astype(BF16), w_ple_gate=w_ple_gate.astype(BF16),
                  w_ple_proj=w_ple_proj.astype(BF16))
    depth = w_in.shape[0]
    b, t, d = x_prompt.shape
    db, dt, _ = x_sample.shape
    rw_proj = state_shift.shape[-1]

    n_phys = cache_k.shape[1]
    page_view = lambda c: c.transpose(0, 1, 3, 4, 2).reshape(depth, n_phys, SB_DIM, PAGE_SIZE)
    cache_kt, cache_vt = page_view(cache_k), page_view(cache_v)

    def prompt_attn(l, lw):
        return lambda k, v, q_t, k_bf16, v_t: _attn_prompt(q_t, k_bf16, v_t, lw["sb_bias"], b, t)

    def sample_attn(l, lw):
        return lambda k, v, q: _attn_sample(q, k, v, cache_kt, cache_vt, l, page_table, lw["sb_bias"], db, dt)

    dtp = x_prompt.dtype
    shift0 = jnp.zeros((depth, b, rw_proj), dtp)
    wkv0 = jnp.zeros((depth, b, RW_HEADS, HEAD_DIM, HEAD_DIM), dtp)
    conv0 = jnp.zeros((depth, b, CONV_K - 1, CONV_DIM), dtp)
    (y_prompt, k_prompt, v_prompt, wkv_prompt, shift_prompt, conv_prompt) = _run_trunk(
        x_prompt, p_prompt, shift0, wkv0, conv0, prompt_attn, min(ATT_BLOCK, t), params, bf16_w,
        min(512, t), 1, WKV_ROWS, b)
    seqs_per_group = WKV_ROWS // dt
    (y_sample, k_sample, v_sample, wkv_sample, shift_sample, conv_sample) = _run_trunk(
        x_sample, p_sample, state_shift, state_wkv, state_conv, sample_attn, None, params, bf16_w,
        min(512, db * dt), seqs_per_group, dt, min(2, db // seqs_per_group))
    return (y_prompt, y_sample, k_prompt, v_prompt, wkv_prompt, shift_prompt, conv_prompt,
            k_sample, v_sample, wkv_sample, shift_sample, conv_sample)
```

```python
import functools

import jax
import jax.numpy as jnp
from jax import lax
from jax.experimental import pallas as pl
from jax.experimental.pallas import tpu as pltpu

F32 = jnp.float32
BF16 = jnp.bfloat16

HEAD_DIM = 64
RW_HEADS = 4
RW_DIM = RW_HEADS * HEAD_DIM
LORA_WA = 128
CONV_DIM = 256
CONV_K = 3
SB_HEADS = 8
SB_DIM = SB_HEADS * HEAD_DIM
PAGE_SIZE = 128
RMS_EPS = 1e-6
GN_EPS = 64e-5
KK_EPS = 1e-12
LOG2E = 1.4426950408889634

WKV_ROWS = 64
ATT_BLOCK = 256
ATT_HEADS = 4
MLP_CHUNK = 1024
VMEM_LIMIT_BYTES = 56 * 1024 * 1024


def _cparams(*sem):
    return pltpu.CompilerParams(dimension_semantics=sem, vmem_limit_bytes=VMEM_LIMIT_BYTES)


def _dot(a, b):
    return jnp.dot(a.astype(BF16), b.astype(BF16), preferred_element_type=F32)


def _dg(a, b, ca, cb):
    return lax.dot_general(a.astype(BF16), b.astype(BF16), (((ca,), (cb,)), ((), ())),
                           preferred_element_type=F32)


def _split2(a):
    hi = a.astype(BF16)
    lo = (a - hi.astype(F32)).astype(BF16)
    return hi, lo


def _dot_exact_rhs(a, b_bf16):
    h1 = a.astype(BF16)
    r1 = a - h1.astype(F32)
    h2 = r1.astype(BF16)
    h3 = (r1 - h2.astype(F32)).astype(BF16)
    out = jnp.dot(h1, b_bf16, preferred_element_type=F32)
    out = out + jnp.dot(h2, b_bf16, preferred_element_type=F32)
    out = out + jnp.dot(h3, b_bf16, preferred_element_type=F32)
    return out


def _dot_exact_lhs(a_bf16, b):
    h1 = b.astype(BF16)
    r1 = b - h1.astype(F32)
    h2 = r1.astype(BF16)
    h3 = (r1 - h2.astype(F32)).astype(BF16)
    out = jnp.dot(a_bf16, h1, preferred_element_type=F32)
    out = out + jnp.dot(a_bf16, h2, preferred_element_type=F32)
    out = out + jnp.dot(a_bf16, h3, preferred_element_type=F32)
    return out


def _softplus(x):
    return jnp.maximum(x, 0.0) + jnp.log1p(jnp.exp(-jnp.abs(x)))


def _sigmoid(x):
    return 1.0 / (1.0 + jnp.exp(-x))


def _rmsnorm_rows(x, g):
    ms = jnp.mean(x * x, axis=-1, keepdims=True)
    return x * lax.rsqrt(ms + RMS_EPS) * g


def _head_block_ones(width, scale):
    i = jnp.arange(width) // HEAD_DIM
    return jnp.where(i[:, None] == i[None, :], scale, 0.0).astype(BF16)


def _rows_before(cur, halo, first_rows, is_first, n):
    row = lax.broadcasted_iota(jnp.int32, cur.shape, 0)
    out = pltpu.roll(cur, n, 0)
    n_halo, n_first = halo.shape[0], first_rows.shape[0]
    for k in range(n):
        before = jnp.where(is_first, first_rows[n_first - n + k:n_first - n + k + 1],
                           halo[n_halo - n + k:n_halo - n + k + 1])
        out = jnp.where(row == k, before, out)
    return out


def _rwkv_heads(p, prev, mu_ref, w0_ref, a0_ref, kk_ref, ka_ref, rk_ref, wwa_ref, g2_ref, ones_ref,
                r_ref, lw_ref, k_ref, v_ref, av_ref, bv_ref, g_ref, bonus_ref):
    xs = p + (prev - p) * mu_ref[...]
    r = xs[:, 0:RW_DIM]
    k = xs[:, RW_DIM:2 * RW_DIM]
    v = xs[:, 2 * RW_DIM:3 * RW_DIM]
    d = xs[:, 3 * RW_DIM:3 * RW_DIM + LORA_WA]
    dg = xs[:, 3 * RW_DIM + LORA_WA:]
    lane = lax.broadcasted_iota(jnp.int32, d.shape, 1)
    dwa = jnp.where(lane < LORA_WA // 2, jnp.tanh(d), d)
    wa = _dot(dwa, wwa_ref[...])
    w_log = -_softplus(-(w0_ref[...] + wa[:, 0:RW_DIM])) - 0.5
    lw_ref[...] = -jnp.exp(w_log)
    a = _sigmoid(a0_ref[...] + wa[:, RW_DIM:])
    g_ref[...] = _dot(_sigmoid(dg), g2_ref[...])
    kk = k * kk_ref[...]
    ss = _dot_exact_rhs(kk * kk, ones_ref[...])
    kk = kk * lax.rsqrt(ss + KK_EPS)
    k2 = k * (1.0 + (a - 1.0) * ka_ref[...])
    r_ref[...] = r
    k_ref[...] = k2
    v_ref[...] = v
    av_ref[...] = -kk
    bv_ref[...] = kk * a
    bonus_ref[...] = _dot_exact_rhs(r * k2 * rk_ref[...], ones_ref[...]) * v


N_RWKV_PARAMS = 9
N_RWKV_OUTS = 8


def _in_proj_kernel(x_ref, g_ref, w_ref, ebd_ref, qg_ref, kg_ref, state_ref, *refs,
                    att_blk, seq_len, tiles_per_seq, layer):
    rwkv_params = refs[:N_RWKV_PARAMS]
    refs = refs[N_RWKV_PARAMS:]
    kv_prev = ()
    if layer > 0:
        kv_prev, refs = refs[:2], refs[2:]
    rw_ref, cv_ref, k_ref, v_ref = refs[:4]
    rwkv_outs = refs[4:4 + N_RWKV_OUTS]
    attn_refs = refs[4 + N_RWKV_OUTS:]
    for j in range(k_ref.shape[0]):
        if j != layer:
            k_ref[j] = kv_prev[0][j] if kv_prev else jnp.zeros(k_ref.shape[1:], F32)
            v_ref[j] = kv_prev[1][j] if kv_prev else jnp.zeros(v_ref.shape[1:], F32)
    h = _rmsnorm_rows(x_ref[...], g_ref[...]).astype(BF16)
    c0 = rw_ref.shape[1]
    c1 = c0 + cv_ref.shape[1]
    p = jnp.dot(h, w_ref[:, 0:c0], preferred_element_type=F32)
    rw_ref[...] = p
    rows = p.shape[0]
    if tiles_per_seq is None:
        nseq = rows // seq_len
        r_i = lax.broadcasted_iota(jnp.int32, (rows, nseq), 0)
        s_i = lax.broadcasted_iota(jnp.int32, (rows, nseq), 1)
        starts = jnp.where(r_i == s_i * seq_len, 1.0, 0.0).astype(BF16)
        t_local = lax.broadcasted_iota(jnp.int32, p.shape, 0) % seq_len
        prev = jnp.where(t_local == 0, _dot_exact_lhs(starts, state_ref[...]), pltpu.roll(p, 1, 0))
    else:
        attn_refs, last_ref = attn_refs[:-1], attn_refs[-1]

        @pl.when(pl.program_id(0) == 0)
        def _():
            last_ref[...] = jnp.zeros_like(last_ref)

        prev = _rows_before(p, last_ref[...], state_ref[0], pl.program_id(0) % tiles_per_seq == 0, 1)
        last_ref[...] = p[rows - last_ref.shape[0]:rows]
    _rwkv_heads(p, prev, *rwkv_params, *rwkv_outs)
    cv_ref[...] = jnp.dot(h, w_ref[:, c0:c1], preferred_element_type=F32)

    def head_norm(t, gain):
        ms = jnp.dot((t * t).astype(BF16), ebd_ref[...], preferred_element_type=F32)
        return t * lax.rsqrt(ms + RMS_EPS) * gain

    q = jnp.dot(h, w_ref[:, c1:c1 + SB_DIM], preferred_element_type=F32)
    q = head_norm(q, qg_ref[...]) * (HEAD_DIM ** -0.5)
    k = jnp.dot(h, w_ref[:, c1 + SB_DIM:c1 + 2 * SB_DIM], preferred_element_type=F32)
    k = head_norm(k, kg_ref[...])
    k_ref[layer] = k
    v = jnp.dot(h, w_ref[:, c1 + 2 * SB_DIM:c1 + 3 * SB_DIM], preferred_element_type=F32)
    v_ref[layer] = v
    if att_blk is None:
        (q_ref,) = attn_refs
        q_ref[...] = q
    else:
        qt_ref, kb_ref, vt_ref = attn_refs
        kb_ref[...] = k.astype(BF16)
        q_t = (q * LOG2E).T
        v_t = v.T
        for s in range(qt_ref.shape[1]):
            qt_ref[0, s] = q_t[:, s * att_blk:(s + 1) * att_blk].astype(BF16)
            vt_ref[0, s] = v_t[:, s * att_blk:(s + 1) * att_blk].astype(BF16)


def _in_proj(x2d, shift_prev, lw, w_in_bf16, tm, seq_len, att_blk, layer, depth, kv_prev):
    n, d = x2d.shape
    cols = w_in_bf16.shape[1]
    rw_cols = cols - 3 * CONV_DIM - 3 * SB_DIM
    ebd = _head_block_ones(SB_DIM, 1.0 / HEAD_DIM)
    qg = jnp.tile(lw["q_gain"], SB_HEADS)[None, :]
    kg = jnp.tile(lw["k_gain"], SB_HEADS)[None, :]
    row = lambda w: pl.BlockSpec((tm, w), lambda i: (i, 0))
    full = lambda a: pl.BlockSpec(a.shape, lambda i: (0,) * a.ndim)
    g2d = lw["g_mix"][None, :]
    w_lora = lw["w2"].shape[0]
    a_lora = lw["a2"].shape[0]
    wwa = jnp.zeros((w_lora + a_lora, 2 * RW_DIM), F32)
    wwa = wwa.at[:w_lora, :RW_DIM].set(lw["w2"]).at[w_lora:, RW_DIM:].set(lw["a2"]).astype(BF16)
    vec = lambda a: a.reshape(1, -1)
    rwkv_params = [vec(lw["mu_shift"]), vec(lw["w0"]), vec(lw["a0"]), vec(lw["k_k"]), vec(lw["k_a"]),
                   vec(lw["r_k"]), wwa, lw["g2"].astype(BF16), _head_block_ones(RW_DIM, 1.0)]
    assert len(rwkv_params) == N_RWKV_PARAMS
    scratch = []
    if seq_len % tm == 0:
        tiles_per_seq = seq_len // tm
        state, state_spec = shift_prev[:, None, :], pl.BlockSpec((1, 1, rw_cols), lambda i: (i // tiles_per_seq, 0, 0))
        scratch = [pltpu.VMEM((8, rw_cols), F32)]
    else:
        tiles_per_seq = None
        state, state_spec = shift_prev, pl.BlockSpec((tm // seq_len, rw_cols), lambda i: (i, 0))
    kv_spec = pl.BlockSpec((depth, tm, SB_DIM), lambda i: (0, i, 0))
    kv_shape = jax.ShapeDtypeStruct((depth, n, SB_DIM), F32)
    out_specs = [row(rw_cols), row(3 * CONV_DIM), kv_spec, kv_spec] + [row(RW_DIM)] * N_RWKV_OUTS
    out_shape = [jax.ShapeDtypeStruct((n, rw_cols), F32), jax.ShapeDtypeStruct((n, 3 * CONV_DIM), F32),
                 kv_shape, kv_shape]
    out_shape += [jax.ShapeDtypeStruct((n, RW_DIM), F32)] * N_RWKV_OUTS
    inputs = [x2d, g2d, w_in_bf16, ebd, qg, kg, state, *rwkv_params]
    in_specs = [row(d), full(g2d), full(w_in_bf16), full(ebd), full(qg), full(kg), state_spec]
    in_specs += [full(a) for a in rwkv_params]
    aliases = {}
    if layer > 0:
        aliases = {len(inputs): 2, len(inputs) + 1: 3}
        inputs += list(kv_prev)
        in_specs += [kv_spec, kv_spec]
    if att_blk is None:
        out_specs.append(row(SB_DIM))
        out_shape.append(jax.ShapeDtypeStruct((n, SB_DIM), F32))
    else:
        att_tiles = seq_len // tm
        blks = tm // att_blk
        t_spec = pl.BlockSpec((1, blks, SB_DIM, att_blk), lambda i: (i // att_tiles, i % att_tiles, 0, 0))
        t_shape = jax.ShapeDtypeStruct((n // seq_len, seq_len // att_blk, SB_DIM, att_blk), BF16)
        out_specs += [t_spec, row(SB_DIM), t_spec]
        out_shape += [t_shape, jax.ShapeDtypeStruct((n, SB_DIM), BF16), t_shape]
    return pl.pallas_call(
        functools.partial(_in_proj_kernel, att_blk=att_blk, seq_len=seq_len, tiles_per_seq=tiles_per_seq,
                          layer=layer),
        grid=(n // tm,),
        in_specs=in_specs,
        out_specs=out_specs,
        out_shape=out_shape,
        scratch_shapes=scratch,
        input_output_aliases=aliases,
        compiler_params=_cparams("arbitrary"),
        name="in_proj",
    )(*inputs)


def _wkv_kernel(r_ref, lw_ref, k_ref, v_ref, a_ref, b_ref, s0_ref, y_ref, sout_ref, s_ref, *, nseq, tlen):
    P = WKV_ROWS
    HP = RW_HEADS * P
    c = pl.program_id(1)
    own_head = (lax.broadcasted_iota(jnp.int32, (RW_DIM, RW_DIM), 0) // HEAD_DIM
                == lax.broadcasted_iota(jnp.int32, (RW_DIM, RW_DIM), 1) // HEAD_DIM)

    @pl.when(c == 0)
    def _():
        for j in range(s_ref.shape[0]):
            s_ref[j] = jnp.where(own_head, jnp.concatenate([s0_ref[j]] * RW_HEADS, axis=1), 0.0)

    prow = lax.broadcasted_iota(jnp.int32, (P, P), 0)
    pcol = lax.broadcasted_iota(jnp.int32, (P, P), 1)
    same_seq = (prow // tlen) == (pcol // tlen)
    tri_incl = jnp.where(same_seq & (pcol <= prow), 1.0, 0.0).astype(BF16)
    last_sel = jnp.where(same_seq & (pcol == (prow // tlen) * tlen + (tlen - 1)), 1.0, 0.0).astype(BF16)
    srow = lax.broadcasted_iota(jnp.int32, (HP, HP), 0)
    scol = lax.broadcasted_iota(jnp.int32, (HP, HP), 1)
    same_blk = ((srow // tlen) == (scol // tlen))
    m_strict = same_blk & (scol < srow)
    m_incl = same_blk & (scol <= srow)
    eye = jnp.where(srow == scol, 1.0, 0.0)
    hrow = lax.broadcasted_iota(jnp.int32, (HP, RW_DIM), 0)
    head_mask = (hrow // P) == (lax.broadcasted_iota(jnp.int32, (HP, RW_DIM), 1) // HEAD_DIM)
    seq_of_row = (hrow % P) // tlen

    def stack(x):
        return jnp.where(head_mask, jnp.concatenate([x] * RW_HEADS, axis=0), 0.0)

    groups = range(r_ref.shape[0])
    lw = [lw_ref[g] for g in groups]
    cw = [_dot_exact_lhs(tri_incl, x) for x in lw]
    cw_end = [_dot_exact_lhs(last_sel, x) for x in cw]
    w_inv = [jnp.exp(-x) for x in cw]
    w_toend = [jnp.exp(e - x) for e, x in zip(cw_end, cw)]
    a_s = [stack(a_ref[g] * jnp.exp(cw[g] - lw[g])) for g in groups]
    r_s = [stack(r_ref[g] * jnp.exp(cw[g])) for g in groups]
    b_s = [stack(b_ref[g] * w_inv[g]) for g in groups]
    k_s = [stack(k_ref[g] * w_inv[g]) for g in groups]
    v_s = [stack(v_ref[g]) for g in groups]
    bend_s = [stack(b_ref[g] * w_toend[g]) for g in groups]
    kend_s = [stack(k_ref[g] * w_toend[g]) for g in groups]

    n_ab = [jnp.where(m_strict, _dg(a_s[g], b_s[g], 1, 1), 0.0) for g in groups]
    a_ak = [jnp.where(m_strict, _dg(a_s[g], k_s[g], 1, 1), 0.0) for g in groups]
    a_rb = [jnp.where(m_incl, _dg(r_s[g], b_s[g], 1, 1), 0.0) for g in groups]
    a_rk = [jnp.where(m_incl, _dg(r_s[g], k_s[g], 1, 1), 0.0) for g in groups]

    t_inv = [eye + x for x in n_ab]
    pw = n_ab
    span = 2
    while span < tlen:
        pw = [_dot(x, x) for x in pw]
        t_inv = [t + _dot(t, x) for t, x in zip(t_inv, pw)]
        span *= 2

    ap_s = [_dot(t_inv[g], a_s[g]) for g in groups]
    akv = [_dot(a_ak[g], v_s[g]) for g in groups]
    u1_s = [_dot(t_inv[g], akv[g]) for g in groups]
    y_s = [_dot(a_rb[g], u1_s[g]) + _dot(a_rk[g], v_s[g]) for g in groups]
    rp_s = [r_s[g] + _dot(a_rb[g], ap_s[g]) for g in groups]
    wc_row = [jnp.exp(x) for x in cw_end]

    for i in range(nseq):
        if nseq == 1:
            sel = lambda x: x
        else:
            sel = lambda x: jnp.where(seq_of_row == i, x, 0.0)
        s_i = [s_ref[g * nseq + i] for g in groups]
        u_i = [_dg(sel(ap_s[g]), s_i[g], 1, 1) + sel(u1_s[g]) for g in groups]
        y_s = [y_s[g] + _dg(sel(rp_s[g]), s_i[g], 1, 1) for g in groups]
        for g in groups:
            wc_i = wc_row[g][i * tlen:i * tlen + 1, :]
            s_ref[g * nseq + i] = (s_i[g] * wc_i + _dg(u_i[g], sel(bend_s[g]), 0, 0)
                                   + _dg(sel(v_s[g]), sel(kend_s[g]), 0, 0))
    for g in groups:
        y = y_s[g][0:P]
        for h in range(1, RW_HEADS):
            y = y + y_s[g][h * P:(h + 1) * P]
        y_ref[g] = y

    @pl.when(c == pl.num_programs(1) - 1)
    def _():
        for j in range(s_ref.shape[0]):
            own = jnp.where(own_head, s_ref[j], 0.0)
            folded = own[:, 0:HEAD_DIM]
            for h in range(1, RW_HEADS):
                folded = folded + own[:, h * HEAD_DIM:(h + 1) * HEAD_DIM]
            sout_ref[j] = folded


def _wkv(r, lw, k, v, av, bv, s0, nseq, tlen, groups):
    g, rows, _ = r.shape
    n_chunks = rows // WKV_ROWS
    blk = pl.BlockSpec((groups, WKV_ROWS, RW_DIM), lambda i, c: (i, c, 0))
    sblk = pl.BlockSpec((groups * nseq, RW_DIM, HEAD_DIM), lambda i, c: (i, 0, 0))
    return pl.pallas_call(
        functools.partial(_wkv_kernel, nseq=nseq, tlen=tlen),
        grid=(g // groups, n_chunks),
        in_specs=[blk] * 6 + [sblk],
        out_specs=[blk, sblk],
        out_shape=[jax.ShapeDtypeStruct(r.shape, F32), jax.ShapeDtypeStruct(s0.shape, F32)],
        scratch_shapes=[pltpu.VMEM((groups * nseq, RW_DIM, RW_DIM), F32)],
        compiler_params=_cparams("parallel", "arbitrary"),
        name="wkv",
    )(r, lw, k, v, av, bv, s0)


ATT_NEG = -1e30


def _attn_prompt_kernel(bias_ref, tq_ref, tj_ref, qt_ref, k_ref, vt_ref, o_ref,
                        z_ref, sp_ref, lb_ref, w_ref, zinit_ref, carry_ref):
    grp = pl.program_id(1)
    nk, gw, blk = qt_ref.shape[1:]
    n_tiles = nk * (nk + 1) // 2
    row_head = lax.broadcasted_iota(jnp.int32, (gw, blk), 0) // HEAD_DIM
    krow = lax.broadcasted_iota(jnp.int32, (blk + 8, blk), 0)
    kcol = lax.broadcasted_iota(jnp.int32, (blk + 8, blk), 1)
    neg_sums = jnp.where((kcol > krow) | (krow >= blk), -1.0, 0.0).astype(BF16)
    hidden = (lax.broadcasted_iota(jnp.int32, (blk, blk), 0) >= lax.broadcasted_iota(jnp.int32, (blk, blk), 1))
    for h in range(ATT_HEADS):
        bias2 = bias_ref[grp * ATT_HEADS + h] * LOG2E
        zinit_ref[0, h] = jnp.full((blk, blk), bias2, F32)
        zinit_ref[1, h] = jnp.where(hidden, ATT_NEG, bias2)
    for ref in (z_ref, sp_ref, lb_ref, w_ref, carry_ref, o_ref):
        ref[...] = jnp.zeros_like(ref)

    def step(i, slot_new):
        slot_old = 1 - slot_new

        f4 = jnp.clip(i - 3, 0, n_tiles - 1)
        q4 = tq_ref[f4]
        j4 = tj_ref[f4]
        keep = jnp.where(j4 == q4, 0.0, 1.0)
        for h in range(ATT_HEADS):
            rows = slice(h * HEAD_DIM, (h + 1) * HEAD_DIM)
            pv = jnp.dot(vt_ref[0, j4, rows, :], w_ref[slot_old, h], preferred_element_type=F32)
            o_ref[0, q4, rows, :] = o_ref[0, q4, rows, :] * keep + pv

        f3 = jnp.clip(i - 2, 0, n_tiles - 1)
        valid3 = (i >= 2) & (i <= n_tiles + 1)
        first3 = tj_ref[f3] == tq_ref[f3]
        null3 = jnp.where(valid3, 0.0, ATT_NEG)
        count3 = jnp.where(valid3, 1.0, 0.0)
        for h in range(ATT_HEADS):
            sums = jnp.dot(neg_sums, sp_ref[slot_old, h], preferred_element_type=F32)
            carry = jnp.where(first3, 0.0, carry_ref[h])
            w_ref[slot_new, h] = jnp.exp2(lb_ref[slot_old, h] + sums[0:blk] + (carry + null3)).astype(BF16)
            carry_ref[h] = carry + sums[blk:blk + 1] * count3

        for h in range(ATT_HEADS):
            z = z_ref[slot_old, h]
            sp = jnp.maximum(z, 0.0) + jnp.log(1.0 + jnp.exp2(-jnp.abs(z))) * LOG2E
            sp_ref[slot_new, h] = sp.astype(BF16)
            lb_ref[slot_new, h] = z - sp

        f1 = jnp.minimum(i, n_tiles - 1)
        q1 = tq_ref[f1]
        j1 = tj_ref[f1]
        diag1 = jnp.where(j1 == q1, 1, 0)
        k_blk = k_ref[0, pl.ds(pl.multiple_of(j1 * blk, blk), blk), :]
        q_grp = qt_ref[0, q1]
        for h in range(ATT_HEADS):
            q_h = jnp.where(row_head == h, q_grp, jnp.zeros_like(q_grp))
            z_ref[slot_new, h] = jnp.dot(k_blk, q_h, preferred_element_type=F32) + zinit_ref[diag1, h]

    def body(pair, _):
        step(2 * pair, 0)
        step(2 * pair + 1, 1)
        return 0

    lax.fori_loop(0, (n_tiles + 3 + 1) // 2, body, 0)


def _attn_prompt(q_t, k_bf16, v_t, sb_bias, b, t):
    _, nk, _, blk = q_t.shape
    gw = ATT_HEADS * HEAD_DIM
    k3 = k_bf16.reshape(b, t, SB_DIM)
    tiles = [(qi, j) for qi in range(nk) for j in range(qi, -1, -1)]
    tile_q = jnp.array([qi for qi, _ in tiles], jnp.int32)
    tile_j = jnp.array([j for _, j in tiles], jnp.int32)
    blocks = pl.BlockSpec((1, nk, gw, blk), lambda i, g, *_: (i, 0, g, 0))
    grid_spec = pltpu.PrefetchScalarGridSpec(
        num_scalar_prefetch=3,
        grid=(b, SB_HEADS // ATT_HEADS),
        in_specs=[blocks, pl.BlockSpec((1, t, gw), lambda i, g, *_: (i, 0, g)), blocks],
        out_specs=blocks,
        scratch_shapes=[pltpu.VMEM((2, ATT_HEADS, blk, blk), F32), pltpu.VMEM((2, ATT_HEADS, blk, blk), BF16),
                        pltpu.VMEM((2, ATT_HEADS, blk, blk), F32), pltpu.VMEM((2, ATT_HEADS, blk, blk), BF16),
                        pltpu.VMEM((2, ATT_HEADS, blk, blk), F32), pltpu.VMEM((ATT_HEADS, 1, blk), F32)],
    )
    return pl.pallas_call(
        _attn_prompt_kernel,
        grid_spec=grid_spec,
        out_shape=jax.ShapeDtypeStruct((b, nk, SB_DIM, blk), F32),
        compiler_params=_cparams("parallel", "parallel"),
        name="attn_prompt",
    )(sb_bias, tile_q, tile_j, q_t, k3, v_t)


SAMPLE_SEQS = 2


def _attn_sample_kernel(pt_ref, q_ref, bias_ref, kn_ref, vn_ref, *refs, tlen, n_pages):
    n_seqs = q_ref.shape[0]
    o_ref = refs[2 * n_seqs * n_pages]
    for s in range(n_seqs):
        _attn_sample_sequence(q_ref.at[s], bias_ref, kn_ref.at[s], vn_ref.at[s],
                              refs[s * n_pages:(s + 1) * n_pages],
                              refs[(n_seqs + s) * n_pages:(n_seqs + s + 1) * n_pages], o_ref.at[s], tlen)


def _attn_sample_sequence(q_ref, bias_ref, kn_ref, vn_ref, k_refs, v_refs, o_ref, tlen):
    n_pages = len(k_refs)
    rows = SB_HEADS * tlen
    rhead = lax.broadcasted_iota(jnp.int32, (rows, SB_DIM), 0) // tlen
    lhead = lax.broadcasted_iota(jnp.int32, (rows, SB_DIM), 1) // HEAD_DIM
    head_mask = rhead == lhead
    q_s = jnp.where(head_mask, jnp.concatenate([q_ref[...]] * SB_HEADS, axis=0), 0.0).astype(BF16)
    jrow = lax.broadcasted_iota(jnp.int32, (PAGE_SIZE, 2 * PAGE_SIZE), 0)
    jcol = lax.broadcasted_iota(jnp.int32, (PAGE_SIZE, 2 * PAGE_SIZE), 1)
    tri_ones = jnp.where((jrow > jcol) | (jcol >= PAGE_SIZE), 1.0, 0.0).astype(BF16)
    bias = bias_ref[...]

    def scores(z, mask):
        z = z + bias
        sp = jnp.maximum(z, 0.0) + jnp.log(1.0 + jnp.exp(-jnp.abs(z)))
        lk = -sp
        if mask is not None:
            lk = jnp.where(mask, lk, 0.0)
        lk_hi, lk_lo = _split2(lk)
        sums = (jnp.dot(lk_hi, tri_ones, preferred_element_type=F32)
                + jnp.dot(lk_lo, tri_ones, preferred_element_type=F32))
        return (z - sp) + sums[:, 0:PAGE_SIZE], sums[:, PAGE_SIZE:]

    pad = jnp.zeros((PAGE_SIZE - tlen, SB_DIM), F32)
    k_new = jnp.concatenate([kn_ref[...], pad], axis=0)
    v_new = jnp.concatenate([vn_ref[...], pad], axis=0)
    new_mask = (lax.broadcasted_iota(jnp.int32, (rows, PAGE_SIZE), 1)
                < lax.broadcasted_iota(jnp.int32, (rows, PAGE_SIZE), 0) % tlen)
    order = list(reversed(range(n_pages)))
    zs = [_dg(q_s, k_new, 1, 1)] + [_dot(q_s, k_refs[j][0, 0]) for j in order]
    parts = [scores(zs[0], new_mask)] + [scores(z, None) for z in zs[1:]]
    carry = jnp.zeros((rows, PAGE_SIZE), F32)
    ws = []
    for n, (logw, total) in enumerate(parts):
        w = jnp.exp(logw + carry)
        ws.append(jnp.where(new_mask, w, 0.0) if n == 0 else w)
        carry = carry + total
    acc = _dot(ws[0], v_new)
    for w, j in zip(ws[1:], order):
        acc = acc + _dg(w, v_refs[j][0, 0], 1, 1)
    acc = jnp.where(head_mask, acc, 0.0)
    out = acc[0:tlen]
    for h in range(1, SB_HEADS):
        out = out + acc[h * tlen:(h + 1) * tlen]
    o_ref[...] = out


def _attn_sample(q, k_new, v_new, cache_kt, cache_vt, layer, page_table, sb_bias, b, tlen):
    n_pages = page_table.shape[1]
    rows = SB_HEADS * tlen
    n_seqs = SAMPLE_SEQS if b % SAMPLE_SEQS == 0 else 1
    bias_rows = jnp.broadcast_to(jnp.repeat(sb_bias, tlen)[:, None], (rows, PAGE_SIZE)).astype(F32)
    seq = lambda a: a.reshape(b, tlen, SB_DIM)
    tok = pl.BlockSpec((n_seqs, tlen, SB_DIM), lambda i, pt: (i, 0, 0))
    page = lambda s, j: pl.BlockSpec((1, 1, SB_DIM, PAGE_SIZE),
                                     lambda i, pt, s=s, j=j: (layer, pt[i * n_seqs + s, j], 0, 0))
    pages = [page(s, j) for s in range(n_seqs) for j in range(n_pages)]
    grid_spec = pltpu.PrefetchScalarGridSpec(
        num_scalar_prefetch=1,
        grid=(b // n_seqs,),
        in_specs=[tok, pl.BlockSpec((rows, PAGE_SIZE), lambda i, pt: (0, 0)), tok, tok] + pages + pages,
        out_specs=tok,
    )
    out = pl.pallas_call(
        functools.partial(_attn_sample_kernel, tlen=tlen, n_pages=n_pages),
        grid_spec=grid_spec,
        out_shape=jax.ShapeDtypeStruct((b, tlen, SB_DIM), F32),
        compiler_params=_cparams("parallel"),
        name="attn_sample",
    )(page_table, seq(q), bias_rows, seq(k_new), seq(v_new),
      *([cache_kt] * len(pages)), *([cache_vt] * len(pages)))
    return out.reshape(b * tlen, SB_DIM)


def _tail_kernel(x_ref, y_ref, bonus_ref, g_ref, cb_ref, cc_ref, cx_ref, *refs, tiles_per_seq):
    n_before = 4 if tiles_per_seq is None else 3
    before_refs = refs[:n_before]
    (osb_ref, p_ref, gnw_ref, gnb_ref, convw_ref, ones_ref, wout_ref, gmlp_ref, wup_ref, wdown_ref,
     gple_ref, wgate_ref, wproj_ref, o_ref) = refs[n_before:]
    y = y_ref[...]
    mu = _dot_exact_rhs(y, ones_ref[...])
    yc = y - mu
    var = _dot_exact_rhs(yc * yc, ones_ref[...])
    yn = yc * lax.rsqrt(var + GN_EPS) * gnw_ref[...] + gnb_ref[...]
    o_rw = (yn + bonus_ref[...]) * g_ref[...]
    cw = convw_ref[...]
    u = cc_ref[...] * cx_ref[...]
    if tiles_per_seq is None:
        cc1_ref, cx1_ref, cc2_ref, cx2_ref = before_refs
        u1 = cc1_ref[...] * cx1_ref[...]
        u2 = cc2_ref[...] * cx2_ref[...]
    else:
        cch_ref, cxh_ref, state_ref = before_refs
        u_halo = cch_ref[...] * cxh_ref[...]
        is_first = pl.program_id(0) % tiles_per_seq == 0
        u1 = _rows_before(u, u_halo, state_ref[0], is_first, 1)
        u2 = _rows_before(u, u_halo, state_ref[0], is_first, 2)
    o_cv = cb_ref[...] * (cw[0:1] * u2 + cw[1:2] * u1 + cw[2:3] * u)
    mix = _dot(o_rw, wout_ref[0:RW_DIM, :])
    mix = mix + _dot(o_cv, wout_ref[RW_DIM:RW_DIM + CONV_DIM, :])
    w_sb = wout_ref[RW_DIM + CONV_DIM:, :]
    if len(osb_ref.shape) == 2:
        mix = mix + _dot(osb_ref[...], w_sb)
    else:
        mix = mix + jnp.concatenate(
            [_dot(osb_ref[0, s].T, w_sb) for s in range(osb_ref.shape[1])], axis=0)
    x1 = x_ref[...] + mix
    h2 = _rmsnorm_rows(x1, gmlp_ref[...]).astype(BF16)
    x2 = x1
    for c in range(0, wup_ref.shape[1], MLP_CHUNK):
        up = jnp.dot(h2, wup_ref[:, c:c + MLP_CHUNK], preferred_element_type=F32)
        x2 = x2 + _dot(jnp.square(jnp.maximum(up, 0.0)), wdown_ref[c:c + MLP_CHUNK, :])
    gate = _sigmoid(_dot(_rmsnorm_rows(x2, gple_ref[...]), wgate_ref[...]))
    o_ref[...] = x2 + gate * _dot(p_ref[...], wproj_ref[...])


def _tail(x2d, y, bonus, g, p_cv, conv_prev, seq_len, o_sb, p3d, layer, lw, wts, tm):
    n, d = x2d.shape
    b = n // seq_len
    ones = _head_block_ones(RW_DIM, 1.0 / HEAD_DIM)
    vec = lambda a: a.reshape(1, -1)
    row = lambda w, col=0: pl.BlockSpec((tm, w), lambda i, f, col=col: (i, col))
    full = lambda a: pl.BlockSpec(a.shape, lambda i, f: (0,) * a.ndim)
    once = lambda a: pl.BlockSpec(a.shape, lambda i, f: (0,) * a.ndim, pipeline_mode=pl.Buffered(1))
    gnw, gnb, gmlp, gple = vec(lw["gn_w"]), vec(lw["gn_b"]), vec(lw["g_mlp"]), vec(lw["g_ple"])
    if o_sb.ndim == 2:
        osb_spec = row(SB_DIM)
    else:
        _, nblk, _, blk = o_sb.shape
        osb_tiles = nblk * blk // tm
        osb_spec = pl.BlockSpec((1, tm // blk, SB_DIM, blk), lambda i, f: (i // osb_tiles, i % osb_tiles, 0, 0))
    if seq_len % tm == 0:
        tiles_per_seq = seq_len // tm
        halo = lambda col: pl.BlockSpec((8, CONV_DIM), lambda i, f: (jnp.maximum(i * (tm // 8) - 1, 0), col))
        before = [p_cv, p_cv, conv_prev]
        before_specs = [halo(1), halo(2),
                        pl.BlockSpec((1, CONV_K - 1, CONV_DIM), lambda i, f: (i // tiles_per_seq, 0, 0))]
    else:
        tiles_per_seq = None
        p_cv3 = p_cv.reshape(b, seq_len, 3 * CONV_DIM)
        state_rows = lambda s: jnp.concatenate([jnp.zeros_like(s), s, jnp.ones_like(s)], axis=-1)
        p_cv1 = _shift_rows(p_cv3, state_rows(conv_prev[:, 1:2]), 1).reshape(n, 3 * CONV_DIM)
        p_cv2 = _shift_rows(p_cv3, state_rows(conv_prev), 2).reshape(n, 3 * CONV_DIM)
        before = [p_cv1, p_cv1, p_cv2, p_cv2]
        before_specs = [row(CONV_DIM, 1), row(CONV_DIM, 2), row(CONV_DIM, 1), row(CONV_DIM, 2)]
    in_specs = [row(d), row(RW_DIM), row(RW_DIM), row(RW_DIM),
                row(CONV_DIM, 0), row(CONV_DIM, 1), row(CONV_DIM, 2)] + before_specs + [
                osb_spec, pl.BlockSpec((None, tm, p3d.shape[2]), lambda i, f: (layer, i, 0)),
                full(gnw), full(gnb), full(lw["conv_w"]), full(ones), once(wts["w_out"]), full(gmlp),
                once(wts["w_up"]), once(wts["w_down"]),
                full(gple), once(wts["w_ple_gate"]), once(wts["w_ple_proj"])]
    return pl.pallas_call(
        functools.partial(_tail_kernel, tiles_per_seq=tiles_per_seq),
        grid=(n // tm, 1),
        in_specs=in_specs,
        out_specs=pl.BlockSpec((tm, d), lambda i, f: (i, 0)),
        out_shape=jax.ShapeDtypeStruct((n, d), F32),
        compiler_params=_cparams("parallel", "arbitrary"),
        name="tail",
    )(x2d, y, bonus, g, p_cv, p_cv, p_cv, *before, o_sb, p3d,
      gnw, gnb, lw["conv_w"], ones, wts["w_out"], gmlp, wts["w_up"], wts["w_down"],
      gple, wts["w_ple_gate"], wts["w_ple_proj"])


def _conv_state_kernel(cv_ref, o_ref):
    o_ref[...] = cv_ref[:, :, CONV_DIM:2 * CONV_DIM] * cv_ref[:, :, 2 * CONV_DIM:3 * CONV_DIM]


def _conv_state(p_cv3):
    b, t, c = p_cv3.shape
    return pl.pallas_call(
        _conv_state_kernel,
        grid=(1,),
        in_specs=[pl.BlockSpec((b, 8, c), lambda i: (0, t // 8 - 1, 0))],
        out_specs=pl.BlockSpec((b, 8, CONV_DIM), lambda i: (0, 0, 0)),
        out_shape=jax.ShapeDtypeStruct((b, 8, CONV_DIM), F32),
        name="conv_state",
    )(p_cv3)


def _shift_rows(x3, first_rows, n):
    return jnp.concatenate([first_rows, x3[:, :-n]], axis=1)


def _layer(x2d, p3d, layer, b, t, shift_prev, wkv_prev, conv_prev, kv_prev, attn_fn, att_blk, lw, wts, tm,
           wkv_nseq, wkv_tlen, wkv_groups):
    n, d = x2d.shape
    p_rw, p_cv, k_all, v_all, r, lwd, k2, v, av, bv, g, bonus, *attn_ops = _in_proj(
        x2d, shift_prev, lw, wts["w_in"], tm, t, att_blk, layer, p3d.shape[0], kv_prev)
    k_new, v_new = k_all[layer], v_all[layer]
    p_rw3 = p_rw.reshape(b, t, p_rw.shape[1])
    grp = lambda a: a.reshape(-1, (n // (b // wkv_nseq)), RW_DIM) if wkv_nseq > 1 else a.reshape(b, t, RW_DIM)
    y, wkv_new = _wkv(grp(r), grp(lwd), grp(k2), grp(v), grp(av), grp(bv), wkv_prev, wkv_nseq, wkv_tlen,
                   wkv_groups)
    y = y.reshape(n, RW_DIM)

    o_sb = attn_fn(k_new, v_new, *attn_ops)

    x_out = _tail(x2d, y, bonus, g, p_cv, conv_prev, t, o_sb, p3d, layer, lw, wts, tm)
    conv_new = _conv_state(p_cv.reshape(b, t, 3 * CONV_DIM))[:, 8 - (CONV_K - 1):]
    return x_out, (k_all, v_all), wkv_new, p_rw3[:, -1], conv_new


def _run_trunk(x, p, shift, wkv, conv, attn_builder, att_blk, params, bf16_w, tm,
               wkv_nseq, wkv_tlen, wkv_groups):
    b, t, d = x.shape
    depth = p.shape[0]
    x2d = x.reshape(b * t, d)
    wkvs, shifts, convs = [], [], []
    kv_all = None
    for l in range(depth):
        lw = {name: arr[l] for name, arr in params.items()}
        wts = {name: arr[l] for name, arr in bf16_w.items()}
        x2d, kv_all, wkv_new, shift_new, conv_new = _layer(
            x2d, p.reshape(depth, b * t, -1), l, b, t, shift[l], wkv[l].reshape(b, RW_DIM, HEAD_DIM), conv[l],
            kv_all, attn_builder(l, lw), att_blk, lw, wts, tm, wkv_nseq, wkv_tlen, wkv_groups)
        wkvs.append(wkv_new.reshape(b, RW_HEADS, HEAD_DIM, HEAD_DIM))
        shifts.append(shift_new)
        convs.append(conv_new)
    heads = lambda a: a.reshape(depth, b, t, SB_HEADS, HEAD_DIM)
    return (x2d.reshape(b, t, d), heads(kv_all[0]), heads(kv_all[1]), jnp.stack(wkvs), jnp.stack(shifts),
            jnp.stack(convs))


def kernel(x_prompt, x_sample, cache_k, cache_v, state_wkv, state_shift, state_conv, page_table,
           p_prompt, p_sample, g_mix, w_in, mu_shift, w0, w2, a0, a2, g2, k_k, k_a, r_k,
           gn_w, gn_b, conv_w, q_gain, k_gain, sb_bias, w_out, g_mlp, w_up, w_down, g_ple,
           w_ple_gate, w_ple_proj):
    params = dict(g_mix=g_mix, mu_shift=mu_shift, w0=w0, w2=w2, a0=a0, a2=a2, g2=g2,
                  k_k=k_k, k_a=k_a, r_k=r_k, gn_w=gn_w, gn_b=gn_b, conv_w=conv_w,
                  q_gain=q_gain, k_gain=k_gain, sb_bias=sb_bias, g_mlp=g_mlp, g_ple=g_ple)
    bf16_w = dict(w_in=w_in.astype(BF16), w_out=w_out.astype(BF16), w_up=w_up.astype(BF16),
                  w_down=w_down.astype(BF16), w_ple_gate=w_ple_gate.astype(BF16),
                  w_ple_proj=w_ple_proj.astype(BF16))
    depth = w_in.shape[0]
    b, t, d = x_prompt.shape
    db, dt, _ = x_sample.shape
    rw_proj = state_shift.shape[-1]

    n_phys = cache_k.shape[1]
    page_view = lambda c: c.transpose(0, 1, 3, 4, 2).reshape(depth, n_phys, SB_DIM, PAGE_SIZE)
    cache_kt, cache_vt = page_view(cache_k), page_view(cache_v)

    def prompt_attn(l, lw):
        return lambda k, v, q_t, k_bf16, v_t: _attn_prompt(q_t, k_bf16, v_t, lw["sb_bias"], b, t)

    def sample_attn(l, lw):
        return lambda k, v, q: _attn_sample(q, k, v, cache_kt, cache_vt, l, page_table, lw["sb_bias"], db, dt)

    dtp = x_prompt.dtype
    shift0 = jnp.zeros((depth, b, rw_proj), dtp)
    wkv0 = jnp.zeros((depth, b, RW_HEADS, HEAD_DIM, HEAD_DIM), dtp)
    conv0 = jnp.zeros((depth, b, CONV_K - 1, CONV_DIM), dtp)
    (y_prompt, k_prompt, v_prompt, wkv_prompt, shift_prompt, conv_prompt) = _run_trunk(
        x_prompt, p_prompt, shift0, wkv0, conv0, prompt_attn, min(ATT_BLOCK, t), params, bf16_w,
        min(512, t), 1, WKV_ROWS, b)
    seqs_per_group = WKV_ROWS // dt
    (y_sample, k_sample, v_sample, wkv_sample, shift_sample, conv_sample) = _run_trunk(
        x_sample, p_sample, state_shift, state_wkv, state_conv, sample_attn, None, params, bf16_w,
        min(512, db * dt), seqs_per_group, dt, min(2, db // seqs_per_group))
    return (y_prompt, y_sample, k_prompt, v_prompt, wkv_prompt, shift_prompt, conv_prompt,
            k_sample, v_sample, wkv_sample, shift_sample, conv_sample)
```
